```python
import math
import jax, jax.numpy as jnp
from jax import lax
import numpy as np

D_MODEL = 1024
BATCH = 4
SEQ = 4096
DEPTH = 1

N_HEADS_A = 8
HEAD_DIM_A = 64
ROT_DIM = HEAD_DIM_A // 4
ROPE_THETA = 500000.0
Q_BLOCK = 128
ATTN_QK_W = N_HEADS_A * 2 * HEAD_DIM_A
ATTN_V_W = N_HEADS_A * 2 * HEAD_DIM_A
EXPAND = 2
D_INNER = EXPAND * D_MODEL
HEAD_DIM_S = 64
N_HEADS_S = D_INNER // HEAD_DIM_S
N_GROUPS = 8
HEADS_PER_GROUP = N_HEADS_S // N_GROUPS
D_STATE = 128
CONV_SSD = 5
CHUNK = 128
XBC_W = D_INNER + 2 * N_GROUPS * D_STATE
D_FF = 2816
CONV_FFN = 3
ALPHA = (2 * DEPTH) ** 0.25
BETA = (8 * DEPTH) ** -0.25
LN_EPS = 1e-5
RMS_EPS = 1e-5
Q_END = ATTN_QK_W
K_END = Q_END + ATTN_QK_W
V_END = K_END + ATTN_V_W
Z_END = V_END + D_INNER
XBC_END = Z_END + XBC_W
DT_END = XBC_END + 2 * N_HEADS_S
D_IN_PROJ = DT_END + 2 * D_MODEL
IN_SPLITS = [Q_END, K_END, V_END, Z_END, XBC_END, DT_END]

kernel_name = 'hybrid_diffattn_ssd_convffn_deepnorm'


def layer_norm(x, g, b):
    xf = x.astype(jnp.float32)
    mu = jnp.mean(xf, axis=-1, keepdims=True)
    var = jnp.mean(jnp.square(xf - mu), axis=-1, keepdims=True)
    return ((xf - mu) * lax.rsqrt(var + LN_EPS) * g + b).astype(x.dtype)


def rms_norm(x, w):
    xf = x.astype(jnp.float32)
    return (xf * lax.rsqrt(jnp.mean(jnp.square(xf), axis=-1, keepdims=True) + RMS_EPS) * w).astype(x.dtype)


def dwconv(x, w, b):
    k = w.shape[0]
    out = lax.conv_general_dilated(x, w[:, None, :].astype(x.dtype), window_strides=(1,),
                                   padding=[(k // 2, k // 2)],
                                   dimension_numbers=('NWC', 'WIO', 'NWC'),
                                   feature_group_count=x.shape[-1])
    return out + b.astype(x.dtype)


def partial_rotary(t, cos, sin):
    half = ROT_DIM // 2
    t1 = t[..., :half]
    t2 = t[..., half:ROT_DIM]
    return jnp.concatenate([t1 * cos - t2 * sin, t2 * cos + t1 * sin, t[..., ROT_DIM:]], axis=-1)


def diff_attention(q, k, v, lam, subln_w, lam_init):
    bsz, s = q.shape[0], q.shape[1]
    nb = s // Q_BLOCK
    kt = k.transpose(0, 2, 3, 1, 4)
    vt = v.transpose(0, 2, 1, 3)
    qb = (q * (HEAD_DIM_A ** -0.5)).reshape(bsz, nb, Q_BLOCK, N_HEADS_A, 2, HEAD_DIM_A)
    qb = qb.transpose(1, 0, 3, 4, 2, 5)

    def block(qi):
        sc = jnp.einsum('bhmqd,bhmkd->bhmqk', qi, kt).astype(jnp.float32)
        p = jax.nn.softmax(sc, axis=-1)
        a = p[:, :, 0] - lam * p[:, :, 1]
        return jnp.einsum('bhqk,bhkv->bhqv', a.astype(vt.dtype), vt)

    o = lax.map(block, qb)
    o = o.transpose(1, 0, 3, 2, 4).reshape(bsz, s, N_HEADS_A, 2 * HEAD_DIM_A)
    o = rms_norm(o, subln_w) * (1.0 - lam_init)
    return o.reshape(bsz, s, ATTN_V_W)


def ssd_chunked(xh, dt, a, bm, cm):
    bsz, s = xh.shape[0], xh.shape[1]
    nc = s // CHUNK
    xdt = (xh * dt[..., None]).reshape(bsz, nc, CHUNK, N_GROUPS, HEADS_PER_GROUP, HEAD_DIM_S)
    da = (dt * a).reshape(bsz, nc, CHUNK, N_GROUPS, HEADS_PER_GROUP)
    bm = bm.reshape(bsz, nc, CHUNK, N_GROUPS, D_STATE)
    cm = cm.reshape(bsz, nc, CHUNK, N_GROUPS, D_STATE)
    a_cs = jnp.cumsum(da, axis=2)
    seg = a_cs[:, :, :, None] - a_cs[:, :, None, :]
    mask = jnp.tril(jnp.ones((CHUNK, CHUNK), dtype=bool))[:, :, None, None]
    decay = jnp.exp(jnp.where(mask, seg, -jnp.inf))
    cb = jnp.einsum('bclgn,bcsgn->bclsg', cm, bm)
    y_diag = jnp.einsum('bclsge,bcsgep->bclgep', cb[..., None] * decay, xdt)
    decay_states = jnp.exp(a_cs[:, :, -1:] - a_cs)
    states = jnp.einsum('bclgn,bclge,bclgep->bcgepn', bm, decay_states, xdt)
    chunk_decay = jnp.exp(a_cs[:, :, -1])

    def step(carry, inp):
        st, dec = inp
        return carry * dec[..., None, None] + st, carry

    init = jnp.zeros((bsz, N_GROUPS, HEADS_PER_GROUP, HEAD_DIM_S, D_STATE), xdt.dtype)
    _, prev = lax.scan(step, init, (jnp.swapaxes(states, 0, 1), jnp.swapaxes(chunk_decay, 0, 1)))
    prev = jnp.swapaxes(prev, 0, 1)
    y_off = jnp.einsum('bclgn,bcgepn,bclge->bclgep', cm, prev, jnp.exp(a_cs))
    return (y_diag + y_off).reshape(bsz, s, N_GROUPS, HEADS_PER_GROUP, HEAD_DIM_S)


def ssd_mixer(z, xbc, dt_raw, conv_w, conv_b, dt_bias, a_log, d_skip, norm_w):
    bsz, s = z.shape[0], z.shape[1]
    f32 = jnp.float32
    xbc = jax.nn.silu(dwconv(xbc, conv_w, conv_b))
    xs, bm, cm = jnp.split(xbc, [D_INNER, D_INNER + N_GROUPS * D_STATE], axis=-1)
    xh = xs.astype(f32).reshape(bsz, s, N_GROUPS, HEADS_PER_GROUP, HEAD_DIM_S)
    bm = bm.astype(f32).reshape(bsz, s, N_GROUPS, D_STATE)
    cm = cm.astype(f32).reshape(bsz, s, N_GROUPS, D_STATE)
    dt = jax.nn.softplus(dt_raw.astype(f32).reshape(bsz, s, 2, N_HEADS_S) + dt_bias.astype(f32))
    dt = dt.reshape(bsz, s, 2, N_GROUPS, HEADS_PER_GROUP)
    a = -jnp.exp(a_log.astype(f32)).reshape(2, N_GROUPS, HEADS_PER_GROUP)
    y_f = ssd_chunked(xh, dt[:, :, 0], a[0], bm, cm)
    flip = lambda t: jnp.flip(t, axis=1)
    y_b = flip(ssd_chunked(flip(xh), flip(dt[:, :, 1]), a[1], flip(bm), flip(cm)))
    y = y_f + y_b + d_skip.astype(f32).reshape(N_GROUPS, HEADS_PER_GROUP)[..., None] * xh
    gw = HEADS_PER_GROUP * HEAD_DIM_S
    y = y.reshape(bsz, s, N_GROUPS, gw) * jax.nn.silu(z.astype(f32)).reshape(bsz, s, N_GROUPS, gw)
    y = y * lax.rsqrt(jnp.mean(jnp.square(y), axis=-1, keepdims=True) + RMS_EPS) * norm_w.astype(f32).reshape(N_GROUPS, gw)
    return y.reshape(bsz, s, D_INNER).astype(z.dtype)


def setup_inputs(seed: int = 0) -> dict:
    key = jax.random.key(seed)
    ks = jax.random.split(key, 32)
    f32 = jnp.float32
    L = DEPTH

    def nrm(k, shape, scale):
        return jax.random.normal(k, shape, f32) * scale

    x = nrm(ks[0], (BATCH, SEQ, D_MODEL), 1.0)
    col_scale = jnp.ones((D_IN_PROJ,), f32).at[K_END:V_END].set(BETA)
    w_in = nrm(ks[1], (L, D_MODEL, D_IN_PROJ), D_MODEL ** -0.5) * col_scale
    b_gate = nrm(ks[2], (L, 2 * D_MODEL), 0.02)
    lambda_q1 = nrm(ks[3], (L, HEAD_DIM_A), 0.1)
    lambda_k1 = nrm(ks[4], (L, HEAD_DIM_A), 0.1)
    lambda_q2 = nrm(ks[5], (L, HEAD_DIM_A), 0.1)
    lambda_k2 = nrm(ks[6], (L, HEAD_DIM_A), 0.1)
    attn_subln_w = 1.0 + nrm(ks[7], (L, 2 * HEAD_DIM_A), 0.02)
    conv_ssd_w = nrm(ks[8], (L, CONV_SSD, XBC_W), CONV_SSD ** -0.5)
    conv_ssd_b = nrm(ks[9], (L, XBC_W), 0.02)
    u = jax.random.uniform(ks[10], (L, 2, N_HEADS_S), f32)
    dt0 = jnp.exp(u * (math.log(0.1) - math.log(0.001)) + math.log(0.001))
    dt_bias = dt0 + jnp.log(-jnp.expm1(-dt0))
    a_log = jnp.log(jax.random.uniform(ks[11], (L, 2, N_HEADS_S), f32, 1.0, 16.0))
    d_skip = 1.0 + nrm(ks[12], (L, N_HEADS_S), 0.02)
    ssd_norm_w = 1.0 + nrm(ks[13], (L, D_INNER), 0.02)
    w_proj_attn = nrm(ks[14], (L, ATTN_V_W, D_MODEL), ATTN_V_W ** -0.5)
    w_proj_ssd = nrm(ks[15], (L, D_INNER, D_MODEL), D_INNER ** -0.5)
    w_out = nrm(ks[16], (L, D_MODEL, D_MODEL), D_MODEL ** -0.5) * BETA
    ln1_g = 1.0 + nrm(ks[17], (L, D_MODEL), 0.02)
    ln1_b = nrm(ks[18], (L, D_MODEL), 0.02)
    w_up = nrm(ks[19], (L, D_MODEL, 2 * D_FF), D_MODEL ** -0.5) * BETA
    conv_ffn_w = nrm(ks[20], (L, CONV_FFN, 2 * D_FF), CONV_FFN ** -0.5)
    conv_ffn_b = nrm(ks[21], (L, 2 * D_FF), 0.02)
    w_down = nrm(ks[22], (L, D_FF, D_MODEL), D_FF ** -0.5) * BETA
    ln2_g = 1.0 + nrm(ks[23], (L, D_MODEL), 0.02)
    ln2_b = nrm(ks[24], (L, D_MODEL), 0.02)
    return {'x': x, 'w_in': w_in, 'b_gate': b_gate, 'lambda_q1': lambda_q1, 'lambda_k1': lambda_k1,
            'lambda_q2': lambda_q2, 'lambda_k2': lambda_k2, 'attn_subln_w': attn_subln_w,
            'conv_ssd_w': conv_ssd_w, 'conv_ssd_b': conv_ssd_b, 'dt_bias': dt_bias, 'a_log': a_log,
            'd_skip': d_skip, 'ssd_norm_w': ssd_norm_w, 'w_proj_attn': w_proj_attn,
            'w_proj_ssd': w_proj_ssd, 'w_out': w_out, 'ln1_g': ln1_g, 'ln1_b': ln1_b, 'w_up': w_up,
            'conv_ffn_w': conv_ffn_w, 'conv_ffn_b': conv_ffn_b, 'w_down': w_down,
            'ln2_g': ln2_g, 'ln2_b': ln2_b}


def reference(x, w_in, b_gate, lambda_q1, lambda_k1, lambda_q2, lambda_k2, attn_subln_w,
              conv_ssd_w, conv_ssd_b, dt_bias, a_log, d_skip, ssd_norm_w, w_proj_attn,
              w_proj_ssd, w_out, ln1_g, ln1_b, w_up, conv_ffn_w, conv_ffn_b, w_down, ln2_g, ln2_b):
    f32 = jnp.float32
    bsz, s = x.shape[0], x.shape[1]
    pos = jnp.arange(s, dtype=f32)
    inv_freq = jnp.power(ROPE_THETA, -jnp.arange(0, ROT_DIM, 2, dtype=f32) / ROT_DIM)
    ang = pos[:, None] * inv_freq[None, :]
    cos = jnp.cos(ang)[None, :, None, None, :].astype(x.dtype)
    sin = jnp.sin(ang)[None, :, None, None, :].astype(x.dtype)

    for l in range(DEPTH):
        lam_init = 0.8 - 0.6 * math.exp(-0.3 * l)
        proj = x @ w_in[l]
        q, k, v, z, xbc, dt_raw, gate_logits = jnp.split(proj, IN_SPLITS, axis=-1)
        q = partial_rotary(q.reshape(bsz, s, N_HEADS_A, 2, HEAD_DIM_A), cos, sin)
        k = partial_rotary(k.reshape(bsz, s, N_HEADS_A, 2, HEAD_DIM_A), cos, sin)
        v = v.reshape(bsz, s, N_HEADS_A, 2 * HEAD_DIM_A)
        lam = (jnp.exp(jnp.sum(lambda_q1[l].astype(f32) * lambda_k1[l].astype(f32)))
               - jnp.exp(jnp.sum(lambda_q2[l].astype(f32) * lambda_k2[l].astype(f32))) + lam_init)
        attn = diff_attention(q, k, v, lam, attn_subln_w[l], lam_init)
        ssd = ssd_mixer(z, xbc, dt_raw, conv_ssd_w[l], conv_ssd_b[l], dt_bias[l], a_log[l],
                        d_skip[l], ssd_norm_w[l])
        gates = jax.nn.sigmoid(gate_logits + b_gate[l])
        g_a, g_s = jnp.split(gates, 2, axis=-1)
        merged = g_a * (attn @ w_proj_attn[l]) + g_s * (ssd @ w_proj_ssd[l])
        x = layer_norm(ALPHA * x + merged @ w_out[l], ln1_g[l], ln1_b[l])
        h = dwconv(x @ w_up[l], conv_ffn_w[l], conv_ffn_b[l])
        hg, hu = jnp.split(h, 2, axis=-1)
        x = layer_norm(ALPHA * x + (jax.nn.silu(hg) * hu) @ w_down[l], ln2_g[l], ln2_b[l])
    return x
```

```python
import functools
import math

import jax
import jax.numpy as jnp
from jax import lax
from jax.experimental import pallas as pl
from jax.experimental.pallas import tpu as pltpu

F32 = jnp.float32
BF16 = jnp.bfloat16

D_MODEL = 1024
N_HEADS_A = 8
HEAD_DIM_A = 64
ROT_DIM = HEAD_DIM_A // 4
ROPE_THETA = 500000.0
ATTN_W = N_HEADS_A * 2 * HEAD_DIM_A
D_INNER = 2 * D_MODEL
HEAD_DIM_S = 64
N_HEADS_S = D_INNER // HEAD_DIM_S
N_GROUPS = 8
HEADS_PER_GROUP = N_HEADS_S // N_GROUPS
GROUP_W = HEADS_PER_GROUP * HEAD_DIM_S
D_STATE = 128
CONV_SSD = 5
CHUNK = 128
D_FF = 2816
DEPTH = 1
ALPHA = (2 * DEPTH) ** 0.25
LN_EPS = 1e-5
RMS_EPS = 1e-5
LAM_INIT = 0.8 - 0.6 * math.exp(-0.3 * 0)

LANES = 128
SUBLANES = 8
VMEM_LIMIT = 56 * 1024 * 1024

Q_OFF, K_OFF, V_OFF = 0, ATTN_W, 2 * ATTN_W
Z_OFF = 3 * ATTN_W
XS_OFF = Z_OFF + D_INNER
B_OFF = XS_OFF + D_INNER
C_OFF = B_OFF + N_GROUPS * D_STATE
GATE_OFF = C_OFF + N_GROUPS * D_STATE
PROJ_W = GATE_OFF + 2 * D_MODEL


def _dot(a, b):
    return jnp.dot(a, b, preferred_element_type=F32)


def _dot_nt(a, b):
    return lax.dot_general(a, b, (((1,), (1,)), ((), ())), preferred_element_type=F32)


def _dot_exact(a, b):
    return jnp.dot(a, b, preferred_element_type=F32, precision=lax.Precision.HIGHEST)


def _in_proj_kernel(x_ref, w_ref, wdt_ref, cos_ref, sa_ref, sb_ref, o_ref, dt_ref, xb_ref, *, n_rot_tiles):
    j = pl.program_id(1)

    @pl.when(j == 0)
    def _():
        xb = x_ref[...].astype(BF16)
        xb_ref[...] = xb
        dt_ref[...] = _dot_nt(wdt_ref[...], xb)

    acc = _dot(xb_ref[...], w_ref[...])

    @pl.when(j < n_rot_tiles)
    def _():
        c, sa, sb = cos_ref[...], sa_ref[...], sb_ref[...]
        for h in range(acc.shape[1] // LANES):
            a = acc[:, h * LANES:(h + 1) * LANES]
            up = pltpu.roll(a, LANES - ROT_DIM // 2, axis=1)
            dn = pltpu.roll(a, ROT_DIM // 2, axis=1)
            o_ref[:, h * LANES:(h + 1) * LANES] = (a * c + up * sa + dn * sb).astype(o_ref.dtype)

    @pl.when(j >= n_rot_tiles)
    def _():
        o_ref[...] = acc.astype(o_ref.dtype)


def _in_proj(x2, w_main, w_dt_t, cos_t, sa_t, sb_t, *, batch, seq, tm=1024, tn=1024):
    t = x2.shape[0]
    n_seq_tiles = seq // tm
    n_dt = w_dt_t.shape[0]
    kern = functools.partial(_in_proj_kernel, n_rot_tiles=(2 * ATTN_W) // tn)
    return pl.pallas_call(
        kern,
        grid=(t // tm, PROJ_W // tn),
        in_specs=[
            pl.BlockSpec((tm, D_MODEL), lambda i, j: (i, 0)),
            pl.BlockSpec((D_MODEL, tn), lambda i, j: (0, j)),
            pl.BlockSpec((n_dt, D_MODEL), lambda i, j: (0, 0)),
            pl.BlockSpec((tm, LANES), lambda i, j: (i % n_seq_tiles, 0)),
            pl.BlockSpec((tm, LANES), lambda i, j: (i % n_seq_tiles, 0)),
            pl.BlockSpec((tm, LANES), lambda i, j: (i % n_seq_tiles, 0)),
        ],
        out_specs=[
            pl.BlockSpec((tm, tn), lambda i, j: (i, j)),
            pl.BlockSpec((None, n_dt, tm), lambda i, j: (i // n_seq_tiles, 0, i % n_seq_tiles)),
        ],
        out_shape=[
            jax.ShapeDtypeStruct((t, PROJ_W), BF16),
            jax.ShapeDtypeStruct((batch, n_dt, seq), F32),
        ],
        scratch_shapes=[pltpu.VMEM((tm, D_MODEL), BF16)],
        compiler_params=pltpu.CompilerParams(
            dimension_semantics=("parallel", "arbitrary"), vmem_limit_bytes=VMEM_LIMIT),
        name="in_proj",
    )(x2, w_main, w_dt_t, cos_t, sa_t, sb_t)


def _attn_kernel(q_ref, k_ref, v_ref, lam_ref, sw_ref, o_ref, v1_ref, s1_ref, s2_ref, a1_ref, a2_ref, *, tk):
    tq = q_ref.shape[0]
    seq = k_ref.shape[0]
    n_kc = seq // tk

    @pl.when(pl.program_id(2) == 0)
    def _():
        v1_ref[:, :LANES] = v_ref[...]
        v1_ref[:, LANES:] = jnp.ones((seq, LANES), BF16)

    q = q_ref[...]
    lane = lax.broadcasted_iota(jnp.int32, q.shape, 1)
    zero = jnp.zeros_like(q)
    q1 = jnp.where(lane < HEAD_DIM_A, q, zero)
    q2 = jnp.where(lane >= HEAD_DIM_A, q, zero)

    def scores(c, carry):
        m1, m2 = carry
        kc = k_ref[pl.ds(pl.multiple_of(c * tk, tk), tk), :]
        sc1 = _dot_nt(q1, kc)
        sc2 = _dot_nt(q2, kc)
        s1_ref[c] = sc1
        s2_ref[c] = sc2
        for u in range(tk // LANES):
            m1 = jnp.maximum(m1, sc1[:, u * LANES:(u + 1) * LANES])
            m2 = jnp.maximum(m2, sc2[:, u * LANES:(u + 1) * LANES])
        return m1, m2

    neg = jnp.full((tq, LANES), -jnp.inf, F32)
    m1, m2 = lax.fori_loop(0, n_kc, scores, (neg, neg))
    m1 = jnp.max(m1, axis=1, keepdims=True)
    m2 = jnp.max(m2, axis=1, keepdims=True)

    a1_ref[...] = jnp.zeros_like(a1_ref)
    a2_ref[...] = jnp.zeros_like(a2_ref)

    def weighted(c, carry):
        vc = v1_ref[pl.ds(pl.multiple_of(c * tk, tk), tk), :]
        p1 = jnp.exp(s1_ref[c] - m1).astype(BF16)
        p2 = jnp.exp(s2_ref[c] - m2).astype(BF16)
        a1_ref[...] += _dot(p1, vc)
        a2_ref[...] += _dot(p2, vc)
        return carry

    lax.fori_loop(0, n_kc, weighted, 0)

    lp = lam_ref[...]
    lam = (jnp.exp(jnp.sum(lp[0:1] * lp[1:2], axis=1, keepdims=True))
           - jnp.exp(jnp.sum(lp[2:3] * lp[3:4], axis=1, keepdims=True)) + LAM_INIT)
    a1 = a1_ref[...]
    a2 = a2_ref[...]
    o = a1[:, :LANES] / a1[:, LANES:] - lam * (a2[:, :LANES] / a2[:, LANES:])
    ms = jnp.mean(o * o, axis=1, keepdims=True)
    o = o * lax.rsqrt(ms + RMS_EPS) * sw_ref[...] * (1.0 - LAM_INIT)
    o_ref[...] = o.astype(o_ref.dtype)


def _attention(proj3, lam_p, subln_w, *, tq=512, tk=512):
    batch, seq, _ = proj3.shape
    hw = 2 * HEAD_DIM_A
    kern = functools.partial(_attn_kernel, tk=tk)
    return pl.pallas_call(
        kern,
        grid=(batch, N_HEADS_A, seq // tq),
        in_specs=[
            pl.BlockSpec((None, tq, hw), lambda b, h, i: (b, i, Q_OFF // hw + h)),
            pl.BlockSpec((None, seq, hw), lambda b, h, i: (b, 0, K_OFF // hw + h)),
            pl.BlockSpec((None, seq, hw), lambda b, h, i: (b, 0, V_OFF // hw + h)),
            pl.BlockSpec((4, HEAD_DIM_A), lambda b, h, i: (0, 0)),
            pl.BlockSpec((1, hw), lambda b, h, i: (0, 0)),
        ],
        out_specs=pl.BlockSpec((None, tq, hw), lambda b, h, i: (b, i, h)),
        out_shape=jax.ShapeDtypeStruct((batch, seq, ATTN_W), BF16),
        scratch_shapes=[
            pltpu.VMEM((seq, 2 * LANES), BF16),
            pltpu.VMEM((seq // tk, tq, tk), F32),
            pltpu.VMEM((seq // tk, tq, tk), F32),
            pltpu.VMEM((tq, 2 * LANES), F32),
            pltpu.VMEM((tq, 2 * LANES), F32),
        ],
        compiler_params=pltpu.CompilerParams(
            dimension_semantics=("parallel", "parallel", "arbitrary"), vmem_limit_bytes=VMEM_LIMIT),
        name="diff_attn",
    )(proj3, proj3, proj3, lam_p, subln_w)


def _silu(v):
    return v * jax.nn.sigmoid(v)


def _ssd_kernel(z_ref, xs_ref, b_ref, c_ref, dt_ref, cwx_ref, cwb_ref, cwc_ref, cbx_ref, cbb_ref, cbc_ref,
                dtb_ref, alog_ref, dsk_ref, nw_ref, o_ref,
                pad_ref, xc_ref, bc_ref, cc_ref, y_ref, cb_ref, dts_ref, cs_ref, st_ref):
    seq = xs_ref.shape[0]
    n_chunks = seq // CHUNK
    halo = SUBLANES
    half = CONV_SSD // 2
    gw = GROUP_W
    w_all = gw + 2 * D_STATE

    pad_ref[0:halo, :] = jnp.zeros((halo, w_all), F32)
    pad_ref[halo + seq:, :] = jnp.zeros((halo, w_all), F32)
    pad_ref[halo:halo + seq, 0:gw] = xs_ref[...].astype(F32)
    pad_ref[halo:halo + seq, gw:gw + D_STATE] = b_ref[...].astype(F32)
    pad_ref[halo:halo + seq, gw + D_STATE:] = c_ref[...].astype(F32)
    cw = jnp.concatenate([cwx_ref[...], cwb_ref[...], cwc_ref[...]], axis=1)
    cbias = jnp.concatenate([cbx_ref[...], cbb_ref[...], cbc_ref[...]], axis=1)

    def conv_chunk(c, carry):
        t0 = pl.multiple_of(c * CHUNK, CHUNK)
        win = pad_ref[pl.ds(t0, CHUNK + 2 * halo), :]
        acc = jnp.zeros((CHUNK, w_all), F32) + cbias
        for k in range(CONV_SSD):
            off = halo - half + k
            acc = acc + win[off:off + CHUNK, :] * cw[k:k + 1, :]
        acc = _silu(acc)
        xc_ref[pl.ds(t0, CHUNK), :] = acc[:, 0:gw]
        bc_ref[pl.ds(t0, CHUNK), :] = acc[:, gw:gw + D_STATE]
        cc_ref[pl.ds(t0, CHUNK), :] = acc[:, gw + D_STATE:]
        return carry

    lax.fori_loop(0, n_chunks, conv_chunk, 0)

    nh = 2 * HEADS_PER_GROUP
    a_col = -jnp.exp(alog_ref[...])
    r_i = lax.broadcasted_iota(jnp.int32, (CHUNK, CHUNK), 0)
    c_i = lax.broadcasted_iota(jnp.int32, (CHUNK, CHUNK), 1)
    upper = (r_i <= c_i).astype(F32)
    lower = (r_i >= c_i).astype(F32)
    row8 = lax.broadcasted_iota(jnp.int32, (nh, CHUNK), 0)
    for c in range(n_chunks):
        raw = dt_ref[:, c * CHUNK:(c + 1) * CHUNK] + dtb_ref[...]
        dt = jnp.maximum(raw, 0.0) + jnp.log1p(jnp.exp(-jnp.abs(raw)))
        da = dt * a_col
        cs = jnp.where(row8 < HEADS_PER_GROUP, _dot_exact(da, upper), _dot_exact(da, lower))
        dts_ref[c] = dt
        cs_ref[c] = cs

    lane_head = lax.broadcasted_iota(jnp.int32, (1, gw), 1) // HEAD_DIM_S
    exp_rows = lax.broadcasted_iota(jnp.int32, (nh, gw), 0)
    exp_lane_head = lax.broadcasted_iota(jnp.int32, (nh, gw), 1) // HEAD_DIM_S

    def chunk_step(c, reverse):
        r0 = HEADS_PER_GROUP if reverse else 0
        t0 = pl.multiple_of(c * CHUNK, CHUNK)
        x = xc_ref[pl.ds(t0, CHUNK), :]
        bm = bc_ref[pl.ds(t0, CHUNK), :]
        cm = cc_ref[pl.ds(t0, CHUNK), :]
        dt = dts_ref[c]
        cs = cs_ref[c]
        cs_t = jnp.transpose(cs)
        if reverse:
            cbm = cb_ref[c]
            tot = cs[:, 0:1]
            keep = r_i <= c_i
        else:
            cbm = _dot_nt(cm.astype(BF16), bm.astype(BF16))
            cb_ref[c] = cbm
            tot = cs[:, CHUNK - 1:CHUNK]
            keep = r_i >= c_i
        wrow = dt * jnp.exp(tot - cs)
        bt = jnp.transpose(bm)
        sel = (exp_rows == exp_lane_head + r0).astype(F32)
        g_out = jnp.exp(_dot_exact(cs_t, sel))
        cdec = jnp.exp(jnp.sum(tot * sel, axis=0, keepdims=True))
        xb = x.astype(BF16)
        zero = jnp.zeros_like(xb)
        y = jnp.zeros((CHUNK, gw), F32)
        new = jnp.zeros((D_STATE, gw), F32)
        for e in range(HEADS_PER_GROUP):
            r = r0 + e
            xm = jnp.where(lane_head == e, xb, zero)
            seg = cs_t[:, r:r + 1] - cs[r:r + 1, :]
            decay = jnp.exp(jnp.where(keep, seg, -jnp.inf))
            m = (cbm * decay * dt[r:r + 1, :]).astype(BF16)
            y = y + _dot(m, xm)
            new = new + _dot((bt * wrow[r:r + 1, :]).astype(BF16), xm)
        st = st_ref[...]
        y = y + _dot(cm.astype(BF16), st.astype(BF16)) * g_out
        st_ref[...] = st * cdec + new
        return t0, x, y

    st_ref[...] = jnp.zeros_like(st_ref)

    def fwd(c, carry):
        t0, x, y = chunk_step(c, False)
        y_ref[pl.ds(t0, CHUNK), :] = y + x * dsk_ref[...]
        return carry

    lax.fori_loop(0, n_chunks, fwd, 0)

    st_ref[...] = jnp.zeros_like(st_ref)

    def bwd(i, carry):
        c = n_chunks - 1 - i
        t0, x, y = chunk_step(c, True)
        y = y + y_ref[pl.ds(t0, CHUNK), :]
        y = y * _silu(z_ref[pl.ds(t0, CHUNK), :].astype(F32))
        ms = jnp.mean(y * y, axis=1, keepdims=True)
        y = y * lax.rsqrt(ms + RMS_EPS) * nw_ref[...]
        o_ref[pl.ds(t0, CHUNK), :] = y.astype(o_ref.dtype)
        return carry

    lax.fori_loop(0, n_chunks, bwd, 0)


def _ssd(proj3, dt_t, cw_x, cw_b, cw_c, cb_x, cb_b, cb_c, dtb, alog, dsk, nw):
    batch, seq, _ = proj3.shape
    gw, ns = GROUP_W, D_STATE
    nh = 2 * HEADS_PER_GROUP
    n_chunks = seq // CHUNK
    gspec = lambda w: pl.BlockSpec((1, w), lambda b, g: (0, g))
    return pl.pallas_call(
        _ssd_kernel,
        grid=(batch, N_GROUPS),
        in_specs=[
            pl.BlockSpec((None, seq, gw), lambda b, g: (b, 0, Z_OFF // gw + g)),
            pl.BlockSpec((None, seq, gw), lambda b, g: (b, 0, XS_OFF // gw + g)),
            pl.BlockSpec((None, seq, ns), lambda b, g: (b, 0, B_OFF // ns + g)),
            pl.BlockSpec((None, seq, ns), lambda b, g: (b, 0, C_OFF // ns + g)),
            pl.BlockSpec((None, nh, seq), lambda b, g: (b, g, 0)),
            pl.BlockSpec((CONV_SSD, gw), lambda b, g: (0, g)),
            pl.BlockSpec((CONV_SSD, ns), lambda b, g: (0, g)),
            pl.BlockSpec((CONV_SSD, ns), lambda b, g: (0, g)),
            gspec(gw), gspec(ns), gspec(ns),
            pl.BlockSpec((nh, 1), lambda b, g: (g, 0)),
            pl.BlockSpec((nh, 1), lambda b, g: (g, 0)),
            gspec(gw), gspec(gw),
        ],
        out_specs=pl.BlockSpec((None, seq, gw), lambda b, g: (b, 0, g)),
        out_shape=jax.ShapeDtypeStruct((batch, seq, D_INNER), BF16),
        scratch_shapes=[
            pltpu.VMEM((seq + 2 * SUBLANES, gw + 2 * ns), F32),
            pltpu.VMEM((seq, gw), F32),
            pltpu.VMEM((seq, ns), F32),
            pltpu.VMEM((seq, ns), F32),
            pltpu.VMEM((seq, gw), F32),
            pltpu.VMEM((n_chunks, CHUNK, CHUNK), F32),
            pltpu.VMEM((n_chunks, nh, CHUNK), F32),
            pltpu.VMEM((n_chunks, nh, CHUNK), F32),
            pltpu.VMEM((ns, gw), F32),
        ],
        compiler_params=pltpu.CompilerParams(
            dimension_semantics=("parallel", "parallel"), vmem_limit_bytes=VMEM_LIMIT),
        name="ssd",
    )(proj3, proj3, proj3, proj3, dt_t, cw_x, cw_b, cw_c, cb_x, cb_b, cb_c, dtb, alog, dsk, nw)


def _layer_norm(v, g, b):
    mu = jnp.mean(v, axis=1, keepdims=True)
    d = v - mu
    var = jnp.mean(d * d, axis=1, keepdims=True)
    return d * lax.rsqrt(var + LN_EPS) * g + b


def _merge_kernel(x_ref, at_ref, sd_ref, ga_ref, gs_ref, wpa_ref, wps_ref, wo_ref, bg_ref, g_ref, b_ref, o_ref):
    bg = bg_ref[...]
    g_a = jax.nn.sigmoid(ga_ref[...].astype(F32) + bg[:, :D_MODEL])
    g_s = jax.nn.sigmoid(gs_ref[...].astype(F32) + bg[:, D_MODEL:])
    merged = g_a * _dot(at_ref[...], wpa_ref[...]) + g_s * _dot(sd_ref[...], wps_ref[...])
    y = ALPHA * x_ref[...] + _dot(merged.astype(BF16), wo_ref[...])
    o_ref[...] = _layer_norm(y, g_ref[...], b_ref[...])


def _merge(x2, attn2, ssd2, proj2, wpa, wps, wo, bg, ln_g, ln_b, *, tm=512):
    t = x2.shape[0]
    full = lambda a: pl.BlockSpec(a.shape, lambda i: (0, 0))
    return pl.pallas_call(
        _merge_kernel,
        grid=(t // tm,),
        in_specs=[
            pl.BlockSpec((tm, D_MODEL), lambda i: (i, 0)),
            pl.BlockSpec((tm, ATTN_W), lambda i: (i, 0)),
            pl.BlockSpec((tm, D_INNER), lambda i: (i, 0)),
            pl.BlockSpec((tm, D_MODEL), lambda i: (i, GATE_OFF // D_MODEL)),
            pl.BlockSpec((tm, D_MODEL), lambda i: (i, GATE_OFF // D_MODEL + 1)),
            full(wpa), full(wps), full(wo), full(bg), full(ln_g), full(ln_b),
        ],
        out_specs=pl.BlockSpec((tm, D_MODEL), lambda i: (i, 0)),
        out_shape=jax.ShapeDtypeStruct((t, D_MODEL), F32),
        compiler_params=pltpu.CompilerParams(
            dimension_semantics=("parallel",), vmem_limit_bytes=VMEM_LIMIT),
        name="merge_ln1",
    )(x2, attn2, ssd2, proj2, proj2, wpa, wps, wo, bg, ln_g, ln_b)


def _ffn_kernel(x_ref, prev_ref, next_ref, wup_ref, cw_ref, cb_ref, wdn_ref, g_ref, b_ref, o_ref, xh_ref,
                *, tiles_per_seq, tf):
    i = pl.program_id(0)
    tm = x_ref.shape[0]
    halo = SUBLANES
    x = x_ref[...]
    first = (i % tiles_per_seq) == 0
    last = (i % tiles_per_seq) == tiles_per_seq - 1
    zero = jnp.zeros((halo, D_MODEL), F32)
    xh_ref[0:halo, :] = jnp.where(first, zero, prev_ref[...]).astype(BF16)
    xh_ref[halo:halo + tm, :] = x.astype(BF16)
    xh_ref[halo + tm:, :] = jnp.where(last, zero, next_ref[...]).astype(BF16)
    xh = xh_ref[...]

    def conv(u, col):
        w = cw_ref[:, col:col + tf]
        out = cb_ref[:, col:col + tf] + u[halo - 1:halo - 1 + tm] * w[0:1]
        out = out + u[halo:halo + tm] * w[1:2]
        return out + u[halo + 1:halo + 1 + tm] * w[2:3]

    acc = jnp.zeros((tm, D_MODEL), F32)
    for j in range(D_FF // tf):
        hg = conv(_dot(xh, wup_ref[:, j * tf:(j + 1) * tf]), j * tf)
        hu = conv(_dot(xh, wup_ref[:, D_FF + j * tf:D_FF + (j + 1) * tf]), D_FF + j * tf)
        act = (_silu(hg) * hu).astype(BF16)
        acc = acc + _dot(act, wdn_ref[j * tf:(j + 1) * tf, :])
    o_ref[...] = _layer_norm(ALPHA * x + acc, g_ref[...], b_ref[...])


def _ffn(x1, wup, cw, cb, wdn, ln_g, ln_b, *, seq, tm=512, tf=256):
    t = x1.shape[0]
    hb = tm // SUBLANES
    n_hb = t // SUBLANES
    full = lambda a: pl.BlockSpec(a.shape, lambda i: (0, 0))
    kern = functools.partial(_ffn_kernel, tiles_per_seq=seq // tm, tf=tf)
    return pl.pallas_call(
        kern,
        grid=(t // tm,),
        in_specs=[
            pl.BlockSpec((tm, D_MODEL), lambda i: (i, 0)),
            pl.BlockSpec((SUBLANES, D_MODEL), lambda i: (jnp.maximum(i * hb - 1, 0), 0)),
            pl.BlockSpec((SUBLANES, D_MODEL), lambda i: (jnp.minimum((i + 1) * hb, n_hb - 1), 0)),
            full(wup), full(cw), full(cb), full(wdn), full(ln_g), full(ln_b),
        ],
        out_specs=pl.BlockSpec((tm, D_MODEL), lambda i: (i, 0)),
        out_shape=jax.ShapeDtypeStruct((t, D_MODEL), F32),
        scratch_shapes=[pltpu.VMEM((tm + 2 * SUBLANES, D_MODEL), BF16)],
        compiler_params=pltpu.CompilerParams(
            dimension_semantics=("parallel",), vmem_limit_bytes=VMEM_LIMIT),
        name="ffn_ln2",
    )(x1, x1, x1, wup, cw, cb, wdn, ln_g, ln_b)


def _rotary_tables(seq):
    half = ROT_DIM // 2
    pos = jnp.arange(seq, dtype=F32)
    inv_freq = jnp.power(ROPE_THETA, -jnp.arange(0, ROT_DIM, 2, dtype=F32) / ROT_DIM)
    ang = pos[:, None] * inv_freq[None, :]
    cos, sin = jnp.cos(ang), jnp.sin(ang)
    pad = jnp.zeros((seq, HEAD_DIM_A - ROT_DIM), F32)
    zer = jnp.zeros((seq, half), F32)
    c64 = jnp.concatenate([cos, cos, pad + 1.0], axis=1)
    sa64 = jnp.concatenate([-sin, zer, pad], axis=1)
    sb64 = jnp.concatenate([zer, sin, pad], axis=1)
    tile2 = lambda a: jnp.concatenate([a, a], axis=1)
    return tile2(c64), tile2(sa64), tile2(sb64)


def _group_rows(p):
    return p.reshape(2, N_GROUPS, HEADS_PER_GROUP).transpose(1, 0, 2).reshape(N_GROUPS * 2 * HEADS_PER_GROUP, 1)


def kernel(x, w_in, b_gate, lambda_q1, lambda_k1, lambda_q2, lambda_k2, attn_subln_w, conv_ssd_w, conv_ssd_b,
           dt_bias, a_log, d_skip, ssd_norm_w, w_proj_attn, w_proj_ssd, w_out, ln1_g, ln1_b, w_up, conv_ffn_w,
           conv_ffn_b, w_down, ln2_g, ln2_b):
    batch, seq, _ = x.shape
    t = batch * seq
    l = 0
    xbc_w = D_INNER + 2 * N_GROUPS * D_STATE
    dt_off = Z_OFF + D_INNER + xbc_w
    w = w_in[l]
    w_main = jnp.concatenate([w[:, :ATTN_W] * (HEAD_DIM_A ** -0.5), w[:, ATTN_W:dt_off],
                              w[:, dt_off + 2 * N_HEADS_S:]], axis=1).astype(BF16)
    w_dt = w[:, dt_off:dt_off + 2 * N_HEADS_S]
    w_dt_t = (w_dt.reshape(D_MODEL, 2, N_GROUPS, HEADS_PER_GROUP).transpose(2, 1, 3, 0)
              .reshape(2 * N_HEADS_S, D_MODEL).astype(BF16))
    cos_t, sa_t, sb_t = _rotary_tables(seq)

    x2 = x.reshape(t, D_MODEL)
    proj2, dt_t = _in_proj(x2, w_main, w_dt_t, cos_t, sa_t, sb_t, batch=batch, seq=seq)
    proj3 = proj2.reshape(batch, seq, PROJ_W)

    lam_p = jnp.stack([lambda_q1[l], lambda_k1[l], lambda_q2[l], lambda_k2[l]]).astype(F32)
    attn = _attention(proj3, lam_p, attn_subln_w[l].reshape(1, 2 * HEAD_DIM_A))

    cw = conv_ssd_w[l]
    cb = conv_ssd_b[l].reshape(1, xbc_w)
    nb = N_GROUPS * D_STATE
    ssd = _ssd(proj3, dt_t,
               cw[:, :D_INNER], cw[:, D_INNER:D_INNER + nb], cw[:, D_INNER + nb:],
               cb[:, :D_INNER], cb[:, D_INNER:D_INNER + nb], cb[:, D_INNER + nb:],
               _group_rows(dt_bias[l]), _group_rows(a_log[l]),
               jnp.repeat(d_skip[l], HEAD_DIM_S).reshape(1, D_INNER), ssd_norm_w[l].reshape(1, D_INNER))

    x1 = _merge(x2, attn.reshape(t, ATTN_W), ssd.reshape(t, D_INNER), proj2,
                w_proj_attn[l].astype(BF16), w_proj_ssd[l].astype(BF16), w_out[l].astype(BF16),
                b_gate[l].reshape(1, 2 * D_MODEL), ln1_g[l].reshape(1, D_MODEL), ln1_b[l].reshape(1, D_MODEL))

    out = _ffn(x1, w_up[l].astype(BF16), conv_ffn_w[l], conv_ffn_b[l].reshape(1, 2 * D_FF),
               w_down[l].astype(BF16), ln2_g[l].reshape(1, D_MODEL), ln2_b[l].reshape(1, D_MODEL), seq=seq)
    return out.reshape(batch, seq, D_MODEL)
```

```python
import functools
import math

import jax
import jax.numpy as jnp
from jax import lax
from jax.experimental import pallas as pl
from jax.experimental.pallas import tpu as pltpu

F32 = jnp.float32
BF16 = jnp.bfloat16

D_MODEL = 1024
N_HEADS_A = 8
HEAD_DIM_A = 64
ROT_DIM = HEAD_DIM_A // 4
ROPE_THETA = 500000.0
ATTN_W = N_HEADS_A * 2 * HEAD_DIM_A
D_INNER = 2 * D_MODEL
HEAD_DIM_S = 64
N_HEADS_S = D_INNER // HEAD_DIM_S
N_GROUPS = 8
HEADS_PER_GROUP = N_HEADS_S // N_GROUPS
GROUP_W = HEADS_PER_GROUP * HEAD_DIM_S
D_STATE = 128
CONV_SSD = 5
CHUNK = 128
D_FF = 2816
DEPTH = 1
ALPHA = (2 * DEPTH) ** 0.25
LN_EPS = 1e-5
RMS_EPS = 1e-5
LAM_INIT = 0.8 - 0.6 * math.exp(-0.3 * 0)
LOG2_E = math.log2(math.e)

LANES = 128
SUBLANES = 8
VMEM_LIMIT = 56 * 1024 * 1024

Q_OFF, K_OFF, V_OFF = 0, ATTN_W, 2 * ATTN_W
Z_OFF = 3 * ATTN_W
XS_OFF = Z_OFF + D_INNER
B_OFF = XS_OFF + D_INNER
C_OFF = B_OFF + N_GROUPS * D_STATE
GATE_OFF = C_OFF + N_GROUPS * D_STATE
PROJ_W = GATE_OFF + 2 * D_MODEL


def _dot(a, b):
    return jnp.dot(a, b, preferred_element_type=F32)


def _dot_nt(a, b):
    return lax.dot_general(a, b, (((1,), (1,)), ((), ())), preferred_element_type=F32)


def _dot_exact(a, b):
    return jnp.dot(a, b, preferred_element_type=F32, precision=lax.Precision.HIGHEST)


def _in_proj_kernel(x_ref, w_ref, wdt_ref, cos_ref, sa_ref, sb_ref, o_ref, dt_ref, xb_ref, *, n_rot_tiles):
    j = pl.program_id(1)

    @pl.when(j == 0)
    def _():
        xb = x_ref[...].astype(BF16)
        xb_ref[...] = xb
        dt_t = _dot_nt(wdt_ref[...], xb)
        for c in range(dt_ref.shape[0]):
            dt_ref[c] = dt_t[:, c * CHUNK:(c + 1) * CHUNK]

    acc = _dot(xb_ref[...], w_ref[...])

    @pl.when(j < n_rot_tiles)
    def _():
        c, sa, sb = cos_ref[...], sa_ref[...], sb_ref[...]
        for h in range(acc.shape[1] // LANES):
            a = acc[:, h * LANES:(h + 1) * LANES]
            up = pltpu.roll(a, LANES - ROT_DIM // 2, axis=1)
            dn = pltpu.roll(a, ROT_DIM // 2, axis=1)
            o_ref[:, h * LANES:(h + 1) * LANES] = (a * c + up * sa + dn * sb).astype(o_ref.dtype)

    @pl.when(j >= n_rot_tiles)
    def _():
        o_ref[...] = acc.astype(o_ref.dtype)


def _in_proj(x2, w_main, w_dt_t, cos_t, sa_t, sb_t, *, batch, seq, tm=1024, tn=1024):
    t = x2.shape[0]
    n_seq_tiles = seq // tm
    n_dt = w_dt_t.shape[0]
    kern = functools.partial(_in_proj_kernel, n_rot_tiles=(2 * ATTN_W) // tn)
    return pl.pallas_call(
        kern,
        grid=(t // tm, PROJ_W // tn),
        in_specs=[
            pl.BlockSpec((tm, D_MODEL), lambda i, j: (i, 0)),
            pl.BlockSpec((D_MODEL, tn), lambda i, j: (0, j)),
            pl.BlockSpec((n_dt, D_MODEL), lambda i, j: (0, 0)),
            pl.BlockSpec((tm, LANES), lambda i, j: (i % n_seq_tiles, 0)),
            pl.BlockSpec((tm, LANES), lambda i, j: (i % n_seq_tiles, 0)),
            pl.BlockSpec((tm, LANES), lambda i, j: (i % n_seq_tiles, 0)),
        ],
        out_specs=[
            pl.BlockSpec((tm, tn), lambda i, j: (i, j)),
            pl.BlockSpec((None, tm // CHUNK, n_dt, CHUNK),
                         lambda i, j: (i // n_seq_tiles, i % n_seq_tiles, 0, 0)),
        ],
        out_shape=[
            jax.ShapeDtypeStruct((t, PROJ_W), BF16),
            jax.ShapeDtypeStruct((batch, seq // CHUNK, n_dt, CHUNK), F32),
        ],
        scratch_shapes=[pltpu.VMEM((tm, D_MODEL), BF16)],
        compiler_params=pltpu.CompilerParams(
            dimension_semantics=("parallel", "arbitrary"), vmem_limit_bytes=VMEM_LIMIT),
        name="in_proj",
    )(x2, w_main, w_dt_t, cos_t, sa_t, sb_t)


def _attn_kernel(q_ref, k_ref, v_ref, lam_ref, sw_ref, o_ref, v1_ref, s1_ref, s2_ref, *, tk):
    tq = q_ref.shape[0]
    seq = k_ref.shape[0]
    n_kc = seq // tk

    @pl.when(pl.program_id(2) == 0)
    def _():
        v1_ref[:, :LANES] = v_ref[...]
        v1_ref[:, LANES:] = jnp.ones((seq, LANES), BF16)

    q = q_ref[...]
    lane = lax.broadcasted_iota(jnp.int32, q.shape, 1)
    zero = jnp.zeros_like(q)
    q1 = jnp.where(lane < HEAD_DIM_A, q, zero)
    q2 = jnp.where(lane >= HEAD_DIM_A, q, zero)

    m1 = jnp.full((tq, LANES), -jnp.inf, F32)
    m2 = m1
    for c in range(n_kc):
        kc = k_ref[c * tk:(c + 1) * tk, :]
        sc1 = _dot_nt(q1, kc)
        sc2 = _dot_nt(q2, kc)
        s1_ref[c] = sc1
        s2_ref[c] = sc2
        for u in range(tk // LANES):
            m1 = jnp.maximum(m1, sc1[:, u * LANES:(u + 1) * LANES])
            m2 = jnp.maximum(m2, sc2[:, u * LANES:(u + 1) * LANES])
    m1 = jnp.max(m1, axis=1, keepdims=True)
    m2 = jnp.max(m2, axis=1, keepdims=True)

    a1 = jnp.zeros((tq, 2 * LANES), F32)
    a2 = a1
    for c in range(n_kc):
        vc = v1_ref[c * tk:(c + 1) * tk, :]
        p1 = jnp.exp2(s1_ref[c] - m1).astype(BF16)
        p2 = jnp.exp2(s2_ref[c] - m2).astype(BF16)
        a1 = a1 + _dot(p1, vc)
        a2 = a2 + _dot(p2, vc)

    lp = lam_ref[...]
    lam = (jnp.exp(jnp.sum(lp[0:1] * lp[1:2], axis=1, keepdims=True))
           - jnp.exp(jnp.sum(lp[2:3] * lp[3:4], axis=1, keepdims=True)) + LAM_INIT)
    o =a1[:, :LANES] / a1[:, LANES:] - lam * (a2[:, :LANES] / a2[:, LANES:])
    ms = jnp.mean(o * o, axis=1, keepdims=True)
    o = o * lax.rsqrt(ms + RMS_EPS) * sw_ref[...] * (1.0 - LAM_INIT)
    o_ref[...] = o.astype(o_ref.dtype)


def _attention(proj3, lam_p, subln_w, *, tq=512, tk=512):
    batch, seq, _ = proj3.shape
    hw = 2 * HEAD_DIM_A
    kern = functools.partial(_attn_kernel, tk=tk)
    return pl.pallas_call(
        kern,
        grid=(batch, N_HEADS_A, seq // tq),
        in_specs=[
            pl.BlockSpec((None, tq, hw), lambda b, h, i: (b, i, Q_OFF // hw + h)),
            pl.BlockSpec((None, seq, hw), lambda b, h, i: (b, 0, K_OFF // hw + h)),
            pl.BlockSpec((None, seq, hw), lambda b, h, i: (b, 0, V_OFF // hw + h)),
            pl.BlockSpec((4, HEAD_DIM_A), lambda b, h, i: (0, 0)),
            pl.BlockSpec((1, hw), lambda b, h, i: (0, 0)),
        ],
        out_specs=pl.BlockSpec((None, tq, hw), lambda b, h, i: (b, i, h)),
        out_shape=jax.ShapeDtypeStruct((batch, seq, ATTN_W), BF16),
        scratch_shapes=[
            pltpu.VMEM((seq, 2 * LANES), BF16),
            pltpu.VMEM((seq // tk, tq, tk), F32),
            pltpu.VMEM((seq // tk, tq, tk), F32),
        ],
        compiler_params=pltpu.CompilerParams(
            dimension_semantics=("parallel", "parallel", "arbitrary"), vmem_limit_bytes=VMEM_LIMIT),
        name="diff_attn",
    )(proj3, proj3, proj3, lam_p, subln_w)


def _silu(v):
    return v * jax.nn.sigmoid(v)


def _ssd_kernel(z_ref, xs_ref, b_ref, c_ref, dt_ref, cwx_ref, cwb_ref, cwc_ref, cbx_ref, cbb_ref, cbc_ref,
                dtb_ref, alog_ref, dsk_ref, nw_ref, o_ref,
                pad_ref, xc_ref, bc_ref, cc_ref, y_ref, cb_ref, src_ref, cs_ref, st_ref):
    seq = xs_ref.shape[0]
    n_chunks = seq // CHUNK
    halo = SUBLANES
    half = CONV_SSD // 2
    gw = GROUP_W
    w_all = gw + 2 * D_STATE

    pad_ref[0:halo, :] = jnp.zeros((halo, w_all), F32)
    pad_ref[halo + seq:, :] = jnp.zeros((halo, w_all), F32)
    pad_ref[halo:halo + seq, 0:gw] = xs_ref[...].astype(F32)
    pad_ref[halo:halo + seq, gw:gw + D_STATE] = b_ref[...].astype(F32)
    pad_ref[halo:halo + seq, gw + D_STATE:] = c_ref[...].astype(F32)
    cw = jnp.concatenate([cwx_ref[...], cwb_ref[...], cwc_ref[...]], axis=1)
    cbias = jnp.concatenate([cbx_ref[...], cbb_ref[...], cbc_ref[...]], axis=1)

    def conv_chunk(c, carry):
        t0 = pl.multiple_of(c * CHUNK, CHUNK)
        win = pad_ref[pl.ds(t0, CHUNK + 2 * halo), :]
        acc = jnp.zeros((CHUNK, w_all), F32) + cbias
        for k in range(CONV_SSD):
            off = halo - half + k
            acc = acc + win[off:off + CHUNK, :] * cw[k:k + 1, :]
        acc = _silu(acc)
        xc_ref[pl.ds(t0, CHUNK), :] = acc[:, 0:gw]
        bc_ref[pl.ds(t0, CHUNK), :] = acc[:, gw:gw + D_STATE]
        cc_ref[pl.ds(t0, CHUNK), :] = acc[:, gw + D_STATE:]
        cb_ref[c] = _dot_nt(acc[:, gw + D_STATE:].astype(BF16), acc[:, gw:gw + D_STATE].astype(BF16))
        return carry

    lax.fori_loop(0, n_chunks, conv_chunk, 0, unroll=2)

    nh = 2 * HEADS_PER_GROUP
    r_i = lax.broadcasted_iota(jnp.int32, (CHUNK, CHUNK), 0)
    c_i = lax.broadcasted_iota(jnp.int32, (CHUNK, CHUNK), 1)
    upper = (r_i <= c_i).astype(F32)
    lower = (r_i >= c_i).astype(F32)
    raw = dt_ref[...].reshape(n_chunks * nh, CHUNK) + dtb_ref[...]
    dt_all = jnp.maximum(raw, 0.0) + jnp.log1p(jnp.exp(-jnp.abs(raw)))
    da = dt_all * (-jnp.exp(alog_ref[...]))
    row_all = lax.broadcasted_iota(jnp.int32, (n_chunks * nh, CHUNK), 0)
    cs_all = jnp.where(row_all % nh < HEADS_PER_GROUP, _dot_exact(da, upper), _dot_exact(da, lower))
    cs2_all = cs_all * LOG2_E
    cs_ref[...] = cs2_all.reshape(n_chunks, nh, CHUNK)
    src_ref[...] = (cs2_all - jnp.log2(dt_all)).reshape(n_chunks, nh, CHUNK)

    lane_head = lax.broadcasted_iota(jnp.int32, (1, gw), 1) // HEAD_DIM_S

    def chunk_step(c, reverse):
        r0 = HEADS_PER_GROUP if reverse else 0
        t0 = pl.multiple_of(c * CHUNK, CHUNK)
        x = xc_ref[pl.ds(t0, CHUNK), :]
        bm = bc_ref[pl.ds(t0, CHUNK), :]
        cm = cc_ref[pl.ds(t0, CHUNK), :].astype(BF16)
        cs2 = cs_ref[c]
        src = src_ref[c]
        cbm = cb_ref[c]
        if reverse:
            tot = cs2[:, 0:1]
            keep = r_i <= c_i
        else:
            tot = cs2[:, CHUNK - 1:CHUNK]
            keep = r_i >= c_i
        cs2_t = jnp.transpose(cs2)
        g_col = jnp.exp2(cs2_t)
        g_out = g_col[:, r0 + HEADS_PER_GROUP - 1:r0 + HEADS_PER_GROUP]
        for e in range(HEADS_PER_GROUP - 2, -1, -1):
            g_out = jnp.where(lane_head == e, g_col[:, r0 + e:r0 + e + 1], g_out)
        sel_row = (lax.broadcasted_iota(jnp.int32, (nh, gw), 0) == lane_head + r0).astype(F32)
        cdec = jnp.exp2(jnp.sum(tot * sel_row, axis=0, keepdims=True))
        wrow = jnp.exp2(tot - src)
        bt = jnp.transpose(bm)
        xb = x.astype(BF16)
        zero = jnp.zeros_like(xb)
        ms, bs, xs = [], [], []
        for e in range(HEADS_PER_GROUP):
            r = r0 + e
            seg = cs2_t[:, r:r + 1] - src[r:r + 1, :]
            decay = jnp.exp2(jnp.where(keep, seg, -jnp.inf))
            ms.append((cbm * decay).astype(BF16))
            bs.append((bt * wrow[r:r + 1, :]).astype(BF16))
            xs.append(jnp.where(lane_head == e, xb, zero))
        xcat = jnp.concatenate(xs, axis=0)
        d = 1 if reverse else 0
        st = st_ref[d]
        y = _dot(jnp.concatenate(ms, axis=1), xcat) + _dot(cm, st.astype(BF16)) * g_out
        st_ref[d] = st * cdec + _dot(jnp.concatenate(bs, axis=1), xcat)
        y_ref[d, pl.ds(t0, CHUNK), :] = y

    st_ref[...] = jnp.zeros_like(st_ref)

    def scan(i, carry):
        chunk_step(i, False)
        chunk_step(n_chunks - 1 - i, True)
        return carry

    lax.fori_loop(0, n_chunks, scan, 0, unroll=2)

    def finish(c, carry):
        t0 = pl.multiple_of(c * CHUNK, CHUNK)
        rows = pl.ds(t0, CHUNK)
        y = y_ref[0, rows, :] + y_ref[1, rows, :] + xc_ref[rows, :] * dsk_ref[...]
        y = y * _silu(z_ref[rows, :].astype(F32))
        ms = jnp.mean(y * y, axis=1, keepdims=True)
        y = y * lax.rsqrt(ms + RMS_EPS) * nw_ref[...]
        o_ref[rows, :] = y.astype(o_ref.dtype)
        return carry

    lax.fori_loop(0, n_chunks, finish, 0)


def _ssd(proj3, dt_t, cw_x, cw_b, cw_c, cb_x, cb_b, cb_c, dtb, alog, dsk, nw):
    batch, seq, _ = proj3.shape
    gw, ns = GROUP_W, D_STATE
    nh = 2 * HEADS_PER_GROUP
    n_chunks = seq // CHUNK
    gspec = lambda w: pl.BlockSpec((1, w), lambda b, g: (0, g))
    return pl.pallas_call(
        _ssd_kernel,
        grid=(batch, N_GROUPS),
        in_specs=[
            pl.BlockSpec((None, seq, gw), lambda b, g: (b, 0, Z_OFF // gw + g)),
            pl.BlockSpec((None, seq, gw), lambda b, g: (b, 0, XS_OFF // gw + g)),
            pl.BlockSpec((None, seq, ns), lambda b, g: (b, 0, B_OFF // ns + g)),
            pl.BlockSpec((None, seq, ns), lambda b, g: (b, 0, C_OFF // ns + g)),
            pl.BlockSpec((None, n_chunks, nh, CHUNK), lambda b, g: (b, 0, g, 0)),
            pl.BlockSpec((CONV_SSD, gw), lambda b, g: (0, g)),
            pl.BlockSpec((CONV_SSD, ns), lambda b, g: (0, g)),
            pl.BlockSpec((CONV_SSD, ns), lambda b, g: (0, g)),
            gspec(gw), gspec(ns), gspec(ns),
            pl.BlockSpec((n_chunks * nh, 1), lambda b, g: (g, 0)),
            pl.BlockSpec((n_chunks * nh, 1), lambda b, g: (g, 0)),
            gspec(gw), gspec(gw),
        ],
        out_specs=pl.BlockSpec((None, seq, gw), lambda b, g: (b, 0, g)),
        out_shape=jax.ShapeDtypeStruct((batch, seq, D_INNER), BF16),
        scratch_shapes=[
            pltpu.VMEM((seq + 2 * SUBLANES, gw + 2 * ns), F32),
            pltpu.VMEM((seq, gw), F32),
            pltpu.VMEM((seq, ns), F32),
            pltpu.VMEM((seq, ns), F32),
            pltpu.VMEM((2, seq, gw), F32),
            pltpu.VMEM((n_chunks, CHUNK, CHUNK), F32),
            pltpu.VMEM((n_chunks, nh, CHUNK), F32),
            pltpu.VMEM((n_chunks, nh, CHUNK), F32),
            pltpu.VMEM((2, ns, gw), F32),
        ],
        compiler_params=pltpu.CompilerParams(
            dimension_semantics=("parallel", "parallel"), vmem_limit_bytes=VMEM_LIMIT),
        name="ssd",
    )(proj3, proj3, proj3, proj3, dt_t, cw_x, cw_b, cw_c, cb_x, cb_b, cb_c, dtb, alog, dsk, nw)


def _layer_norm(v, g, b):
    mu = jnp.mean(v, axis=1, keepdims=True)
    d = v - mu
    var = jnp.mean(d * d, axis=1, keepdims=True)
    return d * lax.rsqrt(var + LN_EPS) * g + b


def _merge_kernel(x_ref, at_ref, sd_ref, ga_ref, gs_ref, wpa_ref, wps_ref, wo_ref, bg_ref, g_ref, b_ref, o_ref):
    bg = bg_ref[...]
    g_a = jax.nn.sigmoid(ga_ref[...].astype(F32) + bg[:, :D_MODEL])
    g_s = jax.nn.sigmoid(gs_ref[...].astype(F32) + bg[:, D_MODEL:])
    merged = g_a * _dot(at_ref[...], wpa_ref[...]) + g_s * _dot(sd_ref[...], wps_ref[...])
    y = ALPHA * x_ref[...] + _dot(merged.astype(BF16), wo_ref[...])
    o_ref[...] = _layer_norm(y, g_ref[...], b_ref[...])


def _merge(x2, attn2, ssd2, proj2, wpa, wps, wo, bg, ln_g, ln_b, *, tm=512):
    t = x2.shape[0]
    full = lambda a: pl.BlockSpec(a.shape, lambda i: (0, 0))
    return pl.pallas_call(
        _merge_kernel,
        grid=(t // tm,),
        in_specs=[
            pl.BlockSpec((tm, D_MODEL), lambda i: (i, 0)),
            pl.BlockSpec((tm, ATTN_W), lambda i: (i, 0)),
            pl.BlockSpec((tm, D_INNER), lambda i: (i, 0)),
            pl.BlockSpec((tm, D_MODEL), lambda i: (i, GATE_OFF // D_MODEL)),
            pl.BlockSpec((tm, D_MODEL), lambda i: (i, GATE_OFF // D_MODEL + 1)),
            full(wpa), full(wps), full(wo), full(bg), full(ln_g), full(ln_b),
        ],
        out_specs=pl.BlockSpec((tm, D_MODEL), lambda i: (i, 0)),
        out_shape=jax.ShapeDtypeStruct((t, D_MODEL), F32),
        compiler_params=pltpu.CompilerParams(
            dimension_semantics=("parallel",), vmem_limit_bytes=VMEM_LIMIT),
        name="merge_ln1",
    )(x2, attn2, ssd2, proj2, proj2, wpa, wps, wo, bg, ln_g, ln_b)


def _ffn_kernel(x_ref, prev_ref, next_ref, wup_ref, cw_ref, cb_ref, wdn_ref, g_ref, b_ref, o_ref, xh_ref,
                *, tiles_per_seq, tf):
    i = pl.program_id(0)
    tm = x_ref.shape[0]
    halo = SUBLANES
    x = x_ref[...]
    first = (i % tiles_per_seq) == 0
    last = (i % tiles_per_seq) == tiles_per_seq - 1
    zero = jnp.zeros((halo, D_MODEL), F32)
    xh_ref[0:halo, :] = jnp.where(first, zero, prev_ref[...]).astype(BF16)
    xh_ref[halo:halo + tm, :] = x.astype(BF16)
    xh_ref[halo + tm:, :] = jnp.where(last, zero, next_ref[...]).astype(BF16)
    xh = xh_ref[...]

    def conv(u, col):
        w = cw_ref[:, col:col + tf]
        out = cb_ref[:, col:col + tf] + u[halo - 1:halo - 1 + tm] * w[0:1]
        out = out + u[halo:halo + tm] * w[1:2]
        return out + u[halo + 1:halo + 1 + tm] * w[2:3]

    acc = jnp.zeros((tm, D_MODEL), F32)
    for j in range(D_FF // tf):
        hg = conv(_dot(xh, wup_ref[:, j * tf:(j + 1) * tf]), j * tf)
        hu = conv(_dot(xh, wup_ref[:, D_FF + j * tf:D_FF + (j + 1) * tf]), D_FF + j * tf)
        act = (_silu(hg) * hu).astype(BF16)
        acc = acc + _dot(act, wdn_ref[j * tf:(j + 1) * tf, :])
    o_ref[...] = _layer_norm(ALPHA * x + acc, g_ref[...], b_ref[...])


def _ffn(x1, wup, cw, cb, wdn, ln_g, ln_b, *, seq, tm=512, tf=256):
    t = x1.shape[0]
    hb = tm // SUBLANES
    n_hb = t // SUBLANES
    full = lambda a: pl.BlockSpec(a.shape, lambda i: (0, 0))
    kern = functools.partial(_ffn_kernel, tiles_per_seq=seq // tm, tf=tf)
    return pl.pallas_call(
        kern,
        grid=(t // tm,),
        in_specs=[
            pl.BlockSpec((tm, D_MODEL), lambda i: (i, 0)),
            pl.BlockSpec((SUBLANES, D_MODEL), lambda i: (jnp.maximum(i * hb - 1, 0), 0)),
            pl.BlockSpec((SUBLANES, D_MODEL), lambda i: (jnp.minimum((i + 1) * hb, n_hb - 1), 0)),
            full(wup), full(cw), full(cb), full(wdn), full(ln_g), full(ln_b),
        ],
        out_specs=pl.BlockSpec((tm, D_MODEL), lambda i: (i, 0)),
        out_shape=jax.ShapeDtypeStruct((t, D_MODEL), F32),
        scratch_shapes=[pltpu.VMEM((tm + 2 * SUBLANES, D_MODEL), BF16)],
        compiler_params=pltpu.CompilerParams(
            dimension_semantics=("parallel",), vmem_limit_bytes=VMEM_LIMIT),
        name="ffn_ln2",
    )(x1, x1, x1, wup, cw, cb, wdn, ln_g, ln_b)


def _rotary_tables(seq):
    half = ROT_DIM // 2
    pos = jnp.arange(seq, dtype=F32)
    inv_freq = jnp.power(ROPE_THETA, -jnp.arange(0, ROT_DIM, 2, dtype=F32) / ROT_DIM)
    ang = pos[:, None] * inv_freq[None, :]
    cos, sin = jnp.cos(ang), jnp.sin(ang)
    pad = jnp.zeros((seq, HEAD_DIM_A - ROT_DIM), F32)
    zer = jnp.zeros((seq, half), F32)
    c64 = jnp.concatenate([cos, cos, pad + 1.0], axis=1)
    sa64 = jnp.concatenate([-sin, zer, pad], axis=1)
    sb64 = jnp.concatenate([zer, sin, pad], axis=1)
    tile2 = lambda a: jnp.concatenate([a, a], axis=1)
    return tile2(c64), tile2(sa64), tile2(sb64)


def _group_rows(p, n_chunks):
    nh = 2 * HEADS_PER_GROUP
    pg = p.reshape(2, N_GROUPS, HEADS_PER_GROUP).transpose(1, 0, 2).reshape(N_GROUPS, 1, nh)
    return jnp.broadcast_to(pg, (N_GROUPS, n_chunks, nh)).reshape(N_GROUPS * n_chunks * nh, 1)


def kernel(x, w_in, b_gate, lambda_q1, lambda_k1, lambda_q2, lambda_k2, attn_subln_w, conv_ssd_w, conv_ssd_b,
           dt_bias, a_log, d_skip, ssd_norm_w, w_proj_attn, w_proj_ssd, w_out, ln1_g, ln1_b, w_up, conv_ffn_w,
           conv_ffn_b, w_down, ln2_g, ln2_b):
    batch, seq, _ = x.shape
    t = batch * seq
    l = 0
    xbc_w = D_INNER + 2 * N_GROUPS * D_STATE
    dt_off = Z_OFF + D_INNER + xbc_w
    w = w_in[l]
    w_main = jnp.concatenate([w[:, :ATTN_W] * (HEAD_DIM_A ** -0.5 * LOG2_E), w[:, ATTN_W:dt_off],
                              w[:, dt_off + 2 * N_HEADS_S:]], axis=1).astype(BF16)
    w_dt = w[:, dt_off:dt_off + 2 * N_HEADS_S]
    w_dt_t = (w_dt.reshape(D_MODEL, 2, N_GROUPS, HEADS_PER_GROUP).transpose(2, 1, 3, 0)
              .reshape(2 * N_HEADS_S, D_MODEL).astype(BF16))
    cos_t, sa_t, sb_t = _rotary_tables(seq)

    x2 = x.reshape(t, D_MODEL)
    proj2, dt_t = _in_proj(x2, w_main, w_dt_t, cos_t, sa_t, sb_t, batch=batch, seq=seq)
    proj3 = proj2.reshape(batch, seq, PROJ_W)

    lam_p = jnp.stack([lambda_q1[l], lambda_k1[l], lambda_q2[l], lambda_k2[l]]).astype(F32)
    attn = _attention(proj3, lam_p, attn_subln_w[l].reshape(1, 2 * HEAD_DIM_A))

    cw = conv_ssd_w[l]
    cb = conv_ssd_b[l].reshape(1, xbc_w)
    nb = N_GROUPS * D_STATE
    ssd = _ssd(proj3, dt_t,
               cw[:, :D_INNER], cw[:, D_INNER:D_INNER + nb], cw[:, D_INNER + nb:],
               cb[:, :D_INNER], cb[:, D_INNER:D_INNER + nb], cb[:, D_INNER + nb:],
               _group_rows(dt_bias[l], seq // CHUNK), _group_rows(a_log[l], seq // CHUNK),
               jnp.repeat(d_skip[l], HEAD_DIM_S).reshape(1, D_INNER), ssd_norm_w[l].reshape(1, D_INNER))

    x1 = _merge(x2, attn.reshape(t, ATTN_W), ssd.reshape(t, D_INNER), proj2,
                w_proj_attn[l].astype(BF16), w_proj_ssd[l].astype(BF16), w_out[l].astype(BF16),
                b_gate[l].reshape(1, 2 * D_MODEL), ln1_g[l].reshape(1, D_MODEL), ln1_b[l].reshape(1, D_MODEL))

    out = _ffn(x1, w_up[l].astype(BF16), conv_ffn_w[l], conv_ffn_b[l].reshape(1, 2 * D_FF),
               w_down[l].astype(BF16), ln2_g[l].reshape(1, D_MODEL), ln2_b[l].reshape(1, D_MODEL), seq=seq)
    return out.reshape(batch, seq, D_MODEL)
```

```python
import functools
import math

import jax
import jax.numpy as jnp
from jax import lax
from jax.experimental import pallas as pl
from jax.experimental.pallas import tpu as pltpu

F32 = jnp.float32
BF16 = jnp.bfloat16

D_MODEL = 1024
N_HEADS_A = 8
HEAD_DIM_A = 64
ROT_DIM = HEAD_DIM_A // 4
ROPE_THETA = 500000.0
ATTN_W = N_HEADS_A * 2 * HEAD_DIM_A
D_INNER = 2 * D_MODEL
HEAD_DIM_S = 64
N_HEADS_S = D_INNER // HEAD_DIM_S
N_GROUPS = 8
HEADS_PER_GROUP = N_HEADS_S // N_GROUPS
GROUP_W = HEADS_PER_GROUP * HEAD_DIM_S
D_STATE = 128
CONV_SSD = 5
CHUNK = 128
D_FF = 2816
DEPTH = 1
ALPHA = (2 * DEPTH) ** 0.25
LN_EPS = 1e-5
RMS_EPS = 1e-5
LAM_INIT = 0.8 - 0.6 * math.exp(-0.3 * 0)
LOG2_E = math.log2(math.e)

LANES = 128
SUBLANES = 8
VMEM_LIMIT = 56 * 1024 * 1024

Q_OFF, K_OFF, V_OFF = 0, ATTN_W, 2 * ATTN_W
Z_OFF = 3 * ATTN_W
XS_OFF = Z_OFF + D_INNER
B_OFF = XS_OFF + D_INNER
C_OFF = B_OFF + N_GROUPS * D_STATE
GATE_OFF = C_OFF + N_GROUPS * D_STATE
PROJ_W = GATE_OFF + 2 * D_MODEL


def _dot(a, b):
    return jnp.dot(a, b, preferred_element_type=F32)


def _dot_nt(a, b):
    return lax.dot_general(a, b, (((1,), (1,)), ((), ())), preferred_element_type=F32)


def _dot_exact(a, b):
    return jnp.dot(a, b, preferred_element_type=F32, precision=lax.Precision.HIGHEST)


def _in_proj_kernel(x_ref, w_ref, wdt_ref, o_ref, dt_ref, xb_ref):
    @pl.when(pl.program_id(1) == 0)
    def _():
        xb = x_ref[...].astype(BF16)
        xb_ref[...] = xb
        dt_t = _dot_nt(wdt_ref[...], xb)
        for c in range(dt_ref.shape[0]):
            dt_ref[c] = dt_t[:, c * CHUNK:(c + 1) * CHUNK]

    o_ref[...] = _dot(xb_ref[...], w_ref[...]).astype(o_ref.dtype)


def _in_proj(x2, w_main, w_dt_t, *, batch, seq, tm=1024, tn=1024):
    t = x2.shape[0]
    n_seq_tiles = seq // tm
    n_dt = w_dt_t.shape[0]
    return pl.pallas_call(
        _in_proj_kernel,
        grid=(t // tm, PROJ_W // tn),
        in_specs=[
            pl.BlockSpec((tm, D_MODEL), lambda i, j: (i, 0)),
            pl.BlockSpec((D_MODEL, tn), lambda i, j: (0, j)),
            pl.BlockSpec((n_dt, D_MODEL), lambda i, j: (0, 0)),
        ],
        out_specs=[
            pl.BlockSpec((tm, tn), lambda i, j: (i, j)),
            pl.BlockSpec((None, tm // CHUNK, n_dt, CHUNK),
                         lambda i, j: (i // n_seq_tiles, i % n_seq_tiles, 0, 0)),
        ],
        out_shape=[
            jax.ShapeDtypeStruct((t, PROJ_W), BF16),
            jax.ShapeDtypeStruct((batch, seq // CHUNK, n_dt, CHUNK), F32),
        ],
        scratch_shapes=[pltpu.VMEM((tm, D_MODEL), BF16)],
        compiler_params=pltpu.CompilerParams(
            dimension_semantics=("parallel", "arbitrary"), vmem_limit_bytes=VMEM_LIMIT),
        name="in_proj",
    )(x2, w_main, w_dt_t)


def _rotary(a_ref, tab_ref, rows):
    a = a_ref[rows, :].astype(F32)
    up = pltpu.roll(a, LANES - ROT_DIM // 2, axis=1)
    dn = pltpu.roll(a, ROT_DIM // 2, axis=1)
    return (a * tab_ref[0, rows, :] + up * tab_ref[1, rows, :] + dn * tab_ref[2, rows, :]).astype(BF16)


def _attn_kernel(q_ref, k_ref, v_ref, tq_ref, tk_ref, lam_ref, sw_ref, o_ref, kr_ref, v1_ref, s1_ref, s2_ref, *, tk):
    tq = q_ref.shape[0]
    seq = k_ref.shape[0]
    n_kc = seq // tk

    @pl.when(pl.program_id(2) == 0)
    def _():
        v1_ref[:, :LANES] = v_ref[...]
        v1_ref[:, LANES:] = jnp.ones((seq, LANES), BF16)
        for c in range(n_kc):
            rows = slice(c * tk, (c + 1) * tk)
            kr_ref[rows, :] = _rotary(k_ref, tk_ref, rows)

    q = _rotary(q_ref, tq_ref, slice(None))
    lane = lax.broadcasted_iota(jnp.int32, q.shape, 1)
    zero = jnp.zeros_like(q)
    q1 = jnp.where(lane < HEAD_DIM_A, q, zero)
    q2 = jnp.where(lane >= HEAD_DIM_A, q, zero)

    m1 = jnp.full((tq, LANES), -jnp.inf, F32)
    m2 = m1
    for c in range(n_kc):
        kc = kr_ref[c * tk:(c + 1) * tk, :]
        sc1 = _dot_nt(q1, kc)
        sc2 = _dot_nt(q2, kc)
        s1_ref[c] = sc1
        s2_ref[c] = sc2
        for u in range(tk // LANES):
            m1 = jnp.maximum(m1, sc1[:, u * LANES:(u + 1) * LANES])
            m2 = jnp.maximum(m2, sc2[:, u * LANES:(u + 1) * LANES])
    m1 = jnp.max(m1, axis=1, keepdims=True)
    m2 = jnp.max(m2, axis=1, keepdims=True)

    a1 = jnp.zeros((tq, 2 * LANES), F32)
    a2 = a1
    for c in range(n_kc):
        vc = v1_ref[c * tk:(c + 1) * tk, :]
        p1 = jnp.exp2(s1_ref[c] - m1).astype(BF16)
        p2 = jnp.exp2(s2_ref[c] - m2).astype(BF16)
        a1 = a1 + _dot(p1, vc)
        a2 = a2 + _dot(p2, vc)

    lp = lam_ref[...]
    lam = (jnp.exp(jnp.sum(lp[0:1] * lp[1:2], axis=1, keepdims=True))
           - jnp.exp(jnp.sum(lp[2:3] * lp[3:4], axis=1, keepdims=True)) + LAM_INIT)
    o =a1[:, :LANES] / a1[:, LANES:] - lam * (a2[:, :LANES] / a2[:, LANES:])
    ms = jnp.mean(o * o, axis=1, keepdims=True)
    o = o * lax.rsqrt(ms + RMS_EPS) * sw_ref[...] * (1.0 - LAM_INIT)
    o_ref[...] = o.astype(o_ref.dtype)


def _attention(proj3, rot_tab, lam_p, subln_w, *, tq=512, tk=512):
    batch, seq, _ = proj3.shape
    hw = 2 * HEAD_DIM_A
    kern = functools.partial(_attn_kernel, tk=tk)
    return pl.pallas_call(
        kern,
        grid=(batch, N_HEADS_A, seq // tq),
        in_specs=[
            pl.BlockSpec((None, tq, hw), lambda b, h, i: (b, i, Q_OFF // hw + h)),
            pl.BlockSpec((None, seq, hw), lambda b, h, i: (b, 0, K_OFF // hw + h)),
            pl.BlockSpec((None, seq, hw), lambda b, h, i: (b, 0, V_OFF // hw + h)),
            pl.BlockSpec((3, tq, hw), lambda b, h, i: (0, i, 0)),
            pl.BlockSpec((3, seq, hw), lambda b, h, i: (0, 0, 0)),
            pl.BlockSpec((4, HEAD_DIM_A), lambda b, h, i: (0, 0)),
            pl.BlockSpec((1, hw), lambda b, h, i: (0, 0)),
        ],
        out_specs=pl.BlockSpec((None, tq, hw), lambda b, h, i: (b, i, h)),
        out_shape=jax.ShapeDtypeStruct((batch, seq, ATTN_W), BF16),
        scratch_shapes=[
            pltpu.VMEM((seq, hw), BF16),
            pltpu.VMEM((seq, 2 * LANES), BF16),
            pltpu.VMEM((seq // tk, tq, tk), F32),
            pltpu.VMEM((seq // tk, tq, tk), F32),
        ],
        compiler_params=pltpu.CompilerParams(
            dimension_semantics=("parallel", "parallel", "arbitrary"), vmem_limit_bytes=VMEM_LIMIT),
        name="diff_attn",
    )(proj3, proj3, proj3, rot_tab, rot_tab, lam_p, subln_w)


def _silu(v):
    return v * jax.nn.sigmoid(v)


def _ssd_kernel(z_ref, xs_ref, b_ref, c_ref, dt_ref, cwx_ref, cwb_ref, cwc_ref, cbx_ref, cbb_ref, cbc_ref,
                dtb_ref, alog_ref, dsk_ref, nw_ref, o_ref,
                pad_ref, xc_ref, bc_ref, cc_ref, y_ref, cb_ref, src_ref, cs_ref, st_ref):
    seq = xs_ref.shape[0]
    n_chunks = seq // CHUNK
    halo = SUBLANES
    half = CONV_SSD // 2
    gw = GROUP_W
    w_all = gw + 2 * D_STATE

    pad_ref[0:halo, :] = jnp.zeros((halo, w_all), F32)
    pad_ref[halo + seq:, :] = jnp.zeros((halo, w_all), F32)
    pad_ref[halo:halo + seq, 0:gw] = xs_ref[...].astype(F32)
    pad_ref[halo:halo + seq, gw:gw + D_STATE] = b_ref[...].astype(F32)
    pad_ref[halo:halo + seq, gw + D_STATE:] = c_ref[...].astype(F32)
    cw = jnp.concatenate([cwx_ref[...], cwb_ref[...], cwc_ref[...]], axis=1)
    cbias = jnp.concatenate([cbx_ref[...], cbb_ref[...], cbc_ref[...]], axis=1)

    def conv_chunk(c, carry):
        t0 = pl.multiple_of(c * CHUNK, CHUNK)
        win = pad_ref[pl.ds(t0, CHUNK + 2 * halo), :]
        acc = jnp.zeros((CHUNK, w_all), F32) + cbias
        for k in range(CONV_SSD):
            off = halo - half + k
            acc = acc + win[off:off + CHUNK, :] * cw[k:k + 1, :]
        acc = _silu(acc)
        xc_ref[pl.ds(t0, CHUNK), :] = acc[:, 0:gw]
        bc_ref[pl.ds(t0, CHUNK), :] = acc[:, gw:gw + D_STATE]
        cc_ref[pl.ds(t0, CHUNK), :] = acc[:, gw + D_STATE:]
        cb_ref[c] = _dot_nt(acc[:, gw + D_STATE:].astype(BF16), acc[:, gw:gw + D_STATE].astype(BF16))
        return carry

    lax.fori_loop(0, n_chunks, conv_chunk, 0, unroll=2)

    nh = 2 * HEADS_PER_GROUP
    r_i = lax.broadcasted_iota(jnp.int32, (CHUNK, CHUNK), 0)
    c_i = lax.broadcasted_iota(jnp.int32, (CHUNK, CHUNK), 1)
    upper = (r_i <= c_i).astype(F32)
    lower = (r_i >= c_i).astype(F32)
    raw = dt_ref[...].reshape(n_chunks * nh, CHUNK) + dtb_ref[...]
    dt_all = jnp.maximum(raw, 0.0) + jnp.log1p(jnp.exp(-jnp.abs(raw)))
    da = dt_all * (-jnp.exp(alog_ref[...]))
    row_all = lax.broadcasted_iota(jnp.int32, (n_chunks * nh, CHUNK), 0)
    cs_all = jnp.where(row_all % nh < HEADS_PER_GROUP, _dot_exact(da, upper), _dot_exact(da, lower))
    cs2_all = cs_all * LOG2_E
    cs_ref[...] = cs2_all.reshape(n_chunks, nh, CHUNK)
    src_ref[...] = (cs2_all - jnp.log2(dt_all)).reshape(n_chunks, nh, CHUNK)

    lane_head = lax.broadcasted_iota(jnp.int32, (1, gw), 1) // HEAD_DIM_S

    def chunk_step(c, reverse):
        r0 = HEADS_PER_GROUP if reverse else 0
        t0 = pl.multiple_of(c * CHUNK, CHUNK)
        x = xc_ref[pl.ds(t0, CHUNK), :]
        bm = bc_ref[pl.ds(t0, CHUNK), :]
        cm = cc_ref[pl.ds(t0, CHUNK), :].astype(BF16)
        cs2 = cs_ref[c]
        src = src_ref[c]
        cbm = cb_ref[c]
        if reverse:
            tot = cs2[:, 0:1]
            keep = r_i <= c_i
        else:
            tot = cs2[:, CHUNK - 1:CHUNK]
            keep = r_i >= c_i
        cs2_t = jnp.transpose(cs2)
        g_col = jnp.exp2(cs2_t)
        g_out = g_col[:, r0 + HEADS_PER_GROUP - 1:r0 + HEADS_PER_GROUP]
        for e in range(HEADS_PER_GROUP - 2, -1, -1):
            g_out = jnp.where(lane_head == e, g_col[:, r0 + e:r0 + e + 1], g_out)
        sel_row = (lax.broadcasted_iota(jnp.int32, (nh, gw), 0) == lane_head + r0).astype(F32)
        cdec = jnp.exp2(jnp.sum(tot * sel_row, axis=0, keepdims=True))
        wrow = jnp.exp2(tot - src)
        bt = jnp.transpose(bm)
        xb = x.astype(BF16)
        zero = jnp.zeros_like(xb)
        ms, bs, xs = [], [], []
        for e in range(HEADS_PER_GROUP):
            r = r0 + e
            seg = cs2_t[:, r:r + 1] - src[r:r + 1, :]
            decay = jnp.exp2(jnp.where(keep, seg, -jnp.inf))
            ms.append((cbm * decay).astype(BF16))
            bs.append((bt * wrow[r:r + 1, :]).astype(BF16))
            xs.append(jnp.where(lane_head == e, xb, zero))
        xcat = jnp.concatenate(xs, axis=0)
        d = 1 if reverse else 0
        st = st_ref[d]
        y = _dot(jnp.concatenate(ms, axis=1), xcat) + _dot(cm, st.astype(BF16)) * g_out
        st_ref[d] = st * cdec + _dot(jnp.concatenate(bs, axis=1), xcat)
        y_ref[d, pl.ds(t0, CHUNK), :] = y

    st_ref[...] = jnp.zeros_like(st_ref)

    def scan(i, carry):
        chunk_step(i, False)
        chunk_step(n_chunks - 1 - i, True)
        return carry

    lax.fori_loop(0, n_chunks, scan, 0, unroll=2)

    def finish(c, carry):
        t0 = pl.multiple_of(c * CHUNK, CHUNK)
        rows = pl.ds(t0, CHUNK)
        y = y_ref[0, rows, :] + y_ref[1, rows, :] + xc_ref[rows, :] * dsk_ref[...]
        y = y * _silu(z_ref[rows, :].astype(F32))
        ms = jnp.mean(y * y, axis=1, keepdims=True)
        y = y * lax.rsqrt(ms + RMS_EPS) * nw_ref[...]
        o_ref[rows, :] = y.astype(o_ref.dtype)
        return carry

    lax.fori_loop(0, n_chunks, finish, 0, unroll=4)


def _ssd(proj3, dt_t, cw_x, cw_b, cw_c, cb_x, cb_b, cb_c, dtb, alog, dsk, nw):
    batch, seq, _ = proj3.shape
    gw, ns = GROUP_W, D_STATE
    nh = 2 * HEADS_PER_GROUP
    n_chunks = seq // CHUNK
    gspec = lambda w: pl.BlockSpec((1, w), lambda b, g: (0, g))
    return pl.pallas_call(
        _ssd_kernel,
        grid=(batch, N_GROUPS),
        in_specs=[
            pl.BlockSpec((None, seq, gw), lambda b, g: (b, 0, Z_OFF // gw + g)),
            pl.BlockSpec((None, seq, gw), lambda b, g: (b, 0, XS_OFF // gw + g)),
            pl.BlockSpec((None, seq, ns), lambda b, g: (b, 0, B_OFF // ns + g)),
            pl.BlockSpec((None, seq, ns), lambda b, g: (b, 0, C_OFF // ns + g)),
            pl.BlockSpec((None, n_chunks, nh, CHUNK), lambda b, g: (b, 0, g, 0)),
            pl.BlockSpec((CONV_SSD, gw), lambda b, g: (0, g)),
            pl.BlockSpec((CONV_SSD, ns), lambda b, g: (0, g)),
            pl.BlockSpec((CONV_SSD, ns), lambda b, g: (0, g)),
            gspec(gw), gspec(ns), gspec(ns),
            pl.BlockSpec((n_chunks * nh, 1), lambda b, g: (g, 0)),
            pl.BlockSpec((n_chunks * nh, 1), lambda b, g: (g, 0)),
            gspec(gw), gspec(gw),
        ],
        out_specs=pl.BlockSpec((None, seq, gw), lambda b, g: (b, 0, g)),
        out_shape=jax.ShapeDtypeStruct((batch, seq, D_INNER), BF16),
        scratch_shapes=[
            pltpu.VMEM((seq + 2 * SUBLANES, gw + 2 * ns), F32),
            pltpu.VMEM((seq, gw), F32),
            pltpu.VMEM((seq, ns), F32),
            pltpu.VMEM((seq, ns), F32),
            pltpu.VMEM((2, seq, gw), F32),
            pltpu.VMEM((n_chunks, CHUNK, CHUNK), F32),
            pltpu.VMEM((n_chunks, nh, CHUNK), F32),
            pltpu.VMEM((n_chunks, nh, CHUNK), F32),
            pltpu.VMEM((2, ns, gw), F32),
        ],
        compiler_params=pltpu.CompilerParams(
            dimension_semantics=("parallel", "parallel"), vmem_limit_bytes=VMEM_LIMIT),
        name="ssd",
    )(proj3, proj3, proj3, proj3, dt_t, cw_x, cw_b, cw_c, cb_x, cb_b, cb_c, dtb, alog, dsk, nw)


def _layer_norm(v, g, b):
    mu = jnp.mean(v, axis=1, keepdims=True)
    d = v - mu
    var = jnp.mean(d * d, axis=1, keepdims=True)
    return d * lax.rsqrt(var + LN_EPS) * g + b


def _merge_kernel(x_ref, at_ref, sd_ref, ga_ref, gs_ref, wpa_ref, wps_ref, wo_ref, bg_ref, g_ref, b_ref, o_ref):
    bg = bg_ref[...]
    g_a = jax.nn.sigmoid(ga_ref[...].astype(F32) + bg[:, :D_MODEL])
    g_s = jax.nn.sigmoid(gs_ref[...].astype(F32) + bg[:, D_MODEL:])
    merged = g_a * _dot(at_ref[...], wpa_ref[...]) + g_s * _dot(sd_ref[...], wps_ref[...])
    y = ALPHA * x_ref[...] + _dot(merged.astype(BF16), wo_ref[...])
    o_ref[...] = _layer_norm(y, g_ref[...], b_ref[...])


def _merge(x2, attn2, ssd2, proj2, wpa, wps, wo, bg, ln_g, ln_b, *, tm=512):
    t = x2.shape[0]
    full = lambda a: pl.BlockSpec(a.shape, lambda i: (0, 0))
    return pl.pallas_call(
        _merge_kernel,
        grid=(t // tm,),
        in_specs=[
            pl.BlockSpec((tm, D_MODEL), lambda i: (i, 0)),
            pl.BlockSpec((tm, ATTN_W), lambda i: (i, 0)),
            pl.BlockSpec((tm, D_INNER), lambda i: (i, 0)),
            pl.BlockSpec((tm, D_MODEL), lambda i: (i, GATE_OFF // D_MODEL)),
            pl.BlockSpec((tm, D_MODEL), lambda i: (i, GATE_OFF // D_MODEL + 1)),
            full(wpa), full(wps), full(wo), full(bg), full(ln_g), full(ln_b),
        ],
        out_specs=pl.BlockSpec((tm, D_MODEL), lambda i: (i, 0)),
        out_shape=jax.ShapeDtypeStruct((t, D_MODEL), F32),
        compiler_params=pltpu.CompilerParams(
            dimension_semantics=("parallel",), vmem_limit_bytes=VMEM_LIMIT),
        name="merge_ln1",
    )(x2, attn2, ssd2, proj2, proj2, wpa, wps, wo, bg, ln_g, ln_b)


def _ffn_kernel(x_ref, prev_ref, next_ref, wup_ref, cw_ref, cb_ref, wdn_ref, g_ref, b_ref, o_ref, xh_ref,
                ug0_ref, uu0_ref, ug1_ref, uu1_ref, *, tiles_per_seq, tf):
    i = pl.program_id(0)
    tm = x_ref.shape[0]
    halo = SUBLANES
    x = x_ref[...]
    first = (i % tiles_per_seq) == 0
    last = (i % tiles_per_seq) == tiles_per_seq - 1
    zero = jnp.zeros((halo, D_MODEL), F32)
    xh_ref[0:halo, :] = jnp.where(first, zero, prev_ref[...]).astype(BF16)
    xh_ref[halo:halo + tm, :] = x.astype(BF16)
    xh_ref[halo + tm:, :] = jnp.where(last, zero, next_ref[...]).astype(BF16)
    xh = xh_ref[...]

    def conv(u_ref, col):
        w = cw_ref[:, col:col + tf]
        out = cb_ref[:, col:col + tf] + u_ref[halo - 1:halo - 1 + tm, :] * w[0:1]
        out = out + u_ref[halo:halo + tm, :] * w[1:2]
        return out + u_ref[halo + 1:halo + 1 + tm, :] * w[2:3]

    bufs = ((ug0_ref, uu0_ref), (ug1_ref, uu1_ref))

    def up(j):
        ug_ref, uu_ref = bufs[j % 2]
        ug_ref[...] = _dot(xh, wup_ref[:, j * tf:(j + 1) * tf])
        uu_ref[...] = _dot(xh, wup_ref[:, D_FF + j * tf:D_FF + (j + 1) * tf])

    n_f = D_FF // tf
    acc = jnp.zeros((tm, D_MODEL), F32)
    up(0)
    for j in range(n_f):
        if j + 1 < n_f:
            up(j + 1)
        ug_ref, uu_ref = bufs[j % 2]
        act = (_silu(conv(ug_ref, j * tf)) * conv(uu_ref, D_FF + j * tf)).astype(BF16)
        acc = acc + _dot(act, wdn_ref[j * tf:(j + 1) * tf, :])
    o_ref[...] = _layer_norm(ALPHA * x + acc, g_ref[...], b_ref[...])


def _ffn(x1, wup, cw, cb, wdn, ln_g, ln_b, *, seq, tm=512, tf=256):
    t = x1.shape[0]
    hb = tm // SUBLANES
    n_hb = t // SUBLANES
    full = lambda a: pl.BlockSpec(a.shape, lambda i: (0, 0))
    kern = functools.partial(_ffn_kernel, tiles_per_seq=seq // tm, tf=tf)
    return pl.pallas_call(
        kern,
        grid=(t // tm,),
        in_specs=[
            pl.BlockSpec((tm, D_MODEL), lambda i: (i, 0)),
            pl.BlockSpec((SUBLANES, D_MODEL), lambda i: (jnp.maximum(i * hb - 1, 0), 0)),
            pl.BlockSpec((SUBLANES, D_MODEL), lambda i: (jnp.minimum((i + 1) * hb, n_hb - 1), 0)),
            full(wup), full(cw), full(cb), full(wdn), full(ln_g), full(ln_b),
        ],
        out_specs=pl.BlockSpec((tm, D_MODEL), lambda i: (i, 0)),
        out_shape=jax.ShapeDtypeStruct((t, D_MODEL), F32),
        scratch_shapes=([pltpu.VMEM((tm + 2 * SUBLANES, D_MODEL), BF16)]
                        + [pltpu.VMEM((tm + 2 * SUBLANES, tf), F32)] * 4),
        compiler_params=pltpu.CompilerParams(
            dimension_semantics=("parallel",), vmem_limit_bytes=VMEM_LIMIT),
        name="ffn_ln2",
    )(x1, x1, x1, wup, cw, cb, wdn, ln_g, ln_b)


def _rotary_tables(seq):
    half = ROT_DIM // 2
    pos = jnp.arange(seq, dtype=F32)
    inv_freq = jnp.power(ROPE_THETA, -jnp.arange(0, ROT_DIM, 2, dtype=F32) / ROT_DIM)
    ang = pos[:, None] * inv_freq[None, :]
    cos, sin = jnp.cos(ang), jnp.sin(ang)
    pad = jnp.zeros((seq, HEAD_DIM_A - ROT_DIM), F32)
    zer = jnp.zeros((seq, half), F32)
    c64 = jnp.concatenate([cos, cos, pad + 1.0], axis=1)
    sa64 = jnp.concatenate([-sin, zer, pad], axis=1)
    sb64 = jnp.concatenate([zer, sin, pad], axis=1)
    tile2 = lambda a: jnp.concatenate([a, a], axis=1)
    return jnp.stack([tile2(c64), tile2(sa64), tile2(sb64)])


def _group_rows(p, n_chunks):
    nh = 2 * HEADS_PER_GROUP
    pg = p.reshape(2, N_GROUPS, HEADS_PER_GROUP).transpose(1, 0, 2).reshape(N_GROUPS, 1, nh)
    return jnp.broadcast_to(pg, (N_GROUPS, n_chunks, nh)).reshape(N_GROUPS * n_chunks * nh, 1)


def kernel(x, w_in, b_gate, lambda_q1, lambda_k1, lambda_q2, lambda_k2, attn_subln_w, conv_ssd_w, conv_ssd_b,
           dt_bias, a_log, d_skip, ssd_norm_w, w_proj_attn, w_proj_ssd, w_out, ln1_g, ln1_b, w_up, conv_ffn_w,
           conv_ffn_b, w_down, ln2_g, ln2_b):
    batch, seq, _ = x.shape
    t = batch * seq
    l = 0
    xbc_w = D_INNER + 2 * N_GROUPS * D_STATE
    dt_off = Z_OFF + D_INNER + xbc_w
    w = w_in[l]
    w_main = jnp.concatenate([w[:, :ATTN_W] * (HEAD_DIM_A ** -0.5 * LOG2_E), w[:, ATTN_W:dt_off],
                              w[:, dt_off + 2 * N_HEADS_S:]], axis=1).astype(BF16)
    w_dt = w[:, dt_off:dt_off + 2 * N_HEADS_S]
    w_dt_t = (w_dt.reshape(D_MODEL, 2, N_GROUPS, HEADS_PER_GROUP).transpose(2, 1, 3, 0)
              .reshape(2 * N_HEADS_S, D_MODEL).astype(BF16))
    rot_tab = _rotary_tables(seq)

    x2 = x.reshape(t, D_MODEL)
    proj2, dt_t = _in_proj(x2, w_main, w_dt_t, batch=batch, seq=seq)
    proj3 = proj2.reshape(batch, seq, PROJ_W)

    lam_p = jnp.stack([lambda_q1[l], lambda_k1[l], lambda_q2[l], lambda_k2[l]]).astype(F32)
    attn = _attention(proj3, rot_tab, lam_p, attn_subln_w[l].reshape(1, 2 * HEAD_DIM_A))

    cw = conv_ssd_w[l]
    cb = conv_ssd_b[l].reshape(1, xbc_w)
    nb = N_GROUPS * D_STATE
    ssd = _ssd(proj3, dt_t,
               cw[:, :D_INNER], cw[:, D_INNER:D_INNER + nb], cw[:, D_INNER + nb:],
               cb[:, :D_INNER], cb[:, D_INNER:D_INNER + nb], cb[:, D_INNER + nb:],
               _group_rows(dt_bias[l], seq // CHUNK), _group_rows(a_log[l], seq // CHUNK),
               jnp.repeat(d_skip[l], HEAD_DIM_S).reshape(1, D_INNER), ssd_norm_w[l].reshape(1, D_INNER))

    x1 = _merge(x2, attn.reshape(t, ATTN_W), ssd.reshape(t, D_INNER), proj2,
                w_proj_attn[l].astype(BF16), w_proj_ssd[l].astype(BF16), w_out[l].astype(BF16),
                b_gate[l].reshape(1, 2 * D_MODEL), ln1_g[l].reshape(1, D_MODEL), ln1_b[l].reshape(1, D_MODEL))

    out = _ffn(x1, w_up[l].astype(BF16), conv_ffn_w[l], conv_ffn_b[l].reshape(1, 2 * D_FF),
               w_down[l].astype(BF16), ln2_g[l].reshape(1, D_MODEL), ln2_b[l].reshape(1, D_MODEL), seq=seq)
    return out.reshape(batch, seq, D_MODEL)
```

```python
import functools
import math

import jax
import jax.numpy as jnp
from jax import lax
from jax.experimental import pallas as pl
from jax.experimental.pallas import tpu as pltpu

F32 = jnp.float32
BF16 = jnp.bfloat16

D_MODEL = 1024
N_HEADS_A = 8
HEAD_DIM_A = 64
ROT_DIM = HEAD_DIM_A // 4
ROPE_THETA = 500000.0
ATTN_W = N_HEADS_A * 2 * HEAD_DIM_A
D_INNER = 2 * D_MODEL
HEAD_DIM_S = 64
N_HEADS_S = D_INNER // HEAD_DIM_S
N_GROUPS = 8
HEADS_PER_GROUP = N_HEADS_S // N_GROUPS
GROUP_W = HEADS_PER_GROUP * HEAD_DIM_S
D_STATE = 128
CONV_SSD = 5
CHUNK = 128
D_FF = 2816
DEPTH = 1
ALPHA = (2 * DEPTH) ** 0.25
LN_EPS = 1e-5
RMS_EPS = 1e-5
LAM_INIT = 0.8 - 0.6 * math.exp(-0.3 * 0)
LOG2_E = math.log2(math.e)

LANES = 128
SUBLANES = 8
VMEM_LIMIT = 56 * 1024 * 1024

Q_OFF, K_OFF, V_OFF = 0, ATTN_W, 2 * ATTN_W
Z_OFF = 3 * ATTN_W
XS_OFF = Z_OFF + D_INNER
B_OFF = XS_OFF + D_INNER
C_OFF = B_OFF + N_GROUPS * D_STATE
GATE_OFF = C_OFF + N_GROUPS * D_STATE
PROJ_W = GATE_OFF + 2 * D_MODEL


def _dot(a, b):
    return jnp.dot(a, b, preferred_element_type=F32)


def _dot_nt(a, b):
    return lax.dot_general(a, b, (((1,), (1,)), ((), ())), preferred_element_type=F32)


def _dot_exact(a, b):
    return jnp.dot(a, b, preferred_element_type=F32, precision=lax.Precision.HIGHEST)


def _in_proj_kernel(x_ref, wa_ref, wb_ref, wdt_ref, o_ref, dt_ref, xb_ref, *, n_q, n_a):
    j = pl.program_id(1)

    @pl.when(j == 0)
    def _():
        xb = x_ref[...].astype(BF16)
        xb_ref[...] = xb
        dt_t = _dot_nt(wdt_ref[...], xb)
        for c in range(dt_ref.shape[0]):
            dt_ref[c] = dt_t[:, c * CHUNK:(c + 1) * CHUNK]

    @pl.when(j < n_q)
    def _():
        o_ref[...] = (_dot(xb_ref[...], wa_ref[...]) * (HEAD_DIM_A ** -0.5 * LOG2_E)).astype(o_ref.dtype)

    @pl.when((j >= n_q) & (j < n_a))
    def _():
        o_ref[...] = _dot(xb_ref[...], wa_ref[...]).astype(o_ref.dtype)

    @pl.when(j >= n_a)
    def _():
        o_ref[...] = _dot(xb_ref[...], wb_ref[...]).astype(o_ref.dtype)


def _in_proj(x2, w_a, w_b, w_dt_t, *, batch, seq, tm=1024, tn=1024):
    t = x2.shape[0]
    n_seq_tiles = seq // tm
    n_dt = w_dt_t.shape[0]
    n_a = GATE_OFF // tn
    kern = functools.partial(_in_proj_kernel, n_q=ATTN_W // tn, n_a=n_a)
    return pl.pallas_call(
        kern,
        grid=(t // tm, PROJ_W // tn),
        in_specs=[
            pl.BlockSpec((tm, D_MODEL), lambda i, j: (i, 0)),
            pl.BlockSpec((D_MODEL, tn), lambda i, j: (0, jnp.minimum(j, n_a - 1))),
            pl.BlockSpec((D_MODEL, tn), lambda i, j: (0, jnp.maximum(j - n_a, 0))),
            pl.BlockSpec((n_dt, D_MODEL), lambda i, j: (0, 0)),
        ],
        out_specs=[
            pl.BlockSpec((tm, tn), lambda i, j: (i, j)),
            pl.BlockSpec((None, tm // CHUNK, n_dt, CHUNK),
                         lambda i, j: (i // n_seq_tiles, i % n_seq_tiles, 0, 0)),
        ],
        out_shape=[
            jax.ShapeDtypeStruct((t, PROJ_W), BF16),
            jax.ShapeDtypeStruct((batch, seq // CHUNK, n_dt, CHUNK), F32),
        ],
        scratch_shapes=[pltpu.VMEM((tm, D_MODEL), BF16)],
        compiler_params=pltpu.CompilerParams(
            dimension_semantics=("parallel", "arbitrary"), vmem_limit_bytes=VMEM_LIMIT),
        name="in_proj",
    )(x2, w_a, w_b, w_dt_t)


def _rotary(a_ref, tab_ref, rows):
    a = a_ref[rows, :].astype(F32)
    up = pltpu.roll(a, LANES - ROT_DIM // 2, axis=1)
    dn = pltpu.roll(a, ROT_DIM // 2, axis=1)
    return (a * tab_ref[0, rows, :] + up * tab_ref[1, rows, :] + dn * tab_ref[2, rows, :]).astype(BF16)


def _attn_kernel(q_ref, k_ref, v_ref, tq_ref, tk_ref, lam_ref, sw_ref, o_ref, kr_ref, v1_ref, s1_ref, s2_ref, *, tk):
    tq = q_ref.shape[0]
    seq = k_ref.shape[0]
    n_kc = seq // tk

    @pl.when(pl.program_id(2) == 0)
    def _():
        v1_ref[:, :LANES] = v_ref[...]
        v1_ref[:, LANES:] = jnp.ones((seq, LANES), BF16)
        for c in range(n_kc):
            rows = slice(c * tk, (c + 1) * tk)
            kr_ref[rows, :] = _rotary(k_ref, tk_ref, rows)

    q = _rotary(q_ref, tq_ref, slice(None))
    lane = lax.broadcasted_iota(jnp.int32, q.shape, 1)
    zero = jnp.zeros_like(q)
    q1 = jnp.where(lane < HEAD_DIM_A, q, zero)
    q2 = jnp.where(lane >= HEAD_DIM_A, q, zero)

    m1 = jnp.full((tq, LANES), -jnp.inf, F32)
    m2 = m1
    for c in range(n_kc):
        kc = kr_ref[c * tk:(c + 1) * tk, :]
        sc1 = _dot_nt(q1, kc)
        sc2 = _dot_nt(q2, kc)
        s1_ref[c] = sc1
        s2_ref[c] = sc2
        for u in range(tk // LANES):
            m1 = jnp.maximum(m1, sc1[:, u * LANES:(u + 1) * LANES])
            m2 = jnp.maximum(m2, sc2[:, u * LANES:(u + 1) * LANES])
    m1 = jnp.max(m1, axis=1, keepdims=True)
    m2 = jnp.max(m2, axis=1, keepdims=True)

    a1 = jnp.zeros((tq, 2 * LANES), F32)
    a2 = a1
    for c in range(n_kc):
        vc = v1_ref[c * tk:(c + 1) * tk, :]
        p1 = jnp.exp2(s1_ref[c] - m1).astype(BF16)
        p2 = jnp.exp2(s2_ref[c] - m2).astype(BF16)
        a1 = a1 + _dot(p1, vc)
        a2 = a2 + _dot(p2, vc)

    lp = lam_ref[...]
    lam = (jnp.exp(jnp.sum(lp[0:1] * lp[1:2], axis=1, keepdims=True))
           - jnp.exp(jnp.sum(lp[2:3] * lp[3:4], axis=1, keepdims=True)) + LAM_INIT)
    o =a1[:, :LANES] / a1[:, LANES:] - lam * (a2[:, :LANES] / a2[:, LANES:])
    ms = jnp.mean(o * o, axis=1, keepdims=True)
    o = o * lax.rsqrt(ms + RMS_EPS) * sw_ref[...] * (1.0 - LAM_INIT)
    o_ref[...] = o.astype(o_ref.dtype)


def _attention(proj3, rot_tab, lam_p, subln_w, *, tq=512, tk=512):
    batch, seq, _ = proj3.shape
    hw = 2 * HEAD_DIM_A
    kern = functools.partial(_attn_kernel, tk=tk)
    return pl.pallas_call(
        kern,
        grid=(batch, N_HEADS_A, seq // tq),
        in_specs=[
            pl.BlockSpec((None, tq, hw), lambda b, h, i: (b, i, Q_OFF // hw + h)),
            pl.BlockSpec((None, seq, hw), lambda b, h, i: (b, 0, K_OFF // hw + h)),
            pl.BlockSpec((None, seq, hw), lambda b, h, i: (b, 0, V_OFF // hw + h)),
            pl.BlockSpec((3, tq, hw), lambda b, h, i: (0, i, 0)),
            pl.BlockSpec((3, seq, hw), lambda b, h, i: (0, 0, 0)),
            pl.BlockSpec((4, HEAD_DIM_A), lambda b, h, i: (0, 0)),
            pl.BlockSpec((1, hw), lambda b, h, i: (0, 0)),
        ],
        out_specs=pl.BlockSpec((None, tq, hw), lambda b, h, i: (b, i, h)),
        out_shape=jax.ShapeDtypeStruct((batch, seq, ATTN_W), BF16),
        scratch_shapes=[
            pltpu.VMEM((seq, hw), BF16),
            pltpu.VMEM((seq, 2 * LANES), BF16),
            pltpu.VMEM((seq // tk, tq, tk), F32),
            pltpu.VMEM((seq // tk, tq, tk), F32),
        ],
        compiler_params=pltpu.CompilerParams(
            dimension_semantics=("parallel", "parallel", "arbitrary"), vmem_limit_bytes=VMEM_LIMIT),
        name="diff_attn",
    )(proj3, proj3, proj3, rot_tab, rot_tab, lam_p, subln_w)


def _silu(v):
    return v * jax.nn.sigmoid(v)


def _ssd_kernel(z_ref, xs_ref, b_ref, c_ref, dt_ref, cwx_ref, cwb_ref, cwc_ref, cbx_ref, cbb_ref, cbc_ref,
                dtb_ref, alog_ref, dsk_ref, nw_ref, o_ref,
                pad_ref, xc_ref, bc_ref, cc_ref, y_ref, cb_ref, src_ref, cs_ref, st_ref):
    seq = xs_ref.shape[0]
    n_chunks = seq // CHUNK
    halo = SUBLANES
    half = CONV_SSD // 2
    gw = GROUP_W
    w_all = gw + 2 * D_STATE

    pad_ref[0:halo, :] = jnp.zeros((halo, w_all), F32)
    pad_ref[halo + seq:, :] = jnp.zeros((halo, w_all), F32)
    pad_ref[halo:halo + seq, 0:gw] = xs_ref[...].astype(F32)
    pad_ref[halo:halo + seq, gw:gw + D_STATE] = b_ref[...].astype(F32)
    pad_ref[halo:halo + seq, gw + D_STATE:] = c_ref[...].astype(F32)
    cw = jnp.concatenate([cwx_ref[...], cwb_ref[...], cwc_ref[...]], axis=1)
    cbias = jnp.concatenate([cbx_ref[...], cbb_ref[...], cbc_ref[...]], axis=1)

    def conv_chunk(c, carry):
        t0 = pl.multiple_of(c * CHUNK, CHUNK)
        win = pad_ref[pl.ds(t0, CHUNK + 2 * halo), :]
        acc = jnp.zeros((CHUNK, w_all), F32) + cbias
        for k in range(CONV_SSD):
            off = halo - half + k
            acc = acc + win[off:off + CHUNK, :] * cw[k:k + 1, :]
        acc = _silu(acc)
        xc_ref[pl.ds(t0, CHUNK), :] = acc[:, 0:gw]
        bc_ref[pl.ds(t0, CHUNK), :] = acc[:, gw:gw + D_STATE]
        cc_ref[pl.ds(t0, CHUNK), :] = acc[:, gw + D_STATE:]
        cb_ref[c] = _dot_nt(acc[:, gw + D_STATE:].astype(BF16), acc[:, gw:gw + D_STATE].astype(BF16))
        return carry

    lax.fori_loop(0, n_chunks, conv_chunk, 0, unroll=2)

    nh = 2 * HEADS_PER_GROUP
    r_i = lax.broadcasted_iota(jnp.int32, (CHUNK, CHUNK), 0)
    c_i = lax.broadcasted_iota(jnp.int32, (CHUNK, CHUNK), 1)
    upper = (r_i <= c_i).astype(F32)
    lower = (r_i >= c_i).astype(F32)
    raw = dt_ref[...].reshape(n_chunks * nh, CHUNK) + dtb_ref[...]
    dt_all = jnp.maximum(raw, 0.0) + jnp.log1p(jnp.exp(-jnp.abs(raw)))
    da = dt_all * (-jnp.exp(alog_ref[...]))
    row_all = lax.broadcasted_iota(jnp.int32, (n_chunks * nh, CHUNK), 0)
    cs_all = jnp.where(row_all % nh < HEADS_PER_GROUP, _dot_exact(da, upper), _dot_exact(da, lower))
    cs2_all = cs_all * LOG2_E
    cs_ref[...] = cs2_all.reshape(n_chunks, nh, CHUNK)
    src_ref[...] = (cs2_all - jnp.log2(dt_all)).reshape(n_chunks, nh, CHUNK)

    lane_head = lax.broadcasted_iota(jnp.int32, (1, gw), 1) // HEAD_DIM_S
    first_head = lax.broadcasted_iota(jnp.int32, (1, LANES), 1) < HEAD_DIM_S

    def chunk_step(c, reverse):
        r0 = HEADS_PER_GROUP if reverse else 0
        t0 = pl.multiple_of(c * CHUNK, CHUNK)
        x = xc_ref[pl.ds(t0, CHUNK), :]
        bm = bc_ref[pl.ds(t0, CHUNK), :]
        cm = cc_ref[pl.ds(t0, CHUNK), :].astype(BF16)
        cs2 = cs_ref[c]
        src = src_ref[c]
        cbm = cb_ref[c]
        if reverse:
            tot = cs2[:, 0:1]
            keep = r_i <= c_i
        else:
            tot = cs2[:, CHUNK - 1:CHUNK]
            keep = r_i >= c_i
        cs2_t = jnp.transpose(cs2)
        col_b = [jnp.broadcast_to(cs2_t[:, r0 + e:r0 + e + 1], (CHUNK, LANES)) for e in range(HEADS_PER_GROUP)]
        g_out = jnp.exp2(jnp.concatenate(
            [jnp.where(first_head, col_b[2 * v], col_b[2 * v + 1]) for v in range(gw // LANES)], axis=1))
        sel_row = (lax.broadcasted_iota(jnp.int32, (nh, gw), 0) == lane_head + r0).astype(F32)
        cdec = jnp.exp2(jnp.sum(tot * sel_row, axis=0, keepdims=True))
        wrow = jnp.exp2(tot - src)
        bt = jnp.transpose(bm)
        xb = x.astype(BF16)
        zero = jnp.zeros_like(xb)
        ms, bs, xs = [], [], []
        for e in range(HEADS_PER_GROUP):
            r = r0 + e
            seg = col_b[e] - src[r:r + 1, :]
            decay = jnp.exp2(jnp.where(keep, seg, -jnp.inf))
            ms.append((cbm * decay).astype(BF16))
            bs.append((bt * wrow[r:r + 1, :]).astype(BF16))
            xs.append(jnp.where(lane_head == e, xb, zero))
        xcat = jnp.concatenate(xs, axis=0)
        d = 1 if reverse else 0
        st = st_ref[d]
        y = _dot(jnp.concatenate(ms, axis=1), xcat) + _dot(cm, st.astype(BF16)) * g_out
        st_ref[d] = st * cdec + _dot(jnp.concatenate(bs, axis=1), xcat)
        y_ref[d, pl.ds(t0, CHUNK), :] = y

    st_ref[...] = jnp.zeros_like(st_ref)

    def scan(i, carry):
        chunk_step(i, False)
        chunk_step(n_chunks - 1 - i, True)
        return carry

    lax.fori_loop(0, n_chunks, scan, 0, unroll=2)

    def finish(c, carry):
        t0 = pl.multiple_of(c * CHUNK, CHUNK)
        rows = pl.ds(t0, CHUNK)
        y = y_ref[0, rows, :] + y_ref[1, rows, :] + xc_ref[rows, :] * dsk_ref[...]
        y = y * _silu(z_ref[rows, :].astype(F32))
        ms = jnp.mean(y * y, axis=1, keepdims=True)
        y = y * lax.rsqrt(ms + RMS_EPS) * nw_ref[...]
        o_ref[rows, :] = y.astype(o_ref.dtype)
        return carry

    lax.fori_loop(0, n_chunks, finish, 0, unroll=4)


def _ssd(proj3, dt_t, cw_x, cw_b, cw_c, cb_x, cb_b, cb_c, dtb, alog, dsk, nw):
    batch, seq, _ = proj3.shape
    gw, ns = GROUP_W, D_STATE
    nh = 2 * HEADS_PER_GROUP
    n_chunks = seq // CHUNK
    gspec = lambda w: pl.BlockSpec((1, w), lambda b, g: (0, g))
    return pl.pallas_call(
        _ssd_kernel,
        grid=(batch, N_GROUPS),
        in_specs=[
            pl.BlockSpec((None, seq, gw), lambda b, g: (b, 0, Z_OFF // gw + g)),
            pl.BlockSpec((None, seq, gw), lambda b, g: (b, 0, XS_OFF // gw + g)),
            pl.BlockSpec((None, seq, ns), lambda b, g: (b, 0, B_OFF // ns + g)),
            pl.BlockSpec((None, seq, ns), lambda b, g: (b, 0, C_OFF // ns + g)),
            pl.BlockSpec((None, n_chunks, nh, CHUNK), lambda b, g: (b, 0, g, 0)),
            pl.BlockSpec((CONV_SSD, gw), lambda b, g: (0, g)),
            pl.BlockSpec((CONV_SSD, ns), lambda b, g: (0, g)),
            pl.BlockSpec((CONV_SSD, ns), lambda b, g: (0, g)),
            gspec(gw), gspec(ns), gspec(ns),
            pl.BlockSpec((n_chunks * nh, 1), lambda b, g: (g, 0)),
            pl.BlockSpec((n_chunks * nh, 1), lambda b, g: (g, 0)),
            gspec(gw), gspec(gw),
        ],
        out_specs=pl.BlockSpec((None, seq, gw), lambda b, g: (b, 0, g)),
        out_shape=jax.ShapeDtypeStruct((batch, seq, D_INNER), BF16),
        scratch_shapes=[
            pltpu.VMEM((seq + 2 * SUBLANES, gw + 2 * ns), F32),
            pltpu.VMEM((seq, gw), F32),
            pltpu.VMEM((seq, ns), F32),
            pltpu.VMEM((seq, ns), F32),
            pltpu.VMEM((2, seq, gw), F32),
            pltpu.VMEM((n_chunks, CHUNK, CHUNK), F32),
            pltpu.VMEM((n_chunks, nh, CHUNK), F32),
            pltpu.VMEM((n_chunks, nh, CHUNK), F32),
            pltpu.VMEM((2, ns, gw), F32),
        ],
        compiler_params=pltpu.CompilerParams(
            dimension_semantics=("parallel", "parallel"), vmem_limit_bytes=VMEM_LIMIT),
        name="ssd",
    )(proj3, proj3, proj3, proj3, dt_t, cw_x, cw_b, cw_c, cb_x, cb_b, cb_c, dtb, alog, dsk, nw)


def _layer_norm(v, g, b):
    mu = jnp.mean(v, axis=1, keepdims=True)
    d = v - mu
    var = jnp.mean(d * d, axis=1, keepdims=True)
    return d * lax.rsqrt(var + LN_EPS) * g + b


def _merge_kernel(x_ref, at_ref, sd_ref, ga_ref, gs_ref, wpa_ref, wps_ref, wo_ref, bg_ref, g_ref, b_ref, o_ref):
    bg = bg_ref[...]
    g_a = jax.nn.sigmoid(ga_ref[...].astype(F32) + bg[:, :D_MODEL])
    g_s = jax.nn.sigmoid(gs_ref[...].astype(F32) + bg[:, D_MODEL:])
    merged = g_a * _dot(at_ref[...], wpa_ref[...]) + g_s * _dot(sd_ref[...], wps_ref[...])
    y = ALPHA * x_ref[...] + _dot(merged.astype(BF16), wo_ref[...])
    o_ref[...] = _layer_norm(y, g_ref[...], b_ref[...])


def _merge(x2, attn2, ssd2, proj2, wpa, wps, wo, bg, ln_g, ln_b, *, tm=512):
    t = x2.shape[0]
    full = lambda a: pl.BlockSpec(a.shape, lambda i: (0, 0))
    return pl.pallas_call(
        _merge_kernel,
        grid=(t // tm,),
        in_specs=[
            pl.BlockSpec((tm, D_MODEL), lambda i: (i, 0)),
            pl.BlockSpec((tm, ATTN_W), lambda i: (i, 0)),
            pl.BlockSpec((tm, D_INNER), lambda i: (i, 0)),
            pl.BlockSpec((tm, D_MODEL), lambda i: (i, GATE_OFF // D_MODEL)),
            pl.BlockSpec((tm, D_MODEL), lambda i: (i, GATE_OFF // D_MODEL + 1)),
            full(wpa), full(wps), full(wo), full(bg), full(ln_g), full(ln_b),
        ],
        out_specs=pl.BlockSpec((tm, D_MODEL), lambda i: (i, 0)),
        out_shape=jax.ShapeDtypeStruct((t, D_MODEL), F32),
        compiler_params=pltpu.CompilerParams(
            dimension_semantics=("parallel",), vmem_limit_bytes=VMEM_LIMIT),
        name="merge_ln1",
    )(x2, attn2, ssd2, proj2, proj2, wpa, wps, wo, bg, ln_g, ln_b)


def _ffn_kernel(x_ref, prev_ref, next_ref, wup_ref, cw_ref, cb_ref, wdn_ref, g_ref, b_ref, o_ref, xh_ref,
                ug0_ref, uu0_ref, ug1_ref, uu1_ref, *, tiles_per_seq, tf):
    i = pl.program_id(0)
    tm = x_ref.shape[0]
    halo = SUBLANES
    x = x_ref[...]
    first = (i % tiles_per_seq) == 0
    last = (i % tiles_per_seq) == tiles_per_seq - 1
    zero = jnp.zeros((halo, D_MODEL), F32)
    xh_ref[0:halo, :] = jnp.where(first, zero, prev_ref[...]).astype(BF16)
    xh_ref[halo:halo + tm, :] = x.astype(BF16)
    xh_ref[halo + tm:, :] = jnp.where(last, zero, next_ref[...]).astype(BF16)
    xh = xh_ref[...]

    def conv(u_ref, col):
        w = cw_ref[:, col:col + tf]
        out = cb_ref[:, col:col + tf] + u_ref[halo - 1:halo - 1 + tm, :] * w[0:1]
        out = out + u_ref[halo:halo + tm, :] * w[1:2]
        return out + u_ref[halo + 1:halo + 1 + tm, :] * w[2:3]

    bufs = ((ug0_ref, uu0_ref), (ug1_ref, uu1_ref))

    def up(j):
        ug_ref, uu_ref = bufs[j % 2]
        ug_ref[...] = _dot(xh, wup_ref[:, j * tf:(j + 1) * tf])
        uu_ref[...] = _dot(xh, wup_ref[:, D_FF + j * tf:D_FF + (j + 1) * tf])

    n_f = D_FF // tf
    acc = jnp.zeros((tm, D_MODEL), F32)
    up(0)
    for j in range(n_f):
        if j + 1 < n_f:
            up(j + 1)
        ug_ref, uu_ref = bufs[j % 2]
        act = (_silu(conv(ug_ref, j * tf)) * conv(uu_ref, D_FF + j * tf)).astype(BF16)
        acc = acc + _dot(act, wdn_ref[j * tf:(j + 1) * tf, :])
    o_ref[...] = _layer_norm(ALPHA * x + acc, g_ref[...], b_ref[...])


def _ffn(x1, wup, cw, cb, wdn, ln_g, ln_b, *, seq, tm=512, tf=256):
    t = x1.shape[0]
    hb = tm // SUBLANES
    n_hb = t // SUBLANES
    full = lambda a: pl.BlockSpec(a.shape, lambda i: (0, 0))
    kern = functools.partial(_ffn_kernel, tiles_per_seq=seq // tm, tf=tf)
    return pl.pallas_call(
        kern,
        grid=(t // tm,),
        in_specs=[
            pl.BlockSpec((tm, D_MODEL), lambda i: (i, 0)),
            pl.BlockSpec((SUBLANES, D_MODEL), lambda i: (jnp.maximum(i * hb - 1, 0), 0)),
            pl.BlockSpec((SUBLANES, D_MODEL), lambda i: (jnp.minimum((i + 1) * hb, n_hb - 1), 0)),
            full(wup), full(cw), full(cb), full(wdn), full(ln_g), full(ln_b),
        ],
        out_specs=pl.BlockSpec((tm, D_MODEL), lambda i: (i, 0)),
        out_shape=jax.ShapeDtypeStruct((t, D_MODEL), F32),
        scratch_shapes=([pltpu.VMEM((tm + 2 * SUBLANES, D_MODEL), BF16)]
                        + [pltpu.VMEM((tm + 2 * SUBLANES, tf), F32)] * 4),
        compiler_params=pltpu.CompilerParams(
            dimension_semantics=("parallel",), vmem_limit_bytes=VMEM_LIMIT),
        name="ffn_ln2",
    )(x1, x1, x1, wup, cw, cb, wdn, ln_g, ln_b)


def _rotary_tables(seq):
    half = ROT_DIM // 2
    pos = jnp.arange(seq, dtype=F32)
    inv_freq = jnp.power(ROPE_THETA, -jnp.arange(0, ROT_DIM, 2, dtype=F32) / ROT_DIM)
    ang = pos[:, None] * inv_freq[None, :]
    cos, sin = jnp.cos(ang), jnp.sin(ang)
    pad = jnp.zeros((seq, HEAD_DIM_A - ROT_DIM), F32)
    zer = jnp.zeros((seq, half), F32)
    c64 = jnp.concatenate([cos, cos, pad + 1.0], axis=1)
    sa64 = jnp.concatenate([-sin, zer, pad], axis=1)
    sb64 = jnp.concatenate([zer, sin, pad], axis=1)
    tile2 = lambda a: jnp.concatenate([a, a], axis=1)
    return jnp.stack([tile2(c64), tile2(sa64), tile2(sb64)])


def _group_rows(p, n_chunks):
    nh = 2 * HEADS_PER_GROUP
    pg = p.reshape(2, N_GROUPS, HEADS_PER_GROUP).transpose(1, 0, 2).reshape(N_GROUPS, 1, nh)
    return jnp.broadcast_to(pg, (N_GROUPS, n_chunks, nh)).reshape(N_GROUPS * n_chunks * nh, 1)


def kernel(x, w_in, b_gate, lambda_q1, lambda_k1, lambda_q2, lambda_k2, attn_subln_w, conv_ssd_w, conv_ssd_b,
           dt_bias, a_log, d_skip, ssd_norm_w, w_proj_attn, w_proj_ssd, w_out, ln1_g, ln1_b, w_up, conv_ffn_w,
           conv_ffn_b, w_down, ln2_g, ln2_b):
    batch, seq, _ = x.shape
    t = batch * seq
    l = 0
    xbc_w = D_INNER + 2 * N_GROUPS * D_STATE
    dt_off = Z_OFF + D_INNER + xbc_w
    w = w_in[l]
    w_a = w.astype(BF16)
    w_b = w[:, dt_off + 2 * N_HEADS_S:].astype(BF16)
    w_dt = w[:, dt_off:dt_off + 2 * N_HEADS_S]
    w_dt_t = (w_dt.reshape(D_MODEL, 2, N_GROUPS, HEADS_PER_GROUP).transpose(2, 1, 3, 0)
              .reshape(2 * N_HEADS_S, D_MODEL).astype(BF16))
    rot_tab = _rotary_tables(seq)

    x2 = x.reshape(t, D_MODEL)
    proj2, dt_t = _in_proj(x2, w_a, w_b, w_dt_t, batch=batch, seq=seq)
    proj3 = proj2.reshape(batch, seq, PROJ_W)

    lam_p = jnp.stack([lambda_q1[l], lambda_k1[l], lambda_q2[l], lambda_k2[l]]).astype(F32)
    attn = _attention(proj3, rot_tab, lam_p, attn_subln_w[l].reshape(1, 2 * HEAD_DIM_A))

    cw = conv_ssd_w[l]
    cb = conv_ssd_b[l].reshape(1, xbc_w)
    nb = N_GROUPS * D_STATE
    ssd = _ssd(proj3, dt_t,
               cw[:, :D_INNER], cw[:, D_INNER:D_INNER + nb], cw[:, D_INNER + nb:],
               cb[:, :D_INNER], cb[:, D_INNER:D_INNER + nb], cb[:, D_INNER + nb:],
               _group_rows(dt_bias[l], seq // CHUNK), _group_rows(a_log[l], seq // CHUNK),
               jnp.repeat(d_skip[l], HEAD_DIM_S).reshape(1, D_INNER), ssd_norm_w[l].reshape(1, D_INNER))

    x1 = _merge(x2, attn.reshape(t, ATTN_W), ssd.reshape(t, D_INNER), proj2,
                w_proj_attn[l].astype(BF16), w_proj_ssd[l].astype(BF16), w_out[l].astype(BF16),
                b_gate[l].reshape(1, 2 * D_MODEL), ln1_g[l].reshape(1, D_MODEL), ln1_b[l].reshape(1, D_MODEL))

    out = _ffn(x1, w_up[l].astype(BF16), conv_ffn_w[l], conv_ffn_b[l].reshape(1, 2 * D_FF),
               w_down[l].astype(BF16), ln2_g[l].reshape(1, D_MODEL), ln2_b[l].reshape(1, D_MODEL), seq=seq)
    return out.reshape(batch, seq, D_MODEL)
```

```python
import functools
import math

import jax
import jax.numpy as jnp
import numpy as np
from jax import lax
from jax.experimental import pallas as pl
from jax.experimental.pallas import tpu as pltpu

F32 = jnp.float32
BF16 = jnp.bfloat16

D_MODEL = 1024
N_HEADS_A = 8
HEAD_DIM_A = 64
ROT_DIM = HEAD_DIM_A // 4
ROPE_THETA = 500000.0
ATTN_W = N_HEADS_A * 2 * HEAD_DIM_A
D_INNER = 2 * D_MODEL
HEAD_DIM_S = 64
N_HEADS_S = D_INNER // HEAD_DIM_S
N_GROUPS = 8
HEADS_PER_GROUP = N_HEADS_S // N_GROUPS
GROUP_W = HEADS_PER_GROUP * HEAD_DIM_S
D_STATE = 128
CONV_SSD = 5
CHUNK = 128
D_FF = 2816
DEPTH = 1
ALPHA = (2 * DEPTH) ** 0.25
LN_EPS = 1e-5
RMS_EPS = 1e-5
LAM_INIT = 0.8 - 0.6 * math.exp(-0.3 * 0)
LOG2_E = math.log2(math.e)

LANES = 128
SUBLANES = 8
VMEM_LIMIT = 56 * 1024 * 1024

Q_OFF, K_OFF, V_OFF = 0, ATTN_W, 2 * ATTN_W
Z_OFF = 3 * ATTN_W
XS_OFF = Z_OFF + D_INNER
B_OFF = XS_OFF + D_INNER
C_OFF = B_OFF + N_GROUPS * D_STATE
GATE_OFF = C_OFF + N_GROUPS * D_STATE
PROJ_W = GATE_OFF + 2 * D_MODEL


def _dot(a, b):
    return jnp.dot(a, b, preferred_element_type=F32)


def _dot_nt(a, b):
    return lax.dot_general(a, b, (((1,), (1,)), ((), ())), preferred_element_type=F32)


def _dot_exact(a, b):
    return jnp.dot(a, b, preferred_element_type=F32, precision=lax.Precision.HIGHEST)


def _in_proj_kernel(x_ref, wa_ref, wb_ref, wdt_ref, o_ref, dt_ref, xb_ref, *, n_q, n_a):
    j = pl.program_id(1)

    @pl.when(j == 0)
    def _():
        xb = x_ref[...].astype(BF16)
        xb_ref[...] = xb
        dt_t = _dot_nt(wdt_ref[...], xb)
        for c in range(dt_ref.shape[0]):
            dt_ref[c] = dt_t[:, c * CHUNK:(c + 1) * CHUNK]

    @pl.when(j < n_q)
    def _():
        o_ref[...] = (_dot(xb_ref[...], wa_ref[...]) * (HEAD_DIM_A ** -0.5 * LOG2_E)).astype(o_ref.dtype)

    @pl.when((j >= n_q) & (j < n_a))
    def _():
        o_ref[...] = _dot(xb_ref[...], wa_ref[...]).astype(o_ref.dtype)

    @pl.when(j >= n_a)
    def _():
        o_ref[...] = _dot(xb_ref[...], wb_ref[...]).astype(o_ref.dtype)


def _in_proj(x2, w_a, w_b, w_dt_t, *, batch, seq, tm=1024, tn=1024):
    t = x2.shape[0]
    n_seq_tiles = seq // tm
    n_dt = w_dt_t.shape[0]
    n_a = GATE_OFF // tn
    kern = functools.partial(_in_proj_kernel, n_q=ATTN_W // tn, n_a=n_a)
    return pl.pallas_call(
        kern,
        grid=(t // tm, PROJ_W // tn),
        in_specs=[
            pl.BlockSpec((tm, D_MODEL), lambda i, j: (i, 0)),
            pl.BlockSpec((D_MODEL, tn), lambda i, j: (0, jnp.minimum(j, n_a - 1))),
            pl.BlockSpec((D_MODEL, tn), lambda i, j: (0, jnp.maximum(j - n_a, 0))),
            pl.BlockSpec((n_dt, D_MODEL), lambda i, j: (0, 0)),
        ],
        out_specs=[
            pl.BlockSpec((tm, tn), lambda i, j: (i, j)),
            pl.BlockSpec((None, tm // CHUNK, n_dt, CHUNK),
                         lambda i, j: (i // n_seq_tiles, i % n_seq_tiles, 0, 0)),
        ],
        out_shape=[
            jax.ShapeDtypeStruct((t, PROJ_W), BF16),
            jax.ShapeDtypeStruct((batch, seq // CHUNK, n_dt, CHUNK), F32),
        ],
        scratch_shapes=[pltpu.VMEM((tm, D_MODEL), BF16)],
        compiler_params=pltpu.CompilerParams(
            dimension_semantics=("parallel", "arbitrary"), vmem_limit_bytes=VMEM_LIMIT),
        name="in_proj",
    )(x2, w_a, w_b, w_dt_t)


def _rotary(a_ref, tab_ref, rows):
    a = a_ref[rows, :].astype(F32)
    up = pltpu.roll(a, LANES - ROT_DIM // 2, axis=1)
    dn = pltpu.roll(a, ROT_DIM // 2, axis=1)
    return (a * tab_ref[0, rows, :] + up * tab_ref[1, rows, :] + dn * tab_ref[2, rows, :]).astype(BF16)


def _attn_kernel(q_ref, k_ref, v_ref, tq_ref, tk_ref, lam_ref, sw_ref, o_ref, kr_ref, v1_ref, s1_ref, s2_ref, *, tk):
    tq = q_ref.shape[0]
    seq = k_ref.shape[0]
    n_kc = seq // tk

    @pl.when(pl.program_id(2) == 0)
    def _():
        v1_ref[:, :LANES] = v_ref[...]
        v1_ref[:, LANES:] = jnp.ones((seq, LANES), BF16)
        for c in range(n_kc):
            rows = slice(c * tk, (c + 1) * tk)
            kr_ref[rows, :] = _rotary(k_ref, tk_ref, rows)

    q = _rotary(q_ref, tq_ref, slice(None))
    lane = lax.broadcasted_iota(jnp.int32, q.shape, 1)
    zero = jnp.zeros_like(q)
    q1 = jnp.where(lane < HEAD_DIM_A, q, zero)
    q2 = jnp.where(lane >= HEAD_DIM_A, q, zero)

    m1 = jnp.full((tq, LANES), -jnp.inf, F32)
    m2 = m1
    for c in range(n_kc):
        kc = kr_ref[c * tk:(c + 1) * tk, :]
        sc1 = _dot_nt(q1, kc)
        sc2 = _dot_nt(q2, kc)
        s1_ref[c] = sc1
        s2_ref[c] = sc2
        for u in range(tk // LANES):
            m1 = jnp.maximum(m1, sc1[:, u * LANES:(u + 1) * LANES])
            m2 = jnp.maximum(m2, sc2[:, u * LANES:(u + 1) * LANES])
    m1 = jnp.max(m1, axis=1, keepdims=True)
    m2 = jnp.max(m2, axis=1, keepdims=True)

    a1 = jnp.zeros((tq, 2 * LANES), F32)
    a2 = a1
    for c in range(n_kc):
        vc = v1_ref[c * tk:(c + 1) * tk, :]
        p1 = jnp.exp2(s1_ref[c] - m1).astype(BF16)
        p2 = jnp.exp2(s2_ref[c] - m2).astype(BF16)
        a1 = a1 + _dot(p1, vc)
        a2 = a2 + _dot(p2, vc)

    lp = lam_ref[...]
    lam = (jnp.exp(jnp.sum(lp[0:1] * lp[1:2], axis=1, keepdims=True))
           - jnp.exp(jnp.sum(lp[2:3] * lp[3:4], axis=1, keepdims=True)) + LAM_INIT)
    o =a1[:, :LANES] / a1[:, LANES:] - lam * (a2[:, :LANES] / a2[:, LANES:])
    ms = jnp.mean(o * o, axis=1, keepdims=True)
    o = o * lax.rsqrt(ms + RMS_EPS) * sw_ref[...] * (1.0 - LAM_INIT)
    o_ref[...] = o.astype(o_ref.dtype)


def _attention(proj3, rot_tab, lam_p, subln_w, *, tq=512, tk=512):
    batch, seq, _ = proj3.shape
    hw = 2 * HEAD_DIM_A
    kern = functools.partial(_attn_kernel, tk=tk)
    return pl.pallas_call(
        kern,
        grid=(batch, N_HEADS_A, seq // tq),
        in_specs=[
            pl.BlockSpec((None, tq, hw), lambda b, h, i: (b, i, Q_OFF // hw + h)),
            pl.BlockSpec((None, seq, hw), lambda b, h, i: (b, 0, K_OFF // hw + h)),
            pl.BlockSpec((None, seq, hw), lambda b, h, i: (b, 0, V_OFF // hw + h)),
            pl.BlockSpec((3, tq, hw), lambda b, h, i: (0, i, 0)),
            pl.BlockSpec((3, seq, hw), lambda b, h, i: (0, 0, 0)),
            pl.BlockSpec((4, HEAD_DIM_A), lambda b, h, i: (0, 0)),
            pl.BlockSpec((1, hw), lambda b, h, i: (0, 0)),
        ],
        out_specs=pl.BlockSpec((None, tq, hw), lambda b, h, i: (b, i, h)),
        out_shape=jax.ShapeDtypeStruct((batch, seq, ATTN_W), BF16),
        scratch_shapes=[
            pltpu.VMEM((seq, hw), BF16),
            pltpu.VMEM((seq, 2 * LANES), BF16),
            pltpu.VMEM((seq // tk, tq, tk), F32),
            pltpu.VMEM((seq // tk, tq, tk), F32),
        ],
        compiler_params=pltpu.CompilerParams(
            dimension_semantics=("parallel", "parallel", "arbitrary"), vmem_limit_bytes=VMEM_LIMIT),
        name="diff_attn",
    )(proj3, proj3, proj3, rot_tab, rot_tab, lam_p, subln_w)


def _silu(v):
    return v * jax.nn.sigmoid(v)


def _ssd_kernel(z_ref, xs_ref, b_ref, c_ref, dt_ref, cwx_ref, cwb_ref, cwc_ref, cbx_ref, cbb_ref, cbc_ref,
                dtb_ref, alog_ref, dsk_ref, nw_ref, o_ref,
                pad_ref, xc_ref, bc_ref, cc_ref, y_ref, cb_ref, src_ref, cs_ref, st_ref):
    seq = xs_ref.shape[0]
    n_chunks = seq // CHUNK
    halo = SUBLANES
    half = CONV_SSD // 2
    gw = GROUP_W
    w_all = gw + 2 * D_STATE

    pad_ref[0:halo, :] = jnp.zeros((halo, w_all), F32)
    pad_ref[halo + seq:, :] = jnp.zeros((halo, w_all), F32)
    pad_ref[halo:halo + seq, 0:gw] = xs_ref[...].astype(F32)
    pad_ref[halo:halo + seq, gw:gw + D_STATE] = b_ref[...].astype(F32)
    pad_ref[halo:halo + seq, gw + D_STATE:] = c_ref[...].astype(F32)
    cw = jnp.concatenate([cwx_ref[...], cwb_ref[...], cwc_ref[...]], axis=1)
    cbias = jnp.concatenate([cbx_ref[...], cbb_ref[...], cbc_ref[...]], axis=1)

    def conv_chunk(c, carry):
        t0 = pl.multiple_of(c * CHUNK, CHUNK)
        win = pad_ref[pl.ds(t0, CHUNK + 2 * halo), :]
        acc = jnp.zeros((CHUNK, w_all), F32) + cbias
        for k in range(CONV_SSD):
            off = halo - half + k
            acc = acc + win[off:off + CHUNK, :] * cw[k:k + 1, :]
        acc = _silu(acc)
        xc_ref[pl.ds(t0, CHUNK), :] = acc[:, 0:gw]
        bc_ref[pl.ds(t0, CHUNK), :] = acc[:, gw:gw + D_STATE]
        cc_ref[pl.ds(t0, CHUNK), :] = acc[:, gw + D_STATE:]
        cb_ref[c] = _dot_nt(acc[:, gw + D_STATE:].astype(BF16), acc[:, gw:gw + D_STATE].astype(BF16))
        return carry

    lax.fori_loop(0, n_chunks, conv_chunk, 0, unroll=2)

    nh = 2 * HEADS_PER_GROUP
    r_i = lax.broadcasted_iota(jnp.int32, (CHUNK, CHUNK), 0)
    c_i = lax.broadcasted_iota(jnp.int32, (CHUNK, CHUNK), 1)
    upper = (r_i <= c_i).astype(F32)
    lower = (r_i >= c_i).astype(F32)
    raw = dt_ref[...].reshape(n_chunks * nh, CHUNK) + dtb_ref[...]
    dt_all = jnp.maximum(raw, 0.0) + jnp.log1p(jnp.exp(-jnp.abs(raw)))
    da = dt_all * (-jnp.exp(alog_ref[...]))
    row_all = lax.broadcasted_iota(jnp.int32, (n_chunks * nh, CHUNK), 0)
    cs_all = jnp.where(row_all % nh < HEADS_PER_GROUP, _dot_exact(da, upper), _dot_exact(da, lower))
    cs2_all = cs_all * LOG2_E
    cs_ref[...] = cs2_all.reshape(n_chunks, nh, CHUNK)
    src_ref[...] = (cs2_all - jnp.log2(dt_all)).reshape(n_chunks, nh, CHUNK)

    lane_head = lax.broadcasted_iota(jnp.int32, (1, gw), 1) // HEAD_DIM_S
    first_head = lax.broadcasted_iota(jnp.int32, (1, LANES), 1) < HEAD_DIM_S

    def chunk_step(c, reverse):
        r0 = HEADS_PER_GROUP if reverse else 0
        t0 = pl.multiple_of(c * CHUNK, CHUNK)
        x = xc_ref[pl.ds(t0, CHUNK), :]
        bm = bc_ref[pl.ds(t0, CHUNK), :]
        cm = cc_ref[pl.ds(t0, CHUNK), :].astype(BF16)
        cs2 = cs_ref[c]
        src = src_ref[c]
        cbm = cb_ref[c]
        if reverse:
            tot = cs2[:, 0:1]
            keep = r_i <= c_i
        else:
            tot = cs2[:, CHUNK - 1:CHUNK]
            keep = r_i >= c_i
        cs2_t = jnp.transpose(cs2)
        col_b = [jnp.broadcast_to(cs2_t[:, r0 + e:r0 + e + 1], (CHUNK, LANES)) for e in range(HEADS_PER_GROUP)]
        g_out = jnp.exp2(jnp.concatenate(
            [jnp.where(first_head, col_b[2 * v], col_b[2 * v + 1]) for v in range(gw // LANES)], axis=1))
        sel_row = (lax.broadcasted_iota(jnp.int32, (nh, gw), 0) == lane_head + r0).astype(F32)
        cdec = jnp.exp2(jnp.sum(tot * sel_row, axis=0, keepdims=True))
        wrow = jnp.exp2(tot - src)
        bt = jnp.transpose(bm)
        xb = x.astype(BF16)
        zero = jnp.zeros_like(xb)
        ms, bs, xs = [], [], []
        for e in range(HEADS_PER_GROUP):
            r = r0 + e
            seg = col_b[e] - src[r:r + 1, :]
            decay = jnp.exp2(jnp.where(keep, seg, -jnp.inf))
            ms.append((cbm * decay).astype(BF16))
            bs.append((bt * wrow[r:r + 1, :]).astype(BF16))
            xs.append(jnp.where(lane_head == e, xb, zero))
        xcat = jnp.concatenate(xs, axis=0)
        d = 1 if reverse else 0
        st = st_ref[d]
        y = _dot(jnp.concatenate(ms, axis=1), xcat) + _dot(cm, st.astype(BF16)) * g_out
        st_ref[d] = st * cdec + _dot(jnp.concatenate(bs, axis=1), xcat)
        y_ref[d, pl.ds(t0, CHUNK), :] = y

    st_ref[...] = jnp.zeros_like(st_ref)

    def scan(i, carry):
        chunk_step(i, False)
        chunk_step(n_chunks - 1 - i, True)
        return carry

    lax.fori_loop(0, n_chunks, scan, 0, unroll=4)

    def finish(c, carry):
        t0 = pl.multiple_of(c * CHUNK, CHUNK)
        rows = pl.ds(t0, CHUNK)
        y = y_ref[0, rows, :] + y_ref[1, rows, :] + xc_ref[rows, :] * dsk_ref[...]
        y = y * _silu(z_ref[rows, :].astype(F32))
        ms = jnp.mean(y * y, axis=1, keepdims=True)
        y = y * lax.rsqrt(ms + RMS_EPS) * nw_ref[...]
        o_ref[rows, :] = y.astype(o_ref.dtype)
        return carry

    lax.fori_loop(0, n_chunks, finish, 0, unroll=4)


def _ssd(proj3, dt_t, cw_x, cw_b, cw_c, cb_x, cb_b, cb_c, dtb, alog, dsk, nw):
    batch, seq, _ = proj3.shape
    gw, ns = GROUP_W, D_STATE
    nh = 2 * HEADS_PER_GROUP
    n_chunks = seq // CHUNK
    gspec = lambda w: pl.BlockSpec((1, w), lambda b, g: (0, g))
    return pl.pallas_call(
        _ssd_kernel,
        grid=(batch, N_GROUPS),
        in_specs=[
            pl.BlockSpec((None, seq, gw), lambda b, g: (b, 0, Z_OFF // gw + g)),
            pl.BlockSpec((None, seq, gw), lambda b, g: (b, 0, XS_OFF // gw + g)),
            pl.BlockSpec((None, seq, ns), lambda b, g: (b, 0, B_OFF // ns + g)),
            pl.BlockSpec((None, seq, ns), lambda b, g: (b, 0, C_OFF // ns + g)),
            pl.BlockSpec((None, n_chunks, nh, CHUNK), lambda b, g: (b, 0, g, 0)),
            pl.BlockSpec((CONV_SSD, gw), lambda b, g: (0, g)),
            pl.BlockSpec((CONV_SSD, ns), lambda b, g: (0, g)),
            pl.BlockSpec((CONV_SSD, ns), lambda b, g: (0, g)),
            gspec(gw), gspec(ns), gspec(ns),
            pl.BlockSpec((n_chunks * nh, 1), lambda b, g: (g, 0)),
            pl.BlockSpec((n_chunks * nh, 1), lambda b, g: (g, 0)),
            gspec(gw), gspec(gw),
        ],
        out_specs=pl.BlockSpec((None, seq, gw), lambda b, g: (b, 0, g)),
        out_shape=jax.ShapeDtypeStruct((batch, seq, D_INNER), BF16),
        scratch_shapes=[
            pltpu.VMEM((seq + 2 * SUBLANES, gw + 2 * ns), F32),
            pltpu.VMEM((seq, gw), F32),
            pltpu.VMEM((seq, ns), F32),
            pltpu.VMEM((seq, ns), F32),
            pltpu.VMEM((2, seq, gw), F32),
            pltpu.VMEM((n_chunks, CHUNK, CHUNK), F32),
            pltpu.VMEM((n_chunks, nh, CHUNK), F32),
            pltpu.VMEM((n_chunks, nh, CHUNK), F32),
            pltpu.VMEM((2, ns, gw), F32),
        ],
        compiler_params=pltpu.CompilerParams(
            dimension_semantics=("parallel", "parallel"), vmem_limit_bytes=VMEM_LIMIT),
        name="ssd",
    )(proj3, proj3, proj3, proj3, dt_t, cw_x, cw_b, cw_c, cb_x, cb_b, cb_c, dtb, alog, dsk, nw)


def _layer_norm(v, g, b):
    mu = jnp.mean(v, axis=1, keepdims=True)
    d = v - mu
    var = jnp.mean(d * d, axis=1, keepdims=True)
    return d * lax.rsqrt(var + LN_EPS) * g + b


def _merge_kernel(x_ref, at_ref, sd_ref, ga_ref, gs_ref, wpa_ref, wps_ref, wo_ref, bg_ref, g_ref, b_ref, o_ref):
    bg = bg_ref[...]
    g_a = jax.nn.sigmoid(ga_ref[...].astype(F32) + bg[:, :D_MODEL])
    g_s = jax.nn.sigmoid(gs_ref[...].astype(F32) + bg[:, D_MODEL:])
    merged = g_a * _dot(at_ref[...], wpa_ref[...]) + g_s * _dot(sd_ref[...], wps_ref[...])
    y = ALPHA * x_ref[...] + _dot(merged.astype(BF16), wo_ref[...])
    o_ref[...] = _layer_norm(y, g_ref[...], b_ref[...])


def _merge(x2, attn2, ssd2, proj2, wpa, wps, wo, bg, ln_g, ln_b, *, tm=512):
    t = x2.shape[0]
    full = lambda a: pl.BlockSpec(a.shape, lambda i: (0, 0))
    return pl.pallas_call(
        _merge_kernel,
        grid=(t // tm,),
        in_specs=[
            pl.BlockSpec((tm, D_MODEL), lambda i: (i, 0)),
            pl.BlockSpec((tm, ATTN_W), lambda i: (i, 0)),
            pl.BlockSpec((tm, D_INNER), lambda i: (i, 0)),
            pl.BlockSpec((tm, D_MODEL), lambda i: (i, GATE_OFF // D_MODEL)),
            pl.BlockSpec((tm, D_MODEL), lambda i: (i, GATE_OFF // D_MODEL + 1)),
            full(wpa), full(wps), full(wo), full(bg), full(ln_g), full(ln_b),
        ],
        out_specs=pl.BlockSpec((tm, D_MODEL), lambda i: (i, 0)),
        out_shape=jax.ShapeDtypeStruct((t, D_MODEL), F32),
        compiler_params=pltpu.CompilerParams(
            dimension_semantics=("parallel",), vmem_limit_bytes=VMEM_LIMIT),
        name="merge_ln1",
    )(x2, attn2, ssd2, proj2, proj2, wpa, wps, wo, bg, ln_g, ln_b)


def _ffn_kernel(x_ref, prev_ref, next_ref, wup_ref, cw_ref, cb_ref, wdn_ref, g_ref, b_ref, o_ref, xh_ref,
                ug0_ref, uu0_ref, ug1_ref, uu1_ref, *, tiles_per_seq, tf):
    i = pl.program_id(0)
    tm = x_ref.shape[0]
    halo = SUBLANES
    x = x_ref[...]
    first = (i % tiles_per_seq) == 0
    last = (i % tiles_per_seq) == tiles_per_seq - 1
    zero = jnp.zeros((halo, D_MODEL), F32)
    xh_ref[0:halo, :] = jnp.where(first, zero, prev_ref[...]).astype(BF16)
    xh_ref[halo:halo + tm, :] = x.astype(BF16)
    xh_ref[halo + tm:, :] = jnp.where(last, zero, next_ref[...]).astype(BF16)
    xh = xh_ref[...]

    def conv(u_ref, col):
        w = cw_ref[:, col:col + tf]
        out = cb_ref[:, col:col + tf] + u_ref[halo - 1:halo - 1 + tm, :] * w[0:1]
        out = out + u_ref[halo:halo + tm, :] * w[1:2]
        return out + u_ref[halo + 1:halo + 1 + tm, :] * w[2:3]

    bufs = ((ug0_ref, uu0_ref), (ug1_ref, uu1_ref))

    def up(j):
        ug_ref, uu_ref = bufs[j % 2]
        ug_ref[...] = _dot(xh, wup_ref[:, j * tf:(j + 1) * tf])
        uu_ref[...] = _dot(xh, wup_ref[:, D_FF + j * tf:D_FF + (j + 1) * tf])

    n_f = D_FF // tf
    acc = jnp.zeros((tm, D_MODEL), F32)
    up(0)
    for j in range(n_f):
        if j + 1 < n_f:
            up(j + 1)
        ug_ref, uu_ref = bufs[j % 2]
        act = (_silu(conv(ug_ref, j * tf)) * conv(uu_ref, D_FF + j * tf)).astype(BF16)
        acc = acc + _dot(act, wdn_ref[j * tf:(j + 1) * tf, :])
    o_ref[...] = _layer_norm(ALPHA * x + acc, g_ref[...], b_ref[...])


def _ffn(x1, wup, cw, cb, wdn, ln_g, ln_b, *, seq, tm=512, tf=256):
    t = x1.shape[0]
    hb = tm // SUBLANES
    n_hb = t // SUBLANES
    full = lambda a: pl.BlockSpec(a.shape, lambda i: (0, 0))
    kern = functools.partial(_ffn_kernel, tiles_per_seq=seq // tm, tf=tf)
    return pl.pallas_call(
        kern,
        grid=(t // tm,),
        in_specs=[
            pl.BlockSpec((tm, D_MODEL), lambda i: (i, 0)),
            pl.BlockSpec((SUBLANES, D_MODEL), lambda i: (jnp.maximum(i * hb - 1, 0), 0)),
            pl.BlockSpec((SUBLANES, D_MODEL), lambda i: (jnp.minimum((i + 1) * hb, n_hb - 1), 0)),
            full(wup), full(cw), full(cb), full(wdn), full(ln_g), full(ln_b),
        ],
        out_specs=pl.BlockSpec((tm, D_MODEL), lambda i: (i, 0)),
        out_shape=jax.ShapeDtypeStruct((t, D_MODEL), F32),
        scratch_shapes=([pltpu.VMEM((tm + 2 * SUBLANES, D_MODEL), BF16)]
                        + [pltpu.VMEM((tm + 2 * SUBLANES, tf), F32)] * 4),
        compiler_params=pltpu.CompilerParams(
            dimension_semantics=("parallel",), vmem_limit_bytes=VMEM_LIMIT),
        name="ffn_ln2",
    )(x1, x1, x1, wup, cw, cb, wdn, ln_g, ln_b)


def _rotary_tables(seq):
    half = ROT_DIM // 2
    pos = np.arange(seq, dtype=np.float32)
    inv_freq = np.power(np.float32(ROPE_THETA), -np.arange(0, ROT_DIM, 2, dtype=np.float32) / np.float32(ROT_DIM))
    ang = (pos[:, None] * inv_freq[None, :].astype(np.float32)).astype(np.float32)
    cos, sin = np.cos(ang).astype(np.float32), np.sin(ang).astype(np.float32)
    pad = np.zeros((seq, HEAD_DIM_A - ROT_DIM), np.float32)
    zer = np.zeros((seq, half), np.float32)
    c64 = np.concatenate([cos, cos, pad + 1.0], axis=1)
    sa64 = np.concatenate([-sin, zer, pad], axis=1)
    sb64 = np.concatenate([zer, sin, pad], axis=1)
    tile2 = lambda a: np.concatenate([a, a], axis=1)
    return jnp.asarray(np.stack([tile2(c64), tile2(sa64), tile2(sb64)]))


def _group_rows(p, n_chunks):
    nh = 2 * HEADS_PER_GROUP
    pg = p.reshape(2, N_GROUPS, HEADS_PER_GROUP).transpose(1, 0, 2).reshape(N_GROUPS, 1, nh)
    return jnp.broadcast_to(pg, (N_GROUPS, n_chunks, nh)).reshape(N_GROUPS * n_chunks * nh, 1)


def kernel(x, w_in, b_gate, lambda_q1, lambda_k1, lambda_q2, lambda_k2, attn_subln_w, conv_ssd_w, conv_ssd_b,
           dt_bias, a_log, d_skip, ssd_norm_w, w_proj_attn, w_proj_ssd, w_out, ln1_g, ln1_b, w_up, conv_ffn_w,
           conv_ffn_b, w_down, ln2_g, ln2_b):
    batch, seq, _ = x.shape
    t = batch * seq
    l = 0
    xbc_w = D_INNER + 2 * N_GROUPS * D_STATE
    dt_off = Z_OFF + D_INNER + xbc_w
    w = w_in[l]
    w_a = w.astype(BF16)
    w_b = w[:, dt_off + 2 * N_HEADS_S:].astype(BF16)
    w_dt = w[:, dt_off:dt_off + 2 * N_HEADS_S]
    w_dt_t = (w_dt.reshape(D_MODEL, 2, N_GROUPS, HEADS_PER_GROUP).transpose(2, 1, 3, 0)
              .reshape(2 * N_HEADS_S, D_MODEL).astype(BF16))
    rot_tab = _rotary_tables(seq)

    x2 = x.reshape(t, D_MODEL)
    proj2, dt_t = _in_proj(x2, w_a, w_b, w_dt_t, batch=batch, seq=seq)
    proj3 = proj2.reshape(batch, seq, PROJ_W)

    lam_p = jnp.stack([lambda_q1[l], lambda_k1[l], lambda_q2[l], lambda_k2[l]]).astype(F32)
    attn = _attention(proj3, rot_tab, lam_p, attn_subln_w[l].reshape(1, 2 * HEAD_DIM_A))

    cw = conv_ssd_w[l]
    cb = conv_ssd_b[l].reshape(1, xbc_w)
    nb = N_GROUPS * D_STATE
    ssd = _ssd(proj3, dt_t,
               cw[:, :D_INNER], cw[:, D_INNER:D_INNER + nb], cw[:, D_INNER + nb:],
               cb[:, :D_INNER], cb[:, D_INNER:D_INNER + nb], cb[:, D_INNER + nb:],
               _group_rows(dt_bias[l], seq // CHUNK), _group_rows(a_log[l], seq // CHUNK),
               jnp.repeat(d_skip[l], HEAD_DIM_S).reshape(1, D_INNER), ssd_norm_w[l].reshape(1, D_INNER))

    x1 = _merge(x2, attn.reshape(t, ATTN_W), ssd.reshape(t, D_INNER), proj2,
                w_proj_attn[l].astype(BF16), w_proj_ssd[l].astype(BF16), w_out[l].astype(BF16),
                b_gate[l].reshape(1, 2 * D_MODEL), ln1_g[l].reshape(1, D_MODEL), ln1_b[l].reshape(1, D_MODEL))

    out = _ffn(x1, w_up[l].astype(BF16), conv_ffn_w[l], conv_ffn_b[l].reshape(1, 2 * D_FF),
               w_down[l].astype(BF16), ln2_g[l].reshape(1, D_MODEL), ln2_b[l].reshape(1, D_MODEL), seq=seq)
    return out.reshape(batch, seq, D_MODEL)
```

```python
import functools
import math

import jax
import jax.numpy as jnp
import numpy as np
from jax import lax
from jax.experimental import pallas as pl
from jax.experimental.pallas import tpu as pltpu

F32 = jnp.float32
BF16 = jnp.bfloat16

D_MODEL = 1024
N_HEADS_A = 8
HEAD_DIM_A = 64
ROT_DIM = HEAD_DIM_A // 4
ROPE_THETA = 500000.0
ATTN_W = N_HEADS_A * 2 * HEAD_DIM_A
D_INNER = 2 * D_MODEL
HEAD_DIM_S = 64
N_HEADS_S = D_INNER // HEAD_DIM_S
N_GROUPS = 8
HEADS_PER_GROUP = N_HEADS_S // N_GROUPS
GROUP_W = HEADS_PER_GROUP * HEAD_DIM_S
D_STATE = 128
CONV_SSD = 5
CHUNK = 128
D_FF = 2816
DEPTH = 1
ALPHA = (2 * DEPTH) ** 0.25
LN_EPS = 1e-5
RMS_EPS = 1e-5
LAM_INIT = 0.8 - 0.6 * math.exp(-0.3 * 0)
LOG2_E = math.log2(math.e)

LANES = 128
SUBLANES = 8
VMEM_LIMIT = 56 * 1024 * 1024

Q_OFF, K_OFF, V_OFF = 0, ATTN_W, 2 * ATTN_W
Z_OFF = 3 * ATTN_W
XS_OFF = Z_OFF + D_INNER
B_OFF = XS_OFF + D_INNER
C_OFF = B_OFF + N_GROUPS * D_STATE
GATE_OFF = C_OFF + N_GROUPS * D_STATE
PROJ_W = GATE_OFF + 2 * D_MODEL


def _dot(a, b):
    return jnp.dot(a, b, preferred_element_type=F32)


def _dot_nt(a, b):
    return lax.dot_general(a, b, (((1,), (1,)), ((), ())), preferred_element_type=F32)


def _dot_exact(a, b):
    return jnp.dot(a, b, preferred_element_type=F32, precision=lax.Precision.HIGHEST)


def _in_proj_kernel(x_ref, wa_ref, wb_ref, wdt_ref, o_ref, dt_ref, xb_ref, *, n_q, n_a):
    j = pl.program_id(1)

    @pl.when(j == 0)
    def _():
        xb = x_ref[...].astype(BF16)
        xb_ref[...] = xb
        dt_t = _dot_nt(wdt_ref[...], xb)
        for c in range(dt_ref.shape[0]):
            dt_ref[c] = dt_t[:, c * CHUNK:(c + 1) * CHUNK]

    @pl.when(j < n_q)
    def _():
        o_ref[...] = (_dot(xb_ref[...], wa_ref[...]) * (HEAD_DIM_A ** -0.5 * LOG2_E)).astype(o_ref.dtype)

    @pl.when((j >= n_q) & (j < n_a))
    def _():
        o_ref[...] = _dot(xb_ref[...], wa_ref[...]).astype(o_ref.dtype)

    @pl.when(j >= n_a)
    def _():
        o_ref[...] = _dot(xb_ref[...], wb_ref[...]).astype(o_ref.dtype)


def _in_proj(x2, w_a, w_b, w_dt_t, *, batch, seq, tm=2048, tn=1024):
    t = x2.shape[0]
    n_seq_tiles = seq // tm
    n_dt = w_dt_t.shape[0]
    n_a = GATE_OFF // tn
    kern = functools.partial(_in_proj_kernel, n_q=ATTN_W // tn, n_a=n_a)
    return pl.pallas_call(
        kern,
        grid=(t // tm, PROJ_W // tn),
        in_specs=[
            pl.BlockSpec((tm, D_MODEL), lambda i, j: (i, 0)),
            pl.BlockSpec((D_MODEL, tn), lambda i, j: (0, jnp.minimum(j, n_a - 1))),
            pl.BlockSpec((D_MODEL, tn), lambda i, j: (0, jnp.maximum(j - n_a, 0))),
            pl.BlockSpec((n_dt, D_MODEL), lambda i, j: (0, 0)),
        ],
        out_specs=[
            pl.BlockSpec((tm, tn), lambda i, j: (i, j)),
            pl.BlockSpec((None, tm // CHUNK, n_dt, CHUNK),
                         lambda i, j: (i // n_seq_tiles, i % n_seq_tiles, 0, 0)),
        ],
        out_shape=[
            jax.ShapeDtypeStruct((t, PROJ_W), BF16),
            jax.ShapeDtypeStruct((batch, seq // CHUNK, n_dt, CHUNK), F32),
        ],
        scratch_shapes=[pltpu.VMEM((tm, D_MODEL), BF16)],
        compiler_params=pltpu.CompilerParams(
            dimension_semantics=("parallel", "arbitrary"), vmem_limit_bytes=VMEM_LIMIT),
        name="in_proj",
    )(x2, w_a, w_b, w_dt_t)


def _rotary(a_ref, tab_ref, rows):
    a = a_ref[rows, :].astype(F32)
    up = pltpu.roll(a, LANES - ROT_DIM // 2, axis=1)
    dn = pltpu.roll(a, ROT_DIM // 2, axis=1)
    return (a * tab_ref[0, rows, :] + up * tab_ref[1, rows, :] + dn * tab_ref[2, rows, :]).astype(BF16)


def _attn_kernel(q_ref, k_ref, v_ref, tq_ref, tk_ref, lam_ref, sw_ref, o_ref, kr_ref, v1_ref, s1_ref, s2_ref, *, tk):
    tq = q_ref.shape[0]
    seq = k_ref.shape[0]
    n_kc = seq // tk

    @pl.when(pl.program_id(2) == 0)
    def _():
        v1_ref[:, :LANES] = v_ref[...]
        v1_ref[:, LANES:] = jnp.ones((seq, LANES), BF16)
        for c in range(n_kc):
            rows = slice(c * tk, (c + 1) * tk)
            kr_ref[rows, :] = _rotary(k_ref, tk_ref, rows)

    q = _rotary(q_ref, tq_ref, slice(None))
    lane = lax.broadcasted_iota(jnp.int32, q.shape, 1)
    zero = jnp.zeros_like(q)
    q1 = jnp.where(lane < HEAD_DIM_A, q, zero)
    q2 = jnp.where(lane >= HEAD_DIM_A, q, zero)

    m1 = jnp.full((tq, LANES), -jnp.inf, F32)
    m2 = m1
    for c in range(n_kc):
        kc = kr_ref[c * tk:(c + 1) * tk, :]
        sc1 = _dot_nt(q1, kc)
        sc2 = _dot_nt(q2, kc)
        s1_ref[c] = sc1
        s2_ref[c] = sc2
        for u in range(tk // LANES):
            m1 = jnp.maximum(m1, sc1[:, u * LANES:(u + 1) * LANES])
            m2 = jnp.maximum(m2, sc2[:, u * LANES:(u + 1) * LANES])
    m1 = jnp.max(m1, axis=1, keepdims=True)
    m2 = jnp.max(m2, axis=1, keepdims=True)

    a1 = jnp.zeros((tq, 2 * LANES), F32)
    a2 = a1
    for c in range(n_kc):
        vc = v1_ref[c * tk:(c + 1) * tk, :]
        p1 = jnp.exp2(s1_ref[c] - m1).astype(BF16)
        p2 = jnp.exp2(s2_ref[c] - m2).astype(BF16)
        a1 = a1 + _dot(p1, vc)
        a2 = a2 + _dot(p2, vc)

    lp = lam_ref[...]
    lam = (jnp.exp(jnp.sum(lp[0:1] * lp[1:2], axis=1, keepdims=True))
           - jnp.exp(jnp.sum(lp[2:3] * lp[3:4], axis=1, keepdims=True)) + LAM_INIT)
    o =a1[:, :LANES] / a1[:, LANES:] - lam * (a2[:, :LANES] / a2[:, LANES:])
    ms = jnp.mean(o * o, axis=1, keepdims=True)
    o = o * lax.rsqrt(ms + RMS_EPS) * sw_ref[...] * (1.0 - LAM_INIT)
    o_ref[...] = o.astype(o_ref.dtype)


def _attention(proj3, rot_tab, lam_p, subln_w, *, tq=512, tk=512):
    batch, seq, _ = proj3.shape
    hw = 2 * HEAD_DIM_A
    kern = functools.partial(_attn_kernel, tk=tk)
    return pl.pallas_call(
        kern,
        grid=(batch, N_HEADS_A, seq // tq),
        in_specs=[
            pl.BlockSpec((None, tq, hw), lambda b, h, i: (b, i, Q_OFF // hw + h)),
            pl.BlockSpec((None, seq, hw), lambda b, h, i: (b, 0, K_OFF // hw + h)),
            pl.BlockSpec((None, seq, hw), lambda b, h, i: (b, 0, V_OFF // hw + h)),
            pl.BlockSpec((3, tq, hw), lambda b, h, i: (0, i, 0)),
            pl.BlockSpec((3, seq, hw), lambda b, h, i: (0, 0, 0)),
            pl.BlockSpec((4, HEAD_DIM_A), lambda b, h, i: (0, 0)),
            pl.BlockSpec((1, hw), lambda b, h, i: (0, 0)),
        ],
        out_specs=pl.BlockSpec((None, tq, hw), lambda b, h, i: (b, i, h)),
        out_shape=jax.ShapeDtypeStruct((batch, seq, ATTN_W), BF16),
        scratch_shapes=[
            pltpu.VMEM((seq, hw), BF16),
            pltpu.VMEM((seq, 2 * LANES), BF16),
            pltpu.VMEM((seq // tk, tq, tk), F32),
            pltpu.VMEM((seq // tk, tq, tk), F32),
        ],
        compiler_params=pltpu.CompilerParams(
            dimension_semantics=("parallel", "parallel", "arbitrary"), vmem_limit_bytes=VMEM_LIMIT),
        name="diff_attn",
    )(proj3, proj3, proj3, rot_tab, rot_tab, lam_p, subln_w)


def _silu(v):
    return v * jax.nn.sigmoid(v)


def _ssd_kernel(z_ref, xs_ref, b_ref, c_ref, dt_ref, cwx_ref, cwb_ref, cwc_ref, cbx_ref, cbb_ref, cbc_ref,
                dtb_ref, alog_ref, dsk_ref, nw_ref, o_ref,
                pad_ref, xc_ref, bc_ref, cc_ref, y_ref, cb_ref, src_ref, cs_ref, st_ref):
    seq = xs_ref.shape[0]
    n_chunks = seq // CHUNK
    halo = SUBLANES
    half = CONV_SSD // 2
    gw = GROUP_W
    w_all = gw + 2 * D_STATE

    pad_ref[0:halo, :] = jnp.zeros((halo, w_all), F32)
    pad_ref[halo + seq:, :] = jnp.zeros((halo, w_all), F32)
    pad_ref[halo:halo + seq, 0:gw] = xs_ref[...].astype(F32)
    pad_ref[halo:halo + seq, gw:gw + D_STATE] = b_ref[...].astype(F32)
    pad_ref[halo:halo + seq, gw + D_STATE:] = c_ref[...].astype(F32)
    cw = jnp.concatenate([cwx_ref[...], cwb_ref[...], cwc_ref[...]], axis=1)
    cbias = jnp.concatenate([cbx_ref[...], cbb_ref[...], cbc_ref[...]], axis=1)

    def conv_chunk(c, carry):
        t0 = pl.multiple_of(c * CHUNK, CHUNK)
        win = pad_ref[pl.ds(t0, CHUNK + 2 * halo), :]
        acc = jnp.zeros((CHUNK, w_all), F32) + cbias
        for k in range(CONV_SSD):
            off = halo - half + k
            acc = acc + win[off:off + CHUNK, :] * cw[k:k + 1, :]
        acc = _silu(acc)
        xc_ref[pl.ds(t0, CHUNK), :] = acc[:, 0:gw]
        bc_ref[pl.ds(t0, CHUNK), :] = acc[:, gw:gw + D_STATE]
        cc_ref[pl.ds(t0, CHUNK), :] = acc[:, gw + D_STATE:]
        cb_ref[c] = _dot_nt(acc[:, gw + D_STATE:].astype(BF16), acc[:, gw:gw + D_STATE].astype(BF16))
        return carry

    lax.fori_loop(0, n_chunks, conv_chunk, 0, unroll=4)

    nh = 2 * HEADS_PER_GROUP
    r_i = lax.broadcasted_iota(jnp.int32, (CHUNK, CHUNK), 0)
    c_i = lax.broadcasted_iota(jnp.int32, (CHUNK, CHUNK), 1)
    upper = (r_i <= c_i).astype(F32)
    lower = (r_i >= c_i).astype(F32)
    raw = dt_ref[...].reshape(n_chunks * nh, CHUNK) + dtb_ref[...]
    dt_all = jnp.maximum(raw, 0.0) + jnp.log1p(jnp.exp(-jnp.abs(raw)))
    da = dt_all * (-jnp.exp(alog_ref[...]))
    row_all = lax.broadcasted_iota(jnp.int32, (n_chunks * nh, CHUNK), 0)
    cs_all = jnp.where(row_all % nh < HEADS_PER_GROUP, _dot_exact(da, upper), _dot_exact(da, lower))
    cs2_all = cs_all * LOG2_E
    cs_ref[...] = cs2_all.reshape(n_chunks, nh, CHUNK)
    src_ref[...] = (cs2_all - jnp.log2(dt_all)).reshape(n_chunks, nh, CHUNK)

    lane_head = lax.broadcasted_iota(jnp.int32, (1, gw), 1) // HEAD_DIM_S
    first_head = lax.broadcasted_iota(jnp.int32, (1, LANES), 1) < HEAD_DIM_S

    def chunk_step(c, reverse):
        r0 = HEADS_PER_GROUP if reverse else 0
        t0 = pl.multiple_of(c * CHUNK, CHUNK)
        x = xc_ref[pl.ds(t0, CHUNK), :]
        bm = bc_ref[pl.ds(t0, CHUNK), :]
        cm = cc_ref[pl.ds(t0, CHUNK), :].astype(BF16)
        cs2 = cs_ref[c]
        src = src_ref[c]
        cbm = cb_ref[c]
        if reverse:
            tot = cs2[:, 0:1]
            keep = r_i <= c_i
        else:
            tot = cs2[:, CHUNK - 1:CHUNK]
            keep = r_i >= c_i
        cs2_t = jnp.transpose(cs2)
        col_b = [jnp.broadcast_to(cs2_t[:, r0 + e:r0 + e + 1], (CHUNK, LANES)) for e in range(HEADS_PER_GROUP)]
        g_out = jnp.exp2(jnp.concatenate(
            [jnp.where(first_head, col_b[2 * v], col_b[2 * v + 1]) for v in range(gw // LANES)], axis=1))
        sel_row = (lax.broadcasted_iota(jnp.int32, (nh, gw), 0) == lane_head + r0).astype(F32)
        cdec = jnp.exp2(jnp.sum(tot * sel_row, axis=0, keepdims=True))
        wrow = jnp.exp2(tot - src)
        bt = jnp.transpose(bm)
        xb = x.astype(BF16)
        zero = jnp.zeros_like(xb)
        ms, bs, xs = [], [], []
        for e in range(HEADS_PER_GROUP):
            r = r0 + e
            seg = col_b[e] - src[r:r + 1, :]
            decay = jnp.exp2(jnp.where(keep, seg, -jnp.inf))
            ms.append((cbm * decay).astype(BF16))
            bs.append((bt * wrow[r:r + 1, :]).astype(BF16))
            xs.append(jnp.where(lane_head == e, xb, zero))
        xcat = jnp.concatenate(xs, axis=0)
        d = 1 if reverse else 0
        st = st_ref[d]
        y = _dot(jnp.concatenate(ms, axis=1), xcat) + _dot(cm, st.astype(BF16)) * g_out
        st_ref[d] = st * cdec + _dot(jnp.concatenate(bs, axis=1), xcat)
        y_ref[d, pl.ds(t0, CHUNK), :] = y

    st_ref[...] = jnp.zeros_like(st_ref)

    def scan(i, carry):
        chunk_step(i, False)
        chunk_step(n_chunks - 1 - i, True)
        return carry

    lax.fori_loop(0, n_chunks, scan, 0, unroll=8)

    def finish(c, carry):
        t0 = pl.multiple_of(c * CHUNK, CHUNK)
        rows = pl.ds(t0, CHUNK)
        y = y_ref[0, rows, :] + y_ref[1, rows, :] + xc_ref[rows, :] * dsk_ref[...]
        y = y * _silu(z_ref[rows, :].astype(F32))
        ms = jnp.mean(y * y, axis=1, keepdims=True)
        y = y * lax.rsqrt(ms + RMS_EPS) * nw_ref[...]
        o_ref[rows, :] = y.astype(o_ref.dtype)
        return carry

    lax.fori_loop(0, n_chunks, finish, 0, unroll=4)


def _ssd(proj3, dt_t, cw_x, cw_b, cw_c, cb_x, cb_b, cb_c, dtb, alog, dsk, nw):
    batch, seq, _ = proj3.shape
    gw, ns = GROUP_W, D_STATE
    nh = 2 * HEADS_PER_GROUP
    n_chunks = seq // CHUNK
    gspec = lambda w: pl.BlockSpec((1, w), lambda b, g: (0, g))
    return pl.pallas_call(
        _ssd_kernel,
        grid=(batch, N_GROUPS),
        in_specs=[
            pl.BlockSpec((None, seq, gw), lambda b, g: (b, 0, Z_OFF // gw + g)),
            pl.BlockSpec((None, seq, gw), lambda b, g: (b, 0, XS_OFF // gw + g)),
            pl.BlockSpec((None, seq, ns), lambda b, g: (b, 0, B_OFF // ns + g)),
            pl.BlockSpec((None, seq, ns), lambda b, g: (b, 0, C_OFF // ns + g)),
            pl.BlockSpec((None, n_chunks, nh, CHUNK), lambda b, g: (b, 0, g, 0)),
            pl.BlockSpec((CONV_SSD, gw), lambda b, g: (0, g)),
            pl.BlockSpec((CONV_SSD, ns), lambda b, g: (0, g)),
            pl.BlockSpec((CONV_SSD, ns), lambda b, g: (0, g)),
            gspec(gw), gspec(ns), gspec(ns),
            pl.BlockSpec((n_chunks * nh, 1), lambda b, g: (g, 0)),
            pl.BlockSpec((n_chunks * nh, 1), lambda b, g: (g, 0)),
            gspec(gw), gspec(gw),
        ],
        out_specs=pl.BlockSpec((None, seq, gw), lambda b, g: (b, 0, g)),
        out_shape=jax.ShapeDtypeStruct((batch, seq, D_INNER), BF16),
        scratch_shapes=[
            pltpu.VMEM((seq + 2 * SUBLANES, gw + 2 * ns), F32),
            pltpu.VMEM((seq, gw), F32),
            pltpu.VMEM((seq, ns), F32),
            pltpu.VMEM((seq, ns), F32),
            pltpu.VMEM((2, seq, gw), F32),
            pltpu.VMEM((n_chunks, CHUNK, CHUNK), F32),
            pltpu.VMEM((n_chunks, nh, CHUNK), F32),
            pltpu.VMEM((n_chunks, nh, CHUNK), F32),
            pltpu.VMEM((2, ns, gw), F32),
        ],
        compiler_params=pltpu.CompilerParams(
            dimension_semantics=("parallel", "parallel"), vmem_limit_bytes=VMEM_LIMIT),
        name="ssd",
    )(proj3, proj3, proj3, proj3, dt_t, cw_x, cw_b, cw_c, cb_x, cb_b, cb_c, dtb, alog, dsk, nw)


def _layer_norm(v, g, b):
    mu = jnp.mean(v, axis=1, keepdims=True)
    d = v - mu
    var = jnp.mean(d * d, axis=1, keepdims=True)
    return d * lax.rsqrt(var + LN_EPS) * g + b


def _merge_kernel(x_ref, at_ref, sd_ref, ga_ref, gs_ref, wpa_ref, wps_ref, wo_ref, bg_ref, g_ref, b_ref, o_ref):
    bg = bg_ref[...]
    g_a = jax.nn.sigmoid(ga_ref[...].astype(F32) + bg[:, :D_MODEL])
    g_s = jax.nn.sigmoid(gs_ref[...].astype(F32) + bg[:, D_MODEL:])
    merged = g_a * _dot(at_ref[...], wpa_ref[...]) + g_s * _dot(sd_ref[...], wps_ref[...])
    y = ALPHA * x_ref[...] + _dot(merged.astype(BF16), wo_ref[...])
    o_ref[...] = _layer_norm(y, g_ref[...], b_ref[...])


def _merge(x2, attn2, ssd2, proj2, wpa, wps, wo, bg, ln_g, ln_b, *, tm=512):
    t = x2.shape[0]
    full = lambda a: pl.BlockSpec(a.shape, lambda i: (0, 0))
    return pl.pallas_call(
        _merge_kernel,
        grid=(t // tm,),
        in_specs=[
            pl.BlockSpec((tm, D_MODEL), lambda i: (i, 0)),
            pl.BlockSpec((tm, ATTN_W), lambda i: (i, 0)),
            pl.BlockSpec((tm, D_INNER), lambda i: (i, 0)),
            pl.BlockSpec((tm, D_MODEL), lambda i: (i, GATE_OFF // D_MODEL)),
            pl.BlockSpec((tm, D_MODEL), lambda i: (i, GATE_OFF // D_MODEL + 1)),
            full(wpa), full(wps), full(wo), full(bg), full(ln_g), full(ln_b),
        ],
        out_specs=pl.BlockSpec((tm, D_MODEL), lambda i: (i, 0)),
        out_shape=jax.ShapeDtypeStruct((t, D_MODEL), F32),
        compiler_params=pltpu.CompilerParams(
            dimension_semantics=("parallel",), vmem_limit_bytes=VMEM_LIMIT),
        name="merge_ln1",
    )(x2, attn2, ssd2, proj2, proj2, wpa, wps, wo, bg, ln_g, ln_b)


def _ffn_kernel(x_ref, prev_ref, next_ref, wup_ref, cw_ref, cb_ref, wdn_ref, g_ref, b_ref, o_ref, xh_ref,
                ug0_ref, uu0_ref, ug1_ref, uu1_ref, *, tiles_per_seq, tf):
    i = pl.program_id(0)
    tm = x_ref.shape[0]
    halo = SUBLANES
    x = x_ref[...]
    first = (i % tiles_per_seq) == 0
    last = (i % tiles_per_seq) == tiles_per_seq - 1
    zero = jnp.zeros((halo, D_MODEL), F32)
    xh_ref[0:halo, :] = jnp.where(first, zero, prev_ref[...]).astype(BF16)
    xh_ref[halo:halo + tm, :] = x.astype(BF16)
    xh_ref[halo + tm:, :] = jnp.where(last, zero, next_ref[...]).astype(BF16)
    xh = xh_ref[...]

    def conv(u_ref, col):
        w = cw_ref[:, col:col + tf]
        out = cb_ref[:, col:col + tf] + u_ref[halo - 1:halo - 1 + tm, :] * w[0:1]
        out = out + u_ref[halo:halo + tm, :] * w[1:2]
        return out + u_ref[halo + 1:halo + 1 + tm, :] * w[2:3]

    bufs = ((ug0_ref, uu0_ref), (ug1_ref, uu1_ref))

    def up(j):
        ug_ref, uu_ref = bufs[j % 2]
        ug_ref[...] = _dot(xh, wup_ref[:, j * tf:(j + 1) * tf])
        uu_ref[...] = _dot(xh, wup_ref[:, D_FF + j * tf:D_FF + (j + 1) * tf])

    n_f = D_FF // tf
    acc = jnp.zeros((tm, D_MODEL), F32)
    up(0)
    for j in range(n_f):
        if j + 1 < n_f:
            up(j + 1)
        ug_ref, uu_ref = bufs[j % 2]
        act = (_silu(conv(ug_ref, j * tf)) * conv(uu_ref, D_FF + j * tf)).astype(BF16)
        acc = acc + _dot(act, wdn_ref[j * tf:(j + 1) * tf, :])
    o_ref[...] = _layer_norm(ALPHA * x + acc, g_ref[...], b_ref[...])


def _ffn(x1, wup, cw, cb, wdn, ln_g, ln_b, *, seq, tm=512, tf=256):
    t = x1.shape[0]
    hb = tm // SUBLANES
    n_hb = t // SUBLANES
    full = lambda a: pl.BlockSpec(a.shape, lambda i: (0, 0))
    kern = functools.partial(_ffn_kernel, tiles_per_seq=seq // tm, tf=tf)
    return pl.pallas_call(
        kern,
        grid=(t // tm,),
        in_specs=[
            pl.BlockSpec((tm, D_MODEL), lambda i: (i, 0)),
            pl.BlockSpec((SUBLANES, D_MODEL), lambda i: (jnp.maximum(i * hb - 1, 0), 0)),
            pl.BlockSpec((SUBLANES, D_MODEL), lambda i: (jnp.minimum((i + 1) * hb, n_hb - 1), 0)),
            full(wup), full(cw), full(cb), full(wdn), full(ln_g), full(ln_b),
        ],
        out_specs=pl.BlockSpec((tm, D_MODEL), lambda i: (i, 0)),
        out_shape=jax.ShapeDtypeStruct((t, D_MODEL), F32),
        scratch_shapes=([pltpu.VMEM((tm + 2 * SUBLANES, D_MODEL), BF16)]
                        + [pltpu.VMEM((tm + 2 * SUBLANES, tf), F32)] * 4),
        compiler_params=pltpu.CompilerParams(
            dimension_semantics=("parallel",), vmem_limit_bytes=VMEM_LIMIT),
        name="ffn_ln2",
    )(x1, x1, x1, wup, cw, cb, wdn, ln_g, ln_b)


def _rotary_tables(seq):
    half = ROT_DIM // 2
    pos = np.arange(seq, dtype=np.float32)
    inv_freq = np.power(np.float32(ROPE_THETA), -np.arange(0, ROT_DIM, 2, dtype=np.float32) / np.float32(ROT_DIM))
    ang = (pos[:, None] * inv_freq[None, :].astype(np.float32)).astype(np.float32)
    cos, sin = np.cos(ang).astype(np.float32), np.sin(ang).astype(np.float32)
    pad = np.zeros((seq, HEAD_DIM_A - ROT_DIM), np.float32)
    zer = np.zeros((seq, half), np.float32)
    c64 = np.concatenate([cos, cos, pad + 1.0], axis=1)
    sa64 = np.concatenate([-sin, zer, pad], axis=1)
    sb64 = np.concatenate([zer, sin, pad], axis=1)
    tile2 = lambda a: np.concatenate([a, a], axis=1)
    return jnp.asarray(np.stack([tile2(c64), tile2(sa64), tile2(sb64)]))


def _group_rows(p, n_chunks):
    nh = 2 * HEADS_PER_GROUP
    pg = p.reshape(2, N_GROUPS, HEADS_PER_GROUP).transpose(1, 0, 2).reshape(N_GROUPS, 1, nh)
    return jnp.broadcast_to(pg, (N_GROUPS, n_chunks, nh)).reshape(N_GROUPS * n_chunks * nh, 1)


def kernel(x, w_in, b_gate, lambda_q1, lambda_k1, lambda_q2, lambda_k2, attn_subln_w, conv_ssd_w, conv_ssd_b,
           dt_bias, a_log, d_skip, ssd_norm_w, w_proj_attn, w_proj_ssd, w_out, ln1_g, ln1_b, w_up, conv_ffn_w,
           conv_ffn_b, w_down, ln2_g, ln2_b):
    batch, seq, _ = x.shape
    t = batch * seq
    l = 0
    xbc_w = D_INNER + 2 * N_GROUPS * D_STATE
    dt_off = Z_OFF + D_INNER + xbc_w
    w = w_in[l]
    w_a = w.astype(BF16)
    w_b = w[:, dt_off + 2 * N_HEADS_S:].astype(BF16)
    w_dt = w[:, dt_off:dt_off + 2 * N_HEADS_S]
    w_dt_t = (w_dt.reshape(D_MODEL, 2, N_GROUPS, HEADS_PER_GROUP).transpose(2, 1, 3, 0)
              .reshape(2 * N_HEADS_S, D_MODEL).astype(BF16))
    rot_tab = _rotary_tables(seq)

    x2 = x.reshape(t, D_MODEL)
    proj2, dt_t = _in_proj(x2, w_a, w_b, w_dt_t, batch=batch, seq=seq)
    proj3 = proj2.reshape(batch, seq, PROJ_W)

    lam_p = jnp.stack([lambda_q1[l], lambda_k1[l], lambda_q2[l], lambda_k2[l]]).astype(F32)
    attn = _attention(proj3, rot_tab, lam_p, attn_subln_w[l].reshape(1, 2 * HEAD_DIM_A))

    cw = conv_ssd_w[l]
    cb = conv_ssd_b[l].reshape(1, xbc_w)
    nb = N_GROUPS * D_STATE
    ssd = _ssd(proj3, dt_t,
               cw[:, :D_INNER], cw[:, D_INNER:D_INNER + nb], cw[:, D_INNER + nb:],
               cb[:, :D_INNER], cb[:, D_INNER:D_INNER + nb], cb[:, D_INNER + nb:],
               _group_rows(dt_bias[l], seq // CHUNK), _group_rows(a_log[l], seq // CHUNK),
               jnp.repeat(d_skip[l], HEAD_DIM_S).reshape(1, D_INNER), ssd_norm_w[l].reshape(1, D_INNER))

    x1 = _merge(x2, attn.reshape(t, ATTN_W), ssd.reshape(t, D_INNER), proj2,
                w_proj_attn[l].astype(BF16), w_proj_ssd[l].astype(BF16), w_out[l].astype(BF16),
                b_gate[l].reshape(1, 2 * D_MODEL), ln1_g[l].reshape(1, D_MODEL), ln1_b[l].reshape(1, D_MODEL))

    out = _ffn(x1, w_up[l].astype(BF16), conv_ffn_w[l], conv_ffn_b[l].reshape(1, 2 * D_FF),
               w_down[l].astype(BF16), ln2_g[l].reshape(1, D_MODEL), ln2_b[l].reshape(1, D_MODEL), seq=seq)
    return out.reshape(batch, seq, D_MODEL)
```

```python
import functools
import math

import jax
import jax.numpy as jnp
import numpy as np
from jax import lax
from jax.experimental import pallas as pl
from jax.experimental.pallas import tpu as pltpu

F32 = jnp.float32
BF16 = jnp.bfloat16

D_MODEL = 1024
N_HEADS_A = 8
HEAD_DIM_A = 64
ROT_DIM = HEAD_DIM_A // 4
ROPE_THETA = 500000.0
ATTN_W = N_HEADS_A * 2 * HEAD_DIM_A
D_INNER = 2 * D_MODEL
HEAD_DIM_S = 64
N_HEADS_S = D_INNER // HEAD_DIM_S
N_GROUPS = 8
HEADS_PER_GROUP = N_HEADS_S // N_GROUPS
GROUP_W = HEADS_PER_GROUP * HEAD_DIM_S
D_STATE = 128
CONV_SSD = 5
CHUNK = 128
D_FF = 2816
DEPTH = 1
ALPHA = (2 * DEPTH) ** 0.25
LN_EPS = 1e-5
RMS_EPS = 1e-5
LAM_INIT = 0.8 - 0.6 * math.exp(-0.3 * 0)
LOG2_E = math.log2(math.e)

LANES = 128
SUBLANES = 8
CONV_STRIDE = 4
VMEM_LIMIT = 56 * 1024 * 1024

Q_OFF, K_OFF, V_OFF = 0, ATTN_W, 2 * ATTN_W
Z_OFF = 3 * ATTN_W
XS_OFF = Z_OFF + D_INNER
B_OFF = XS_OFF + D_INNER
C_OFF = B_OFF + N_GROUPS * D_STATE
GATE_OFF = C_OFF + N_GROUPS * D_STATE
PROJ_W = GATE_OFF + 2 * D_MODEL


def _dot(a, b):
    return jnp.dot(a, b, preferred_element_type=F32)


def _dot_nt(a, b):
    return lax.dot_general(a, b, (((1,), (1,)), ((), ())), preferred_element_type=F32)


def _dot_exact(a, b):
    return jnp.dot(a, b, preferred_element_type=F32, precision=lax.Precision.HIGHEST)


def _in_proj_kernel(x_ref, wa_ref, wb_ref, wdt_ref, o_ref, dt_ref, xb_ref, *, n_q, n_a):
    j = pl.program_id(1)

    @pl.when(j == 0)
    def _():
        xb = x_ref[...].astype(BF16)
        xb_ref[...] = xb
        dt_t = _dot_nt(wdt_ref[...], xb)
        for c in range(dt_ref.shape[0]):
            dt_ref[c] = dt_t[:, c * CHUNK:(c + 1) * CHUNK]

    @pl.when(j < n_q)
    def _():
        o_ref[...] = (_dot(xb_ref[...], wa_ref[...]) * (HEAD_DIM_A ** -0.5 * LOG2_E)).astype(o_ref.dtype)

    @pl.when((j >= n_q) & (j < n_a))
    def _():
        o_ref[...] = _dot(xb_ref[...], wa_ref[...]).astype(o_ref.dtype)

    @pl.when(j >= n_a)
    def _():
        o_ref[...] = _dot(xb_ref[...], wb_ref[...]).astype(o_ref.dtype)


def _in_proj(x2, w_a, w_b, w_dt_t, *, batch, seq, tm=2048, tn=1024):
    t = x2.shape[0]
    n_seq_tiles = seq // tm
    n_dt = w_dt_t.shape[0]
    n_a = GATE_OFF // tn
    kern = functools.partial(_in_proj_kernel, n_q=ATTN_W // tn, n_a=n_a)
    return pl.pallas_call(
        kern,
        grid=(t // tm, PROJ_W // tn),
        in_specs=[
            pl.BlockSpec((tm, D_MODEL), lambda i, j: (i, 0)),
            pl.BlockSpec((D_MODEL, tn), lambda i, j: (0, jnp.minimum(j, n_a - 1))),
            pl.BlockSpec((D_MODEL, tn), lambda i, j: (0, jnp.maximum(j - n_a, 0))),
            pl.BlockSpec((n_dt, D_MODEL), lambda i, j: (0, 0)),
        ],
        out_specs=[
            pl.BlockSpec((tm, tn), lambda i, j: (i, j)),
            pl.BlockSpec((None, tm // CHUNK, n_dt, CHUNK),
                         lambda i, j: (i // n_seq_tiles, i % n_seq_tiles, 0, 0)),
        ],
        out_shape=[
            jax.ShapeDtypeStruct((t, PROJ_W), BF16),
            jax.ShapeDtypeStruct((batch, seq // CHUNK, n_dt, CHUNK), F32),
        ],
        scratch_shapes=[pltpu.VMEM((tm, D_MODEL), BF16)],
        compiler_params=pltpu.CompilerParams(
            dimension_semantics=("parallel", "arbitrary"), vmem_limit_bytes=VMEM_LIMIT),
        name="in_proj",
    )(x2, w_a, w_b, w_dt_t)


def _rotary(a_ref, tab_ref, rows):
    a = a_ref[rows, :].astype(F32)
    up = pltpu.roll(a, LANES - ROT_DIM // 2, axis=1)
    dn = pltpu.roll(a, ROT_DIM // 2, axis=1)
    return (a * tab_ref[0, rows, :] + up * tab_ref[1, rows, :] + dn * tab_ref[2, rows, :]).astype(BF16)


def _attn_kernel(q_ref, k_ref, v_ref, tq_ref, tk_ref, lam_ref, sw_ref, o_ref, kr_ref, v1_ref, s1_ref, s2_ref, *, tk):
    tq = q_ref.shape[0]
    seq = k_ref.shape[0]
    n_kc = seq // tk

    @pl.when(pl.program_id(2) == 0)
    def _():
        v1_ref[:, :LANES] = v_ref[...]
        v1_ref[:, LANES:] = jnp.ones((seq, LANES), BF16)
        for c in range(n_kc):
            rows = slice(c * tk, (c + 1) * tk)
            kr_ref[rows, :] = _rotary(k_ref, tk_ref, rows)

    q = _rotary(q_ref, tq_ref, slice(None))
    lane = lax.broadcasted_iota(jnp.int32, q.shape, 1)
    zero = jnp.zeros_like(q)
    q1 = jnp.where(lane < HEAD_DIM_A, q, zero)
    q2 = jnp.where(lane >= HEAD_DIM_A, q, zero)

    m1 = jnp.full((tq, LANES), -jnp.inf, F32)
    m2 = m1
    for c in range(n_kc):
        kc = kr_ref[c * tk:(c + 1) * tk, :]
        sc1 = _dot_nt(q1, kc)
        sc2 = _dot_nt(q2, kc)
        s1_ref[c] = sc1
        s2_ref[c] = sc2
        for u in range(tk // LANES):
            m1 = jnp.maximum(m1, sc1[:, u * LANES:(u + 1) * LANES])
            m2 = jnp.maximum(m2, sc2[:, u * LANES:(u + 1) * LANES])
    m1 = jnp.max(m1, axis=1, keepdims=True)
    m2 = jnp.max(m2, axis=1, keepdims=True)

    a1 = jnp.zeros((tq, 2 * LANES), F32)
    a2 = a1
    for c in range(n_kc):
        vc = v1_ref[c * tk:(c + 1) * tk, :]
        p1 = jnp.exp2(s1_ref[c] - m1).astype(BF16)
        p2 = jnp.exp2(s2_ref[c] - m2).astype(BF16)
        a1 = a1 + _dot(p1, vc)
        a2 = a2 + _dot(p2, vc)

    lp = lam_ref[...]
    lam = (jnp.exp(jnp.sum(lp[0:1] * lp[1:2], axis=1, keepdims=True))
           - jnp.exp(jnp.sum(lp[2:3] * lp[3:4], axis=1, keepdims=True)) + LAM_INIT)
    o =a1[:, :LANES] / a1[:, LANES:] - lam * (a2[:, :LANES] / a2[:, LANES:])
    ms = jnp.mean(o * o, axis=1, keepdims=True)
    o = o * lax.rsqrt(ms + RMS_EPS) * sw_ref[...] * (1.0 - LAM_INIT)
    o_ref[...] = o.astype(o_ref.dtype)


def _attention(proj3, rot_tab, lam_p, subln_w, *, tq=512, tk=512):
    batch, seq, _ = proj3.shape
    hw = 2 * HEAD_DIM_A
    kern = functools.partial(_attn_kernel, tk=tk)
    return pl.pallas_call(
        kern,
        grid=(batch, N_HEADS_A, seq // tq),
        in_specs=[
            pl.BlockSpec((None, tq, hw), lambda b, h, i: (b, i, Q_OFF // hw + h)),
            pl.BlockSpec((None, seq, hw), lambda b, h, i: (b, 0, K_OFF // hw + h)),
            pl.BlockSpec((None, seq, hw), lambda b, h, i: (b, 0, V_OFF // hw + h)),
            pl.BlockSpec((3, tq, hw), lambda b, h, i: (0, i, 0)),
            pl.BlockSpec((3, seq, hw), lambda b, h, i: (0, 0, 0)),
            pl.BlockSpec((4, HEAD_DIM_A), lambda b, h, i: (0, 0)),
            pl.BlockSpec((1, hw), lambda b, h, i: (0, 0)),
        ],
        out_specs=pl.BlockSpec((None, tq, hw), lambda b, h, i: (b, i, h)),
        out_shape=jax.ShapeDtypeStruct((batch, seq, ATTN_W), BF16),
        scratch_shapes=[
            pltpu.VMEM((seq, hw), BF16),
            pltpu.VMEM((seq, 2 * LANES), BF16),
            pltpu.VMEM((seq // tk, tq, tk), F32),
            pltpu.VMEM((seq // tk, tq, tk), F32),
        ],
        compiler_params=pltpu.CompilerParams(
            dimension_semantics=("parallel", "parallel", "arbitrary"), vmem_limit_bytes=VMEM_LIMIT),
        name="diff_attn",
    )(proj3, proj3, proj3, rot_tab, rot_tab, lam_p, subln_w)


def _silu(v):
    return v * jax.nn.sigmoid(v)


def _ssd_kernel(z_ref, xs_ref, b_ref, c_ref, dt_ref, cwx_ref, cwb_ref, cwc_ref, cbx_ref, cbb_ref, cbc_ref,
                dtb_ref, alog_ref, dsk_ref, nw_ref, o_ref,
                pad_ref, cv_ref, y_ref, cb_ref, src_ref, cs_ref, st_ref):
    seq = xs_ref.shape[0]
    n_chunks = seq // CHUNK
    halo = SUBLANES
    half = CONV_SSD // 2
    gw = GROUP_W
    w_all = gw + 2 * D_STATE

    n_slab = w_all // LANES
    srcs = (xs_ref, xs_ref, b_ref, c_ref)
    lane0 = (0, LANES, 0, 0)
    zero_halo = jnp.zeros((halo, LANES), F32)
    for s in range(n_slab):
        pad_ref[s, 0:halo, :] = zero_halo
        pad_ref[s, halo + seq:, :] = zero_halo
        pad_ref[s, halo:halo + seq, :] = srcs[s][:, lane0[s]:lane0[s] + LANES].astype(F32)
    cw = jnp.concatenate([cwx_ref[...], cwb_ref[...], cwc_ref[...]], axis=1)
    cbias = jnp.concatenate([cbx_ref[...], cbb_ref[...], cbc_ref[...]], axis=1)
    stride = CONV_STRIDE
    blk = SUBLANES * stride

    def conv_chunk(c, carry):
        t0 = pl.multiple_of(c * CHUNK, CHUNK)
        for s in range(n_slab):
            w_s = cw[:, s * LANES:(s + 1) * LANES]
            b_s = jnp.broadcast_to(cbias[:, s * LANES:(s + 1) * LANES], (SUBLANES, LANES))
            for r in range(0, CHUNK, blk):
                for v in range(stride):
                    base = t0 + (r + v)
                    acc = b_s
                    for k in range(CONV_SSD):
                        acc = acc + pad_ref[s, pl.ds(base + (halo - half + k), SUBLANES, stride=stride), :] * w_s[k:k + 1, :]
                    cv_ref[s, pl.ds(base, SUBLANES, stride=stride), :] = _silu(acc)
        rows = pl.ds(t0, CHUNK)
        cb_ref[c] = _dot_nt(cv_ref[3, rows, :].astype(BF16), cv_ref[2, rows, :].astype(BF16))
        return carry

    lax.fori_loop(0, n_chunks, conv_chunk, 0, unroll=4)

    nh = 2 * HEADS_PER_GROUP
    r_i = lax.broadcasted_iota(jnp.int32, (CHUNK, CHUNK), 0)
    c_i = lax.broadcasted_iota(jnp.int32, (CHUNK, CHUNK), 1)
    upper = (r_i <= c_i).astype(F32)
    lower = (r_i >= c_i).astype(F32)
    raw = dt_ref[...].reshape(n_chunks * nh, CHUNK) + dtb_ref[...]
    dt_all = jnp.maximum(raw, 0.0) + jnp.log1p(jnp.exp(-jnp.abs(raw)))
    da = dt_all * (-jnp.exp(alog_ref[...]))
    row_all = lax.broadcasted_iota(jnp.int32, (n_chunks * nh, CHUNK), 0)
    cs_all = jnp.where(row_all % nh < HEADS_PER_GROUP, _dot_exact(da, upper), _dot_exact(da, lower))
    cs2_all = cs_all * LOG2_E
    cs_ref[...] = cs2_all.reshape(n_chunks, nh, CHUNK)
    src_ref[...] = (cs2_all - jnp.log2(dt_all)).reshape(n_chunks, nh, CHUNK)

    lane_head = lax.broadcasted_iota(jnp.int32, (1, gw), 1) // HEAD_DIM_S
    first_head = lax.broadcasted_iota(jnp.int32, (1, LANES), 1) < HEAD_DIM_S

    def chunk_step(c, reverse):
        r0 = HEADS_PER_GROUP if reverse else 0
        t0 = pl.multiple_of(c * CHUNK, CHUNK)
        rows = pl.ds(t0, CHUNK)
        x = jnp.concatenate([cv_ref[0, rows, :], cv_ref[1, rows, :]], axis=1)
        bm = cv_ref[2, rows, :]
        cm = cv_ref[3, rows, :].astype(BF16)
        cs2 = cs_ref[c]
        src = src_ref[c]
        cbm = cb_ref[c]
        if reverse:
            tot = cs2[:, 0:1]
            keep = r_i <= c_i
        else:
            tot = cs2[:, CHUNK - 1:CHUNK]
            keep = r_i >= c_i
        cs2_t = jnp.transpose(cs2)
        col_b = [jnp.broadcast_to(cs2_t[:, r0 + e:r0 + e + 1], (CHUNK, LANES)) for e in range(HEADS_PER_GROUP)]
        g_out = jnp.exp2(jnp.concatenate(
            [jnp.where(first_head, col_b[2 * v], col_b[2 * v + 1]) for v in range(gw // LANES)], axis=1))
        sel_row = (lax.broadcasted_iota(jnp.int32, (nh, gw), 0) == lane_head + r0).astype(F32)
        cdec = jnp.exp2(jnp.sum(tot * sel_row, axis=0, keepdims=True))
        wrow = jnp.exp2(tot - src)
        bt = jnp.transpose(bm)
        xb = x.astype(BF16)
        zero = jnp.zeros_like(xb)
        ms, bs, xs = [], [], []
        for e in range(HEADS_PER_GROUP):
            r = r0 + e
            seg = col_b[e] - src[r:r + 1, :]
            decay = jnp.exp2(jnp.where(keep, seg, -jnp.inf))
            ms.append((cbm * decay).astype(BF16))
            bs.append((bt * wrow[r:r + 1, :]).astype(BF16))
            xs.append(jnp.where(lane_head == e, xb, zero))
        xcat = jnp.concatenate(xs, axis=0)
        d = 1 if reverse else 0
        st = st_ref[d]
        y = _dot(jnp.concatenate(ms, axis=1), xcat) + _dot(cm, st.astype(BF16)) * g_out
        st_ref[d] = st * cdec + _dot(jnp.concatenate(bs, axis=1), xcat)
        y_ref[d, pl.ds(t0, CHUNK), :] = y

    st_ref[...] = jnp.zeros_like(st_ref)

    def scan(i, carry):
        chunk_step(i, False)
        chunk_step(n_chunks - 1 - i, True)
        return carry

    lax.fori_loop(0, n_chunks, scan, 0, unroll=8)

    def finish(c, carry):
        t0 = pl.multiple_of(c * CHUNK, CHUNK)
        rows = pl.ds(t0, CHUNK)
        xc = jnp.concatenate([cv_ref[0, rows, :], cv_ref[1, rows, :]], axis=1)
        y = y_ref[0, rows, :] + y_ref[1, rows, :] + xc * dsk_ref[...]
        y = y * _silu(z_ref[rows, :].astype(F32))
        ms = jnp.mean(y * y, axis=1, keepdims=True)
        y = y * lax.rsqrt(ms + RMS_EPS) * nw_ref[...]
        o_ref[rows, :] = y.astype(o_ref.dtype)
        return carry

    lax.fori_loop(0, n_chunks, finish, 0, unroll=4)


def _ssd(proj3, dt_t, cw_x, cw_b, cw_c, cb_x, cb_b, cb_c, dtb, alog, dsk, nw):
    batch, seq, _ = proj3.shape
    gw, ns = GROUP_W, D_STATE
    nh = 2 * HEADS_PER_GROUP
    n_chunks = seq // CHUNK
    gspec = lambda w: pl.BlockSpec((1, w), lambda b, g: (0, g))
    return pl.pallas_call(
        _ssd_kernel,
        grid=(batch, N_GROUPS),
        in_specs=[
            pl.BlockSpec((None, seq, gw), lambda b, g: (b, 0, Z_OFF // gw + g)),
            pl.BlockSpec((None, seq, gw), lambda b, g: (b, 0, XS_OFF // gw + g)),
            pl.BlockSpec((None, seq, ns), lambda b, g: (b, 0, B_OFF // ns + g)),
            pl.BlockSpec((None, seq, ns), lambda b, g: (b, 0, C_OFF // ns + g)),
            pl.BlockSpec((None, n_chunks, nh, CHUNK), lambda b, g: (b, 0, g, 0)),
            pl.BlockSpec((CONV_SSD, gw), lambda b, g: (0, g)),
            pl.BlockSpec((CONV_SSD, ns), lambda b, g: (0, g)),
            pl.BlockSpec((CONV_SSD, ns), lambda b, g: (0, g)),
            gspec(gw), gspec(ns), gspec(ns),
            pl.BlockSpec((n_chunks * nh, 1), lambda b, g: (g, 0)),
            pl.BlockSpec((n_chunks * nh, 1), lambda b, g: (g, 0)),
            gspec(gw), gspec(gw),
        ],
        out_specs=pl.BlockSpec((None, seq, gw), lambda b, g: (b, 0, g)),
        out_shape=jax.ShapeDtypeStruct((batch, seq, D_INNER), BF16),
        scratch_shapes=[
            pltpu.VMEM(((gw + 2 * ns) // LANES, seq + 2 * SUBLANES, LANES), F32),
            pltpu.VMEM(((gw + 2 * ns) // LANES, seq, LANES), F32),
            pltpu.VMEM((2, seq, gw), F32),
            pltpu.VMEM((n_chunks, CHUNK, CHUNK), F32),
            pltpu.VMEM((n_chunks, nh, CHUNK), F32),
            pltpu.VMEM((n_chunks, nh, CHUNK), F32),
            pltpu.VMEM((2, ns, gw), F32),
        ],
        compiler_params=pltpu.CompilerParams(
            dimension_semantics=("parallel", "parallel"), vmem_limit_bytes=VMEM_LIMIT),
        name="ssd",
    )(proj3, proj3, proj3, proj3, dt_t, cw_x, cw_b, cw_c, cb_x, cb_b, cb_c, dtb, alog, dsk, nw)


def _layer_norm(v, g, b):
    mu = jnp.mean(v, axis=1, keepdims=True)
    d = v - mu
    var = jnp.mean(d * d, axis=1, keepdims=True)
    return d * lax.rsqrt(var + LN_EPS) * g + b


def _merge_kernel(x_ref, at_ref, sd_ref, ga_ref, gs_ref, wpa_ref, wps_ref, wo_ref, bg_ref, g_ref, b_ref, o_ref):
    bg = bg_ref[...]
    g_a = jax.nn.sigmoid(ga_ref[...].astype(F32) + bg[:, :D_MODEL])
    g_s = jax.nn.sigmoid(gs_ref[...].astype(F32) + bg[:, D_MODEL:])
    merged = g_a * _dot(at_ref[...], wpa_ref[...]) + g_s * _dot(sd_ref[...], wps_ref[...])
    y = ALPHA * x_ref[...] + _dot(merged.astype(BF16), wo_ref[...])
    o_ref[...] = _layer_norm(y, g_ref[...], b_ref[...])


def _merge(x2, attn2, ssd2, proj2, wpa, wps, wo, bg, ln_g, ln_b, *, tm=512):
    t = x2.shape[0]
    full = lambda a: pl.BlockSpec(a.shape, lambda i: (0, 0))
    return pl.pallas_call(
        _merge_kernel,
        grid=(t // tm,),
        in_specs=[
            pl.BlockSpec((tm, D_MODEL), lambda i: (i, 0)),
            pl.BlockSpec((tm, ATTN_W), lambda i: (i, 0)),
            pl.BlockSpec((tm, D_INNER), lambda i: (i, 0)),
            pl.BlockSpec((tm, D_MODEL), lambda i: (i, GATE_OFF // D_MODEL)),
            pl.BlockSpec((tm, D_MODEL), lambda i: (i, GATE_OFF // D_MODEL + 1)),
            full(wpa), full(wps), full(wo), full(bg), full(ln_g), full(ln_b),
        ],
        out_specs=pl.BlockSpec((tm, D_MODEL), lambda i: (i, 0)),
        out_shape=jax.ShapeDtypeStruct((t, D_MODEL), F32),
        compiler_params=pltpu.CompilerParams(
            dimension_semantics=("parallel",), vmem_limit_bytes=VMEM_LIMIT),
        name="merge_ln1",
    )(x2, attn2, ssd2, proj2, proj2, wpa, wps, wo, bg, ln_g, ln_b)


def _ffn_kernel(x_ref, prev_ref, next_ref, wup_ref, cw_ref, cb_ref, wdn_ref, g_ref, b_ref, o_ref, xh_ref,
                ug0_ref, uu0_ref, ug1_ref, uu1_ref, *, tiles_per_seq, tf):
    i = pl.program_id(0)
    tm = x_ref.shape[0]
    halo = SUBLANES
    x = x_ref[...]
    first = (i % tiles_per_seq) == 0
    last = (i % tiles_per_seq) == tiles_per_seq - 1
    zero = jnp.zeros((halo, D_MODEL), F32)
    xh_ref[0:halo, :] = jnp.where(first, zero, prev_ref[...]).astype(BF16)
    xh_ref[halo:halo + tm, :] = x.astype(BF16)
    xh_ref[halo + tm:, :] = jnp.where(last, zero, next_ref[...]).astype(BF16)
    xh = xh_ref[...]

    def conv(u_ref, col):
        w = cw_ref[:, col:col + tf]
        out = cb_ref[:, col:col + tf] + u_ref[halo - 1:halo - 1 + tm, :] * w[0:1]
        out = out + u_ref[halo:halo + tm, :] * w[1:2]
        return out + u_ref[halo + 1:halo + 1 + tm, :] * w[2:3]

    bufs = ((ug0_ref, uu0_ref), (ug1_ref, uu1_ref))

    def up(j):
        ug_ref, uu_ref = bufs[j % 2]
        ug_ref[...] = _dot(xh, wup_ref[:, j * tf:(j + 1) * tf])
        uu_ref[...] = _dot(xh, wup_ref[:, D_FF + j * tf:D_FF + (j + 1) * tf])

    n_f = D_FF // tf
    acc = jnp.zeros((tm, D_MODEL), F32)
    up(0)
    for j in range(n_f):
        if j + 1 < n_f:
            up(j + 1)
        ug_ref, uu_ref = bufs[j % 2]
        act = (_silu(conv(ug_ref, j * tf)) * conv(uu_ref, D_FF + j * tf)).astype(BF16)
        acc = acc + _dot(act, wdn_ref[j * tf:(j + 1) * tf, :])
    o_ref[...] = _layer_norm(ALPHA * x + acc, g_ref[...], b_ref[...])


def _ffn(x1, wup, cw, cb, wdn, ln_g, ln_b, *, seq, tm=512, tf=256):
    t = x1.shape[0]
    hb = tm // SUBLANES
    n_hb = t // SUBLANES
    full = lambda a: pl.BlockSpec(a.shape, lambda i: (0, 0))
    kern = functools.partial(_ffn_kernel, tiles_per_seq=seq // tm, tf=tf)
    return pl.pallas_call(
        kern,
        grid=(t // tm,),
        in_specs=[
            pl.BlockSpec((tm, D_MODEL), lambda i: (i, 0)),
            pl.BlockSpec((SUBLANES, D_MODEL), lambda i: (jnp.maximum(i * hb - 1, 0), 0)),
            pl.BlockSpec((SUBLANES, D_MODEL), lambda i: (jnp.minimum((i + 1) * hb, n_hb - 1), 0)),
            full(wup), full(cw), full(cb), full(wdn), full(ln_g), full(ln_b),
        ],
        out_specs=pl.BlockSpec((tm, D_MODEL), lambda i: (i, 0)),
        out_shape=jax.ShapeDtypeStruct((t, D_MODEL), F32),
        scratch_shapes=([pltpu.VMEM((tm + 2 * SUBLANES, D_MODEL), BF16)]
                        + [pltpu.VMEM((tm + 2 * SUBLANES, tf), F32)] * 4),
        compiler_params=pltpu.CompilerParams(
            dimension_semantics=("parallel",), vmem_limit_bytes=VMEM_LIMIT),
        name="ffn_ln2",
    )(x1, x1, x1, wup, cw, cb, wdn, ln_g, ln_b)


def _rotary_tables(seq):
    half = ROT_DIM // 2
    pos = np.arange(seq, dtype=np.float32)
    inv_freq = np.power(np.float32(ROPE_THETA), -np.arange(0, ROT_DIM, 2, dtype=np.float32) / np.float32(ROT_DIM))
    ang = (pos[:, None] * inv_freq[None, :].astype(np.float32)).astype(np.float32)
    cos, sin = np.cos(ang).astype(np.float32), np.sin(ang).astype(np.float32)
    pad = np.zeros((seq, HEAD_DIM_A - ROT_DIM), np.float32)
    zer = np.zeros((seq, half), np.float32)
    c64 = np.concatenate([cos, cos, pad + 1.0], axis=1)
    sa64 = np.concatenate([-sin, zer, pad], axis=1)
    sb64 = np.concatenate([zer, sin, pad], axis=1)
    tile2 = lambda a: np.concatenate([a, a], axis=1)
    return jnp.asarray(np.stack([tile2(c64), tile2(sa64), tile2(sb64)]))


def _group_rows(p, n_chunks):
    nh = 2 * HEADS_PER_GROUP
    pg = p.reshape(2, N_GROUPS, HEADS_PER_GROUP).transpose(1, 0, 2).reshape(N_GROUPS, 1, nh)
    return jnp.broadcast_to(pg, (N_GROUPS, n_chunks, nh)).reshape(N_GROUPS * n_chunks * nh, 1)


def kernel(x, w_in, b_gate, lambda_q1, lambda_k1, lambda_q2, lambda_k2, attn_subln_w, conv_ssd_w, conv_ssd_b,
           dt_bias, a_log, d_skip, ssd_norm_w, w_proj_attn, w_proj_ssd, w_out, ln1_g, ln1_b, w_up, conv_ffn_w,
           conv_ffn_b, w_down, ln2_g, ln2_b):
    batch, seq, _ = x.shape
    t = batch * seq
    l = 0
    xbc_w = D_INNER + 2 * N_GROUPS * D_STATE
    dt_off = Z_OFF + D_INNER + xbc_w
    w = w_in[l]
    w_a = w.astype(BF16)
    w_b = w[:, dt_off + 2 * N_HEADS_S:].astype(BF16)
    w_dt = w[:, dt_off:dt_off + 2 * N_HEADS_S]
    w_dt_t = (w_dt.reshape(D_MODEL, 2, N_GROUPS, HEADS_PER_GROUP).transpose(2, 1, 3, 0)
              .reshape(2 * N_HEADS_S, D_MODEL).astype(BF16))
    rot_tab = _rotary_tables(seq)

    x2 = x.reshape(t, D_MODEL)
    proj2, dt_t = _in_proj(x2, w_a, w_b, w_dt_t, batch=batch, seq=seq)
    proj3 = proj2.reshape(batch, seq, PROJ_W)

    lam_p = jnp.stack([lambda_q1[l], lambda_k1[l], lambda_q2[l], lambda_k2[l]]).astype(F32)
    attn = _attention(proj3, rot_tab, lam_p, attn_subln_w[l].reshape(1, 2 * HEAD_DIM_A))

    cw = conv_ssd_w[l]
    cb = conv_ssd_b[l].reshape(1, xbc_w)
    nb = N_GROUPS * D_STATE
    ssd = _ssd(proj3, dt_t,
               cw[:, :D_INNER], cw[:, D_INNER:D_INNER + nb], cw[:, D_INNER + nb:],
               cb[:, :D_INNER], cb[:, D_INNER:D_INNER + nb], cb[:, D_INNER + nb:],
               _group_rows(dt_bias[l], seq // CHUNK), _group_rows(a_log[l], seq // CHUNK),
               jnp.repeat(d_skip[l], HEAD_DIM_S).reshape(1, D_INNER), ssd_norm_w[l].reshape(1, D_INNER))

    x1 = _merge(x2, attn.reshape(t, ATTN_W), ssd.reshape(t, D_INNER), proj2,
                w_proj_attn[l].astype(BF16), w_proj_ssd[l].astype(BF16), w_out[l].astype(BF16),
                b_gate[l].reshape(1, 2 * D_MODEL), ln1_g[l].reshape(1, D_MODEL), ln1_b[l].reshape(1, D_MODEL))

    out = _ffn(x1, w_up[l].astype(BF16), conv_ffn_w[l], conv_ffn_b[l].reshape(1, 2 * D_FF),
               w_down[l].astype(BF16), ln2_g[l].reshape(1, D_MODEL), ln2_b[l].reshape(1, D_MODEL), seq=seq)
    return out.reshape(batch, seq, D_MODEL)
```

```python
import functools
import math

import jax
import jax.numpy as jnp
import numpy as np
from jax import lax
from jax.experimental import pallas as pl
from jax.experimental.pallas import tpu as pltpu

F32 = jnp.float32
BF16 = jnp.bfloat16

D_MODEL = 1024
N_HEADS_A = 8
HEAD_DIM_A = 64
ROT_DIM = HEAD_DIM_A // 4
ROPE_THETA = 500000.0
ATTN_W = N_HEADS_A * 2 * HEAD_DIM_A
D_INNER = 2 * D_MODEL
HEAD_DIM_S = 64
N_HEADS_S = D_INNER // HEAD_DIM_S
N_GROUPS = 8
HEADS_PER_GROUP = N_HEADS_S // N_GROUPS
GROUP_W = HEADS_PER_GROUP * HEAD_DIM_S
D_STATE = 128
CONV_SSD = 5
CHUNK = 128
D_FF = 2816
DEPTH = 1
ALPHA = (2 * DEPTH) ** 0.25
LN_EPS = 1e-5
RMS_EPS = 1e-5
LAM_INIT = 0.8 - 0.6 * math.exp(-0.3 * 0)
LOG2_E = math.log2(math.e)

LANES = 128
SUBLANES = 8
CONV_STRIDE = 4
VMEM_LIMIT = 56 * 1024 * 1024

Q_OFF, K_OFF, V_OFF = 0, ATTN_W, 2 * ATTN_W
Z_OFF = 3 * ATTN_W
XS_OFF = Z_OFF + D_INNER
B_OFF = XS_OFF + D_INNER
C_OFF = B_OFF + N_GROUPS * D_STATE
GATE_OFF = C_OFF + N_GROUPS * D_STATE
PROJ_W = GATE_OFF + 2 * D_MODEL


def _dot(a, b):
    return jnp.dot(a, b, preferred_element_type=F32)


def _dot_nt(a, b):
    return lax.dot_general(a, b, (((1,), (1,)), ((), ())), preferred_element_type=F32)


def _dot_exact(a, b):
    return jnp.dot(a, b, preferred_element_type=F32, precision=lax.Precision.HIGHEST)


def _in_proj_kernel(x_ref, wa_ref, wb_ref, wdt_ref, o_ref, dt_ref, xb_ref, *, n_q, n_a):
    j = pl.program_id(1)

    @pl.when(j == 0)
    def _():
        xb = x_ref[...].astype(BF16)
        xb_ref[...] = xb
        dt_t = _dot_nt(wdt_ref[...], xb)
        for c in range(dt_ref.shape[0]):
            dt_ref[c] = dt_t[:, c * CHUNK:(c + 1) * CHUNK]

    @pl.when(j < n_q)
    def _():
        o_ref[...] = (_dot(xb_ref[...], wa_ref[...]) * (HEAD_DIM_A ** -0.5 * LOG2_E)).astype(o_ref.dtype)

    @pl.when((j >= n_q) & (j < n_a))
    def _():
        o_ref[...] = _dot(xb_ref[...], wa_ref[...]).astype(o_ref.dtype)

    @pl.when(j >= n_a)
    def _():
        o_ref[...] = _dot(xb_ref[...], wb_ref[...]).astype(o_ref.dtype)


def _in_proj(x2, w_a, w_b, w_dt_t, *, batch, seq, tm=2048, tn=1024):
    t = x2.shape[0]
    n_seq_tiles = seq // tm
    n_dt = w_dt_t.shape[0]
    n_a = GATE_OFF // tn
    kern = functools.partial(_in_proj_kernel, n_q=ATTN_W // tn, n_a=n_a)
    return pl.pallas_call(
        kern,
        grid=(t // tm, PROJ_W // tn),
        in_specs=[
            pl.BlockSpec((tm, D_MODEL), lambda i, j: (i, 0)),
            pl.BlockSpec((D_MODEL, tn), lambda i, j: (0, jnp.minimum(j, n_a - 1))),
            pl.BlockSpec((D_MODEL, tn), lambda i, j: (0, jnp.maximum(j - n_a, 0))),
            pl.BlockSpec((n_dt, D_MODEL), lambda i, j: (0, 0)),
        ],
        out_specs=[
            pl.BlockSpec((tm, tn), lambda i, j: (i, j)),
            pl.BlockSpec((None, tm // CHUNK, n_dt, CHUNK),
                         lambda i, j: (i // n_seq_tiles, i % n_seq_tiles, 0, 0)),
        ],
        out_shape=[
            jax.ShapeDtypeStruct((t, PROJ_W), BF16),
            jax.ShapeDtypeStruct((batch, seq // CHUNK, n_dt, CHUNK), F32),
        ],
        scratch_shapes=[pltpu.VMEM((tm, D_MODEL), BF16)],
        compiler_params=pltpu.CompilerParams(
            dimension_semantics=("parallel", "arbitrary"), vmem_limit_bytes=VMEM_LIMIT),
        name="in_proj",
    )(x2, w_a, w_b, w_dt_t)


def _rotary(a_ref, tab_ref, rows):
    a = a_ref[rows, :].astype(F32)
    up = pltpu.roll(a, LANES - ROT_DIM // 2, axis=1)
    dn = pltpu.roll(a, ROT_DIM // 2, axis=1)
    return (a * tab_ref[0, rows, :] + up * tab_ref[1, rows, :] + dn * tab_ref[2, rows, :]).astype(BF16)


def _attn_kernel(q_ref, k_ref, v_ref, tq_ref, tk_ref, lam_ref, sw_ref, o_ref, kr_ref, v1_ref, s1_ref, s2_ref, *, tk):
    tq = q_ref.shape[0]
    seq = k_ref.shape[0]
    n_kc = seq // tk

    @pl.when(pl.program_id(2) == 0)
    def _():
        v1_ref[:, :LANES] = v_ref[...]
        v1_ref[:, LANES:] = jnp.ones((seq, LANES), BF16)
        for c in range(n_kc):
            rows = slice(c * tk, (c + 1) * tk)
            kr_ref[rows, :] = _rotary(k_ref, tk_ref, rows)

    q = _rotary(q_ref, tq_ref, slice(None))
    lane = lax.broadcasted_iota(jnp.int32, q.shape, 1)
    zero = jnp.zeros_like(q)
    q1 = jnp.where(lane < HEAD_DIM_A, q, zero)
    q2 = jnp.where(lane >= HEAD_DIM_A, q, zero)

    m1 = jnp.full((tq, LANES), -jnp.inf, F32)
    m2 = m1
    for c in range(n_kc):
        kc = kr_ref[c * tk:(c + 1) * tk, :]
        sc1 = _dot_nt(q1, kc)
        sc2 = _dot_nt(q2, kc)
        s1_ref[c] = sc1
        s2_ref[c] = sc2
        for u in range(tk // LANES):
            m1 = jnp.maximum(m1, sc1[:, u * LANES:(u + 1) * LANES])
            m2 = jnp.maximum(m2, sc2[:, u * LANES:(u + 1) * LANES])
    m1 = jnp.max(m1, axis=1, keepdims=True)
    m2 = jnp.max(m2, axis=1, keepdims=True)

    a1 = jnp.zeros((tq, 2 * LANES), F32)
    a2 = a1
    for c in range(n_kc):
        vc = v1_ref[c * tk:(c + 1) * tk, :]
        p1 = jnp.exp2(s1_ref[c] - m1).astype(BF16)
        p2 = jnp.exp2(s2_ref[c] - m2).astype(BF16)
        a1 = a1 + _dot(p1, vc)
        a2 = a2 + _dot(p2, vc)

    lp = lam_ref[...]
    lam = (jnp.exp(jnp.sum(lp[0:1] * lp[1:2], axis=1, keepdims=True))
           - jnp.exp(jnp.sum(lp[2:3] * lp[3:4], axis=1, keepdims=True)) + LAM_INIT)
    o =a1[:, :LANES] / a1[:, LANES:] - lam * (a2[:, :LANES] / a2[:, LANES:])
    ms = jnp.mean(o * o, axis=1, keepdims=True)
    o = o * lax.rsqrt(ms + RMS_EPS) * sw_ref[...] * (1.0 - LAM_INIT)
    o_ref[...] = o.astype(o_ref.dtype)


def _attention(proj3, rot_tab, lam_p, subln_w, *, tq=1024, tk=512):
    batch, seq, _ = proj3.shape
    hw = 2 * HEAD_DIM_A
    kern = functools.partial(_attn_kernel, tk=tk)
    return pl.pallas_call(
        kern,
        grid=(batch, N_HEADS_A, seq // tq),
        in_specs=[
            pl.BlockSpec((None, tq, hw), lambda b, h, i: (b, i, Q_OFF // hw + h)),
            pl.BlockSpec((None, seq, hw), lambda b, h, i: (b, 0, K_OFF // hw + h)),
            pl.BlockSpec((None, seq, hw), lambda b, h, i: (b, 0, V_OFF // hw + h)),
            pl.BlockSpec((3, tq, hw), lambda b, h, i: (0, i, 0)),
            pl.BlockSpec((3, seq, hw), lambda b, h, i: (0, 0, 0), pipeline_mode=pl.Buffered(1)),
            pl.BlockSpec((4, HEAD_DIM_A), lambda b, h, i: (0, 0)),
            pl.BlockSpec((1, hw), lambda b, h, i: (0, 0)),
        ],
        out_specs=pl.BlockSpec((None, tq, hw), lambda b, h, i: (b, i, h)),
        out_shape=jax.ShapeDtypeStruct((batch, seq, ATTN_W), BF16),
        scratch_shapes=[
            pltpu.VMEM((seq, hw), BF16),
            pltpu.VMEM((seq, 2 * LANES), BF16),
            pltpu.VMEM((seq // tk, tq, tk), F32),
            pltpu.VMEM((seq // tk, tq, tk), F32),
        ],
        compiler_params=pltpu.CompilerParams(
            dimension_semantics=("parallel", "parallel", "arbitrary"), vmem_limit_bytes=VMEM_LIMIT),
        name="diff_attn",
    )(proj3, proj3, proj3, rot_tab, rot_tab, lam_p, subln_w)


def _silu(v):
    return v * jax.nn.sigmoid(v)


def _ssd_kernel(z_ref, xs_ref, b_ref, c_ref, dt_ref, cwx_ref, cwb_ref, cwc_ref, cbx_ref, cbb_ref, cbc_ref,
                dtb_ref, alog_ref, dsk_ref, nw_ref, o_ref,
                pad_ref, cv_ref, y_ref, cb_ref, src_ref, cs_ref, st_ref):
    seq = xs_ref.shape[0]
    n_chunks = seq // CHUNK
    halo = SUBLANES
    half = CONV_SSD // 2
    gw = GROUP_W
    w_all = gw + 2 * D_STATE

    n_slab = w_all // LANES
    srcs = (xs_ref, xs_ref, b_ref, c_ref)
    lane0 = (0, LANES, 0, 0)
    zero_halo = jnp.zeros((halo, LANES), F32)
    for s in range(n_slab):
        pad_ref[s, 0:halo, :] = zero_halo
        pad_ref[s, halo + seq:, :] = zero_halo
        pad_ref[s, halo:halo + seq, :] = srcs[s][:, lane0[s]:lane0[s] + LANES].astype(F32)
    cw = jnp.concatenate([cwx_ref[...], cwb_ref[...], cwc_ref[...]], axis=1)
    cbias = jnp.concatenate([cbx_ref[...], cbb_ref[...], cbc_ref[...]], axis=1)
    stride = CONV_STRIDE
    blk = SUBLANES * stride

    def conv_chunk(c, carry):
        t0 = pl.multiple_of(c * CHUNK, CHUNK)
        for s in range(n_slab):
            w_s = cw[:, s * LANES:(s + 1) * LANES]
            b_s = jnp.broadcast_to(cbias[:, s * LANES:(s + 1) * LANES], (SUBLANES, LANES))
            for r in range(0, CHUNK, blk):
                for v in range(stride):
                    base = t0 + (r + v)
                    acc = b_s
                    for k in range(CONV_SSD):
                        acc = acc + pad_ref[s, pl.ds(base + (halo - half + k), SUBLANES, stride=stride), :] * w_s[k:k + 1, :]
                    cv_ref[s, pl.ds(base, SUBLANES, stride=stride), :] = _silu(acc)
        rows = pl.ds(t0, CHUNK)
        cb_ref[c] = _dot_nt(cv_ref[3, rows, :].astype(BF16), cv_ref[2, rows, :].astype(BF16))
        return carry

    lax.fori_loop(0, n_chunks, conv_chunk, 0, unroll=4)

    nh = 2 * HEADS_PER_GROUP
    r_i = lax.broadcasted_iota(jnp.int32, (CHUNK, CHUNK), 0)
    c_i = lax.broadcasted_iota(jnp.int32, (CHUNK, CHUNK), 1)
    upper = (r_i <= c_i).astype(F32)
    lower = (r_i >= c_i).astype(F32)
    raw = dt_ref[...].reshape(n_chunks * nh, CHUNK) + dtb_ref[...]
    dt_all = jnp.maximum(raw, 0.0) + jnp.log1p(jnp.exp(-jnp.abs(raw)))
    da = dt_all * (-jnp.exp(alog_ref[...]))
    row_all = lax.broadcasted_iota(jnp.int32, (n_chunks * nh, CHUNK), 0)
    cs_all = jnp.where(row_all % nh < HEADS_PER_GROUP, _dot_exact(da, upper), _dot_exact(da, lower))
    cs2_all = cs_all * LOG2_E
    cs_ref[...] = cs2_all.reshape(n_chunks, nh, CHUNK)
    src_ref[...] = (cs2_all - jnp.log2(dt_all)).reshape(n_chunks, nh, CHUNK)

    lane_head = lax.broadcasted_iota(jnp.int32, (1, gw), 1) // HEAD_DIM_S
    first_head = lax.broadcasted_iota(jnp.int32, (1, LANES), 1) < HEAD_DIM_S

    def chunk_step(c, reverse):
        r0 = HEADS_PER_GROUP if reverse else 0
        t0 = pl.multiple_of(c * CHUNK, CHUNK)
        rows = pl.ds(t0, CHUNK)
        x = jnp.concatenate([cv_ref[0, rows, :], cv_ref[1, rows, :]], axis=1)
        bm = cv_ref[2, rows, :]
        cm = cv_ref[3, rows, :].astype(BF16)
        cs2 = cs_ref[c]
        src = src_ref[c]
        cbm = cb_ref[c]
        if reverse:
            tot = cs2[:, 0:1]
            keep = r_i <= c_i
        else:
            tot = cs2[:, CHUNK - 1:CHUNK]
            keep = r_i >= c_i
        cs2_t = jnp.transpose(cs2)
        col_b = [jnp.broadcast_to(cs2_t[:, r0 + e:r0 + e + 1], (CHUNK, LANES)) for e in range(HEADS_PER_GROUP)]
        g_out = jnp.exp2(jnp.concatenate(
            [jnp.where(first_head, col_b[2 * v], col_b[2 * v + 1]) for v in range(gw // LANES)], axis=1))
        sel_row = (lax.broadcasted_iota(jnp.int32, (nh, gw), 0) == lane_head + r0).astype(F32)
        cdec = jnp.exp2(jnp.sum(tot * sel_row, axis=0, keepdims=True))
        wrow = jnp.exp2(tot - src)
        bt = jnp.transpose(bm)
        xb = x.astype(BF16)
        zero = jnp.zeros_like(xb)
        ms, bs, xs = [], [], []
        for e in range(HEADS_PER_GROUP):
            r = r0 + e
            seg = col_b[e] - src[r:r + 1, :]
            decay = jnp.exp2(jnp.where(keep, seg, -jnp.inf))
            ms.append((cbm * decay).astype(BF16))
            bs.append((bt * wrow[r:r + 1, :]).astype(BF16))
            xs.append(jnp.where(lane_head == e, xb, zero))
        xcat = jnp.concatenate(xs, axis=0)
        d = 1 if reverse else 0
        st = st_ref[d]
        y = _dot(jnp.concatenate(ms, axis=1), xcat) + _dot(cm, st.astype(BF16)) * g_out
        st_ref[d] = st * cdec + _dot(jnp.concatenate(bs, axis=1), xcat)
        y_ref[d, pl.ds(t0, CHUNK), :] = y

    st_ref[...] = jnp.zeros_like(st_ref)

    def scan(i, carry):
        chunk_step(i, False)
        chunk_step(n_chunks - 1 - i, True)
        return carry

    lax.fori_loop(0, n_chunks, scan, 0, unroll=8)

    def finish(c, carry):
        t0 = pl.multiple_of(c * CHUNK, CHUNK)
        rows = pl.ds(t0, CHUNK)
        xc = jnp.concatenate([cv_ref[0, rows, :], cv_ref[1, rows, :]], axis=1)
        y = y_ref[0, rows, :] + y_ref[1, rows, :] + xc * dsk_ref[...]
        y = y * _silu(z_ref[rows, :].astype(F32))
        ms = jnp.mean(y * y, axis=1, keepdims=True)
        y = y * lax.rsqrt(ms + RMS_EPS) * nw_ref[...]
        o_ref[rows, :] = y.astype(o_ref.dtype)
        return carry

    lax.fori_loop(0, n_chunks, finish, 0, unroll=4)


def _ssd(proj3, dt_t, cw_x, cw_b, cw_c, cb_x, cb_b, cb_c, dtb, alog, dsk, nw):
    batch, seq, _ = proj3.shape
    gw, ns = GROUP_W, D_STATE
    nh = 2 * HEADS_PER_GROUP
    n_chunks = seq // CHUNK
    gspec = lambda w: pl.BlockSpec((1, w), lambda b, g: (0, g))
    return pl.pallas_call(
        _ssd_kernel,
        grid=(batch, N_GROUPS),
        in_specs=[
            pl.BlockSpec((None, seq, gw), lambda b, g: (b, 0, Z_OFF // gw + g)),
            pl.BlockSpec((None, seq, gw), lambda b, g: (b, 0, XS_OFF // gw + g)),
            pl.BlockSpec((None, seq, ns), lambda b, g: (b, 0, B_OFF // ns + g)),
            pl.BlockSpec((None, seq, ns), lambda b, g: (b, 0, C_OFF // ns + g)),
            pl.BlockSpec((None, n_chunks, nh, CHUNK), lambda b, g: (b, 0, g, 0)),
            pl.BlockSpec((CONV_SSD, gw), lambda b, g: (0, g)),
            pl.BlockSpec((CONV_SSD, ns), lambda b, g: (0, g)),
            pl.BlockSpec((CONV_SSD, ns), lambda b, g: (0, g)),
            gspec(gw), gspec(ns), gspec(ns),
            pl.BlockSpec((n_chunks * nh, 1), lambda b, g: (g, 0)),
            pl.BlockSpec((n_chunks * nh, 1), lambda b, g: (g, 0)),
            gspec(gw), gspec(gw),
        ],
        out_specs=pl.BlockSpec((None, seq, gw), lambda b, g: (b, 0, g)),
        out_shape=jax.ShapeDtypeStruct((batch, seq, D_INNER), BF16),
        scratch_shapes=[
            pltpu.VMEM(((gw + 2 * ns) // LANES, seq + 2 * SUBLANES, LANES), F32),
            pltpu.VMEM(((gw + 2 * ns) // LANES, seq, LANES), F32),
            pltpu.VMEM((2, seq, gw), F32),
            pltpu.VMEM((n_chunks, CHUNK, CHUNK), F32),
            pltpu.VMEM((n_chunks, nh, CHUNK), F32),
            pltpu.VMEM((n_chunks, nh, CHUNK), F32),
            pltpu.VMEM((2, ns, gw), F32),
        ],
        compiler_params=pltpu.CompilerParams(
            dimension_semantics=("parallel", "parallel"), vmem_limit_bytes=VMEM_LIMIT),
        name="ssd",
    )(proj3, proj3, proj3, proj3, dt_t, cw_x, cw_b, cw_c, cb_x, cb_b, cb_c, dtb, alog, dsk, nw)


def _layer_norm(v, g, b):
    mu = jnp.mean(v, axis=1, keepdims=True)
    d = v - mu
    var = jnp.mean(d * d, axis=1, keepdims=True)
    return d * lax.rsqrt(var + LN_EPS) * g + b


def _merge_kernel(x_ref, at_ref, sd_ref, ga_ref, gs_ref, wpa_ref, wps_ref, wo_ref, bg_ref, g_ref, b_ref, o_ref):
    bg = bg_ref[...]
    g_a = jax.nn.sigmoid(ga_ref[...].astype(F32) + bg[:, :D_MODEL])
    g_s = jax.nn.sigmoid(gs_ref[...].astype(F32) + bg[:, D_MODEL:])
    merged = g_a * _dot(at_ref[...], wpa_ref[...]) + g_s * _dot(sd_ref[...], wps_ref[...])
    y = ALPHA * x_ref[...] + _dot(merged.astype(BF16), wo_ref[...])
    o_ref[...] = _layer_norm(y, g_ref[...], b_ref[...])


def _merge(x2, attn2, ssd2, proj2, wpa, wps, wo, bg, ln_g, ln_b, *, tm=512):
    t = x2.shape[0]
    full = lambda a: pl.BlockSpec(a.shape, lambda i: (0, 0))
    return pl.pallas_call(
        _merge_kernel,
        grid=(t // tm,),
        in_specs=[
            pl.BlockSpec((tm, D_MODEL), lambda i: (i, 0)),
            pl.BlockSpec((tm, ATTN_W), lambda i: (i, 0)),
            pl.BlockSpec((tm, D_INNER), lambda i: (i, 0)),
            pl.BlockSpec((tm, D_MODEL), lambda i: (i, GATE_OFF // D_MODEL)),
            pl.BlockSpec((tm, D_MODEL), lambda i: (i, GATE_OFF // D_MODEL + 1)),
            full(wpa), full(wps), full(wo), full(bg), full(ln_g), full(ln_b),
        ],
        out_specs=pl.BlockSpec((tm, D_MODEL), lambda i: (i, 0)),
        out_shape=jax.ShapeDtypeStruct((t, D_MODEL), F32),
        compiler_params=pltpu.CompilerParams(
            dimension_semantics=("parallel",), vmem_limit_bytes=VMEM_LIMIT),
        name="merge_ln1",
    )(x2, attn2, ssd2, proj2, proj2, wpa, wps, wo, bg, ln_g, ln_b)


def _ffn_kernel(x_ref, prev_ref, next_ref, wup_ref, cw_ref, cb_ref, wdn_ref, g_ref, b_ref, o_ref, xh_ref,
                ug0_ref, uu0_ref, ug1_ref, uu1_ref, *, tiles_per_seq, tf):
    i = pl.program_id(0)
    tm = x_ref.shape[0]
    halo = SUBLANES
    x = x_ref[...]
    first = (i % tiles_per_seq) == 0
    last = (i % tiles_per_seq) == tiles_per_seq - 1
    zero = jnp.zeros((halo, D_MODEL), F32)
    xh_ref[0:halo, :] = jnp.where(first, zero, prev_ref[...]).astype(BF16)
    xh_ref[halo:halo + tm, :] = x.astype(BF16)
    xh_ref[halo + tm:, :] = jnp.where(last, zero, next_ref[...]).astype(BF16)
    xh = xh_ref[...]

    def conv(u_ref, col):
        w = cw_ref[:, col:col + tf]
        out = cb_ref[:, col:col + tf] + u_ref[halo - 1:halo - 1 + tm, :] * w[0:1]
        out = out + u_ref[halo:halo + tm, :] * w[1:2]
        return out + u_ref[halo + 1:halo + 1 + tm, :] * w[2:3]

    bufs = ((ug0_ref, uu0_ref), (ug1_ref, uu1_ref))

    def up(j):
        ug_ref, uu_ref = bufs[j % 2]
        ug_ref[...] = _dot(xh, wup_ref[:, j * tf:(j + 1) * tf])
        uu_ref[...] = _dot(xh, wup_ref[:, D_FF + j * tf:D_FF + (j + 1) * tf])

    n_f = D_FF // tf
    acc = jnp.zeros((tm, D_MODEL), F32)
    up(0)
    for j in range(n_f):
        if j + 1 < n_f:
            up(j + 1)
        ug_ref, uu_ref = bufs[j % 2]
        act = (_silu(conv(ug_ref, j * tf)) * conv(uu_ref, D_FF + j * tf)).astype(BF16)
        acc = acc + _dot(act, wdn_ref[j * tf:(j + 1) * tf, :])
    o_ref[...] = _layer_norm(ALPHA * x + acc, g_ref[...], b_ref[...])


def _ffn(x1, wup, cw, cb, wdn, ln_g, ln_b, *, seq, tm=512, tf=256):
    t = x1.shape[0]
    hb = tm // SUBLANES
    n_hb = t // SUBLANES
    full = lambda a: pl.BlockSpec(a.shape, lambda i: (0, 0))
    kern = functools.partial(_ffn_kernel, tiles_per_seq=seq // tm, tf=tf)
    return pl.pallas_call(
        kern,
        grid=(t // tm,),
        in_specs=[
            pl.BlockSpec((tm, D_MODEL), lambda i: (i, 0)),
            pl.BlockSpec((SUBLANES, D_MODEL), lambda i: (jnp.maximum(i * hb - 1, 0), 0)),
            pl.BlockSpec((SUBLANES, D_MODEL), lambda i: (jnp.minimum((i + 1) * hb, n_hb - 1), 0)),
            full(wup), full(cw), full(cb), full(wdn), full(ln_g), full(ln_b),
        ],
        out_specs=pl.BlockSpec((tm, D_MODEL), lambda i: (i, 0)),
        out_shape=jax.ShapeDtypeStruct((t, D_MODEL), F32),
        scratch_shapes=([pltpu.VMEM((tm + 2 * SUBLANES, D_MODEL), BF16)]
                        + [pltpu.VMEM((tm + 2 * SUBLANES, tf), F32)] * 4),
        compiler_params=pltpu.CompilerParams(
            dimension_semantics=("parallel",), vmem_limit_bytes=VMEM_LIMIT),
        name="ffn_ln2",
    )(x1, x1, x1, wup, cw, cb, wdn, ln_g, ln_b)


def _rotary_tables(seq):
    half = ROT_DIM // 2
    pos = np.arange(seq, dtype=np.float32)
    inv_freq = np.power(np.float32(ROPE_THETA), -np.arange(0, ROT_DIM, 2, dtype=np.float32) / np.float32(ROT_DIM))
    ang = (pos[:, None] * inv_freq[None, :].astype(np.float32)).astype(np.float32)
    cos, sin = np.cos(ang).astype(np.float32), np.sin(ang).astype(np.float32)
    pad = np.zeros((seq, HEAD_DIM_A - ROT_DIM), np.float32)
    zer = np.zeros((seq, half), np.float32)
    c64 = np.concatenate([cos, cos, pad + 1.0], axis=1)
    sa64 = np.concatenate([-sin, zer, pad], axis=1)
    sb64 = np.concatenate([zer, sin, pad], axis=1)
    tile2 = lambda a: np.concatenate([a, a], axis=1)
    return jnp.asarray(np.stack([tile2(c64), tile2(sa64), tile2(sb64)]))


def _group_rows(p, n_chunks):
    nh = 2 * HEADS_PER_GROUP
    pg = p.reshape(2, N_GROUPS, HEADS_PER_GROUP).transpose(1, 0, 2).reshape(N_GROUPS, 1, nh)
    return jnp.broadcast_to(pg, (N_GROUPS, n_chunks, nh)).reshape(N_GROUPS * n_chunks * nh, 1)


def kernel(x, w_in, b_gate, lambda_q1, lambda_k1, lambda_q2, lambda_k2, attn_subln_w, conv_ssd_w, conv_ssd_b,
           dt_bias, a_log, d_skip, ssd_norm_w, w_proj_attn, w_proj_ssd, w_out, ln1_g, ln1_b, w_up, conv_ffn_w,
           conv_ffn_b, w_down, ln2_g, ln2_b):
    batch, seq, _ = x.shape
    t = batch * seq
    l = 0
    xbc_w = D_INNER + 2 * N_GROUPS * D_STATE
    dt_off = Z_OFF + D_INNER + xbc_w
    w = w_in[l]
    w_a = w.astype(BF16)
    w_b = w[:, dt_off + 2 * N_HEADS_S:].astype(BF16)
    w_dt = w[:, dt_off:dt_off + 2 * N_HEADS_S]
    w_dt_t = (w_dt.reshape(D_MODEL, 2, N_GROUPS, HEADS_PER_GROUP).transpose(2, 1, 3, 0)
              .reshape(2 * N_HEADS_S, D_MODEL).astype(BF16))
    rot_tab = _rotary_tables(seq)

    x2 = x.reshape(t, D_MODEL)
    proj2, dt_t = _in_proj(x2, w_a, w_b, w_dt_t, batch=batch, seq=seq)
    proj3 = proj2.reshape(batch, seq, PROJ_W)

    lam_p = jnp.stack([lambda_q1[l], lambda_k1[l], lambda_q2[l], lambda_k2[l]]).astype(F32)
    attn = _attention(proj3, rot_tab, lam_p, attn_subln_w[l].reshape(1, 2 * HEAD_DIM_A))

    cw = conv_ssd_w[l]
    cb = conv_ssd_b[l].reshape(1, xbc_w)
    nb = N_GROUPS * D_STATE
    ssd = _ssd(proj3, dt_t,
               cw[:, :D_INNER], cw[:, D_INNER:D_INNER + nb], cw[:, D_INNER + nb:],
               cb[:, :D_INNER], cb[:, D_INNER:D_INNER + nb], cb[:, D_INNER + nb:],
               _group_rows(dt_bias[l], seq // CHUNK), _group_rows(a_log[l], seq // CHUNK),
               jnp.repeat(d_skip[l], HEAD_DIM_S).reshape(1, D_INNER), ssd_norm_w[l].reshape(1, D_INNER))

    x1 = _merge(x2, attn.reshape(t, ATTN_W), ssd.reshape(t, D_INNER), proj2,
                w_proj_attn[l].astype(BF16), w_proj_ssd[l].astype(BF16), w_out[l].astype(BF16),
                b_gate[l].reshape(1, 2 * D_MODEL), ln1_g[l].reshape(1, D_MODEL), ln1_b[l].reshape(1, D_MODEL))

    out = _ffn(x1, w_up[l].astype(BF16), conv_ffn_w[l], conv_ffn_b[l].reshape(1, 2 * D_FF),
               w_down[l].astype(BF16), ln2_g[l].reshape(1, D_MODEL), ln2_b[l].reshape(1, D_MODEL), seq=seq)
    return out.reshape(batch, seq, D_MODEL)
```

```python
import functools
import math

import jax
import jax.numpy as jnp
import numpy as np
from jax import lax
from jax.experimental import pallas as pl
from jax.experimental.pallas import tpu as pltpu

F32 = jnp.float32
BF16 = jnp.bfloat16

D_MODEL = 1024
N_HEADS_A = 8
HEAD_DIM_A = 64
ROT_DIM = HEAD_DIM_A // 4
ROPE_THETA = 500000.0
ATTN_W = N_HEADS_A * 2 * HEAD_DIM_A
D_INNER = 2 * D_MODEL
HEAD_DIM_S = 64
N_HEADS_S = D_INNER // HEAD_DIM_S
N_GROUPS = 8
HEADS_PER_GROUP = N_HEADS_S // N_GROUPS
GROUP_W = HEADS_PER_GROUP * HEAD_DIM_S
D_STATE = 128
CONV_SSD = 5
CHUNK = 128
D_FF = 2816
DEPTH = 1
ALPHA = (2 * DEPTH) ** 0.25
LN_EPS = 1e-5
RMS_EPS = 1e-5
LAM_INIT = 0.8 - 0.6 * math.exp(-0.3 * 0)
LOG2_E = math.log2(math.e)

LANES = 128
SUBLANES = 8
CONV_STRIDE = 4
VMEM_LIMIT = 56 * 1024 * 1024

Q_OFF, K_OFF, V_OFF = 0, ATTN_W, 2 * ATTN_W
Z_OFF = 3 * ATTN_W
XS_OFF = Z_OFF + D_INNER
B_OFF = XS_OFF + D_INNER
C_OFF = B_OFF + N_GROUPS * D_STATE
GATE_OFF = C_OFF + N_GROUPS * D_STATE
PROJ_W = GATE_OFF + 2 * D_MODEL


def _dot(a, b):
    return jnp.dot(a, b, preferred_element_type=F32)


def _dot_nt(a, b):
    return lax.dot_general(a, b, (((1,), (1,)), ((), ())), preferred_element_type=F32)


def _dot_exact(a, b):
    return jnp.dot(a, b, preferred_element_type=F32, precision=lax.Precision.HIGHEST)


def _in_proj_kernel(x_ref, wa_ref, wb_ref, wdt_ref, o_ref, dt_ref, xb_ref, *, n_q, n_a):
    j = pl.program_id(1)

    @pl.when(j == 0)
    def _():
        xb = x_ref[...].astype(BF16)
        xb_ref[...] = xb
        dt_t = _dot_nt(wdt_ref[...], xb)
        for c in range(dt_ref.shape[0]):
            dt_ref[c] = dt_t[:, c * CHUNK:(c + 1) * CHUNK]

    @pl.when(j < n_q)
    def _():
        o_ref[...] = (_dot(xb_ref[...], wa_ref[...]) * (HEAD_DIM_A ** -0.5 * LOG2_E)).astype(o_ref.dtype)

    @pl.when((j >= n_q) & (j < n_a))
    def _():
        o_ref[...] = _dot(xb_ref[...], wa_ref[...]).astype(o_ref.dtype)

    @pl.when(j >= n_a)
    def _():
        o_ref[...] = _dot(xb_ref[...], wb_ref[...]).astype(o_ref.dtype)


def _in_proj(x2, w_a, w_b, w_dt_t, *, batch, seq, tm=2048, tn=1024):
    t = x2.shape[0]
    n_seq_tiles = seq // tm
    n_dt = w_dt_t.shape[0]
    n_a = GATE_OFF // tn
    kern = functools.partial(_in_proj_kernel, n_q=ATTN_W // tn, n_a=n_a)
    return pl.pallas_call(
        kern,
        grid=(t // tm, PROJ_W // tn),
        in_specs=[
            pl.BlockSpec((tm, D_MODEL), lambda i, j: (i, 0)),
            pl.BlockSpec((D_MODEL, tn), lambda i, j: (0, jnp.minimum(j, n_a - 1))),
            pl.BlockSpec((D_MODEL, tn), lambda i, j: (0, jnp.maximum(j - n_a, 0))),
            pl.BlockSpec((n_dt, D_MODEL), lambda i, j: (0, 0)),
        ],
        out_specs=[
            pl.BlockSpec((tm, tn), lambda i, j: (i, j)),
            pl.BlockSpec((None, tm // CHUNK, n_dt, CHUNK),
                         lambda i, j: (i // n_seq_tiles, i % n_seq_tiles, 0, 0)),
        ],
        out_shape=[
            jax.ShapeDtypeStruct((t, PROJ_W), BF16),
            jax.ShapeDtypeStruct((batch, seq // CHUNK, n_dt, CHUNK), F32),
        ],
        scratch_shapes=[pltpu.VMEM((tm, D_MODEL), BF16)],
        compiler_params=pltpu.CompilerParams(
            dimension_semantics=("parallel", "arbitrary"), vmem_limit_bytes=VMEM_LIMIT),
        name="in_proj",
    )(x2, w_a, w_b, w_dt_t)


def _rotary(a_ref, tab_ref, rows):
    a = a_ref[rows, :].astype(F32)
    up = pltpu.roll(a, LANES - ROT_DIM // 2, axis=1)
    dn = pltpu.roll(a, ROT_DIM // 2, axis=1)
    return (a * tab_ref[0, rows, :] + up * tab_ref[1, rows, :] + dn * tab_ref[2, rows, :]).astype(BF16)


def _attn_kernel(q_ref, k_ref, v_ref, tq_ref, tk_ref, lam_ref, sw_ref, o_ref, kr_ref, v1_ref, s1_ref, s2_ref, *, tk):
    tq = q_ref.shape[0]
    seq = k_ref.shape[0]
    n_kc = seq // tk

    @pl.when(pl.program_id(2) == 0)
    def _():
        v1_ref[:, :LANES] = v_ref[...]
        v1_ref[:, LANES:] = jnp.ones((seq, LANES), BF16)
        for c in range(n_kc):
            rows = slice(c * tk, (c + 1) * tk)
            kr_ref[rows, :] = _rotary(k_ref, tk_ref, rows)

    q = _rotary(q_ref, tq_ref, slice(None))
    lane = lax.broadcasted_iota(jnp.int32, q.shape, 1)
    zero = jnp.zeros_like(q)
    q1 = jnp.where(lane < HEAD_DIM_A, q, zero)
    q2 = jnp.where(lane >= HEAD_DIM_A, q, zero)

    m1 = jnp.full((tq, LANES), -jnp.inf, F32)
    m2 = m1
    for c in range(n_kc):
        kc = kr_ref[c * tk:(c + 1) * tk, :]
        sc1 = _dot_nt(q1, kc)
        sc2 = _dot_nt(q2, kc)
        s1_ref[c] = sc1
        s2_ref[c] = sc2
        for u in range(tk // LANES):
            m1 = jnp.maximum(m1, sc1[:, u * LANES:(u + 1) * LANES])
            m2 = jnp.maximum(m2, sc2[:, u * LANES:(u + 1) * LANES])
    m1 = jnp.max(m1, axis=1, keepdims=True)
    m2 = jnp.max(m2, axis=1, keepdims=True)

    a1 = jnp.zeros((tq, 2 * LANES), F32)
    a2 = a1
    for c in range(n_kc):
        vc = v1_ref[c * tk:(c + 1) * tk, :]
        p1 = jnp.exp2(s1_ref[c] - m1).astype(BF16)
        p2 = jnp.exp2(s2_ref[c] - m2).astype(BF16)
        a1 = a1 + _dot(p1, vc)
        a2 = a2 + _dot(p2, vc)

    lp = lam_ref[...]
    lam = (jnp.exp(jnp.sum(lp[0:1] * lp[1:2], axis=1, keepdims=True))
           - jnp.exp(jnp.sum(lp[2:3] * lp[3:4], axis=1, keepdims=True)) + LAM_INIT)
    o =a1[:, :LANES] / a1[:, LANES:] - lam * (a2[:, :LANES] / a2[:, LANES:])
    ms = jnp.mean(o * o, axis=1, keepdims=True)
    o = o * lax.rsqrt(ms + RMS_EPS) * sw_ref[...] * (1.0 - LAM_INIT)
    o_ref[...] = o.astype(o_ref.dtype)


def _attention(proj3, rot_tab, lam_p, subln_w, *, tq=1024, tk=512):
    batch, seq, _ = proj3.shape
    hw = 2 * HEAD_DIM_A
    kern = functools.partial(_attn_kernel, tk=tk)
    return pl.pallas_call(
        kern,
        grid=(batch, N_HEADS_A, seq // tq),
        in_specs=[
            pl.BlockSpec((None, tq, hw), lambda b, h, i: (b, i, Q_OFF // hw + h)),
            pl.BlockSpec((None, seq, hw), lambda b, h, i: (b, 0, K_OFF // hw + h)),
            pl.BlockSpec((None, seq, hw), lambda b, h, i: (b, 0, V_OFF // hw + h)),
            pl.BlockSpec((3, tq, hw), lambda b, h, i: (0, i, 0)),
            pl.BlockSpec((3, seq, hw), lambda b, h, i: (0, 0, 0), pipeline_mode=pl.Buffered(1)),
            pl.BlockSpec((4, HEAD_DIM_A), lambda b, h, i: (0, 0)),
            pl.BlockSpec((1, hw), lambda b, h, i: (0, 0)),
        ],
        out_specs=pl.BlockSpec((None, tq, hw), lambda b, h, i: (b, i, h)),
        out_shape=jax.ShapeDtypeStruct((batch, seq, ATTN_W), BF16),
        scratch_shapes=[
            pltpu.VMEM((seq, hw), BF16),
            pltpu.VMEM((seq, 2 * LANES), BF16),
            pltpu.VMEM((seq // tk, tq, tk), F32),
            pltpu.VMEM((seq // tk, tq, tk), F32),
        ],
        compiler_params=pltpu.CompilerParams(
            dimension_semantics=("parallel", "parallel", "arbitrary"), vmem_limit_bytes=VMEM_LIMIT),
        name="diff_attn",
    )(proj3, proj3, proj3, rot_tab, rot_tab, lam_p, subln_w)


def _silu(v):
    return v * jax.nn.sigmoid(v)


def _ssd_kernel(z_ref, xs_ref, b_ref, c_ref, dt_ref, cwx_ref, cwb_ref, cwc_ref, cbx_ref, cbb_ref, cbc_ref,
                dtb_ref, alog_ref, dsk_ref, nw_ref, o_ref,
                pad_ref, cv_ref, y_ref, cb_ref, src_ref, cs_ref, st_ref):
    seq = xs_ref.shape[0]
    n_chunks = seq // CHUNK
    halo = SUBLANES
    half = CONV_SSD // 2
    gw = GROUP_W
    w_all = gw + 2 * D_STATE

    n_slab = w_all // LANES
    srcs = (xs_ref, xs_ref, b_ref, c_ref)
    lane0 = (0, LANES, 0, 0)
    zero_halo = jnp.zeros((halo, LANES), F32)
    for s in range(n_slab):
        pad_ref[s, 0:halo, :] = zero_halo
        pad_ref[s, halo + seq:, :] = zero_halo
        pad_ref[s, halo:halo + seq, :] = srcs[s][:, lane0[s]:lane0[s] + LANES].astype(F32)
    cw = jnp.concatenate([cwx_ref[...], cwb_ref[...], cwc_ref[...]], axis=1)
    cbias = jnp.concatenate([cbx_ref[...], cbb_ref[...], cbc_ref[...]], axis=1)
    stride = CONV_STRIDE
    blk = SUBLANES * stride

    def conv_chunk(c, carry):
        t0 = pl.multiple_of(c * CHUNK, CHUNK)
        for s in range(n_slab):
            w_s = cw[:, s * LANES:(s + 1) * LANES]
            b_s = jnp.broadcast_to(cbias[:, s * LANES:(s + 1) * LANES], (SUBLANES, LANES))
            for r in range(0, CHUNK, blk):
                for v in range(stride):
                    base = t0 + (r + v)
                    acc = b_s
                    for k in range(CONV_SSD):
                        acc = acc + pad_ref[s, pl.ds(base + (halo - half + k), SUBLANES, stride=stride), :] * w_s[k:k + 1, :]
                    cv_ref[s, pl.ds(base, SUBLANES, stride=stride), :] = _silu(acc)
        rows = pl.ds(t0, CHUNK)
        cb_ref[c] = _dot_nt(cv_ref[3, rows, :].astype(BF16), cv_ref[2, rows, :].astype(BF16))
        return carry

    lax.fori_loop(0, n_chunks, conv_chunk, 0, unroll=4)

    nh = 2 * HEADS_PER_GROUP
    r_i = lax.broadcasted_iota(jnp.int32, (CHUNK, CHUNK), 0)
    c_i = lax.broadcasted_iota(jnp.int32, (CHUNK, CHUNK), 1)
    upper = (r_i <= c_i).astype(F32)
    lower = (r_i >= c_i).astype(F32)
    raw = dt_ref[...].reshape(n_chunks * nh, CHUNK) + dtb_ref[...]
    dt_all = jnp.maximum(raw, 0.0) + jnp.log1p(jnp.exp(-jnp.abs(raw)))
    da = dt_all * (-jnp.exp(alog_ref[...]))
    row_all = lax.broadcasted_iota(jnp.int32, (n_chunks * nh, CHUNK), 0)
    cs_all = jnp.where(row_all % nh < HEADS_PER_GROUP, _dot_exact(da, upper), _dot_exact(da, lower))
    cs2_all = cs_all * LOG2_E
    cs_ref[...] = cs2_all.reshape(n_chunks, nh, CHUNK)
    src_ref[...] = (cs2_all - jnp.log2(dt_all)).reshape(n_chunks, nh, CHUNK)

    lane_head = lax.broadcasted_iota(jnp.int32, (1, gw), 1) // HEAD_DIM_S
    first_head = lax.broadcasted_iota(jnp.int32, (1, LANES), 1) < HEAD_DIM_S

    def chunk_step(c, reverse):
        r0 = HEADS_PER_GROUP if reverse else 0
        t0 = pl.multiple_of(c * CHUNK, CHUNK)
        rows = pl.ds(t0, CHUNK)
        x = jnp.concatenate([cv_ref[0, rows, :], cv_ref[1, rows, :]], axis=1)
        bm = cv_ref[2, rows, :]
        cm = cv_ref[3, rows, :].astype(BF16)
        cs2 = cs_ref[c]
        src = src_ref[c]
        cbm = cb_ref[c]
        if reverse:
            tot = cs2[:, 0:1]
            keep = r_i <= c_i
        else:
            tot = cs2[:, CHUNK - 1:CHUNK]
            keep = r_i >= c_i
        cs2_t = jnp.transpose(cs2)
        col_b = [jnp.broadcast_to(cs2_t[:, r0 + e:r0 + e + 1], (CHUNK, LANES)) for e in range(HEADS_PER_GROUP)]
        g_out = jnp.exp2(jnp.concatenate(
            [jnp.where(first_head, col_b[2 * v], col_b[2 * v + 1]) for v in range(gw // LANES)], axis=1))
        sel_row = (lax.broadcasted_iota(jnp.int32, (nh, gw), 0) == lane_head + r0).astype(F32)
        cdec = jnp.exp2(jnp.sum(tot * sel_row, axis=0, keepdims=True))
        wrow = jnp.exp2(tot - src)
        bt = jnp.transpose(bm)
        xb = x.astype(BF16)
        zero = jnp.zeros_like(xb)
        ms, bs, xs = [], [], []
        for e in range(HEADS_PER_GROUP):
            r = r0 + e
            seg = col_b[e] - src[r:r + 1, :]
            decay = jnp.exp2(jnp.where(keep, seg, -jnp.inf))
            ms.append((cbm * decay).astype(BF16))
            bs.append((bt * wrow[r:r + 1, :]).astype(BF16))
            xs.append(jnp.where(lane_head == e, xb, zero))
        xcat = jnp.concatenate(xs, axis=0)
        d = 1 if reverse else 0
        st = st_ref[d]
        y = _dot(jnp.concatenate(ms, axis=1), xcat) + _dot(cm, st.astype(BF16)) * g_out
        st_ref[d] = st * cdec + _dot(jnp.concatenate(bs, axis=1), xcat)
        y_ref[d, pl.ds(t0, CHUNK), :] = y

    st_ref[...] = jnp.zeros_like(st_ref)

    def scan(i, carry):
        chunk_step(i, False)
        chunk_step(n_chunks - 1 - i, True)
        return carry

    lax.fori_loop(0, n_chunks, scan, 0, unroll=8)

    def finish(c, carry):
        t0 = pl.multiple_of(c * CHUNK, CHUNK)
        rows = pl.ds(t0, CHUNK)
        xc = jnp.concatenate([cv_ref[0, rows, :], cv_ref[1, rows, :]], axis=1)
        y = y_ref[0, rows, :] + y_ref[1, rows, :] + xc * dsk_ref[...]
        y = y * _silu(z_ref[rows, :].astype(F32))
        ms = jnp.mean(y * y, axis=1, keepdims=True)
        y = y * lax.rsqrt(ms + RMS_EPS) * nw_ref[...]
        o_ref[rows, :] = y.astype(o_ref.dtype)
        return carry

    lax.fori_loop(0, n_chunks, finish, 0, unroll=4)


def _ssd(proj3, dt_t, cw_x, cw_b, cw_c, cb_x, cb_b, cb_c, dtb, alog, dsk, nw):
    batch, seq, _ = proj3.shape
    gw, ns = GROUP_W, D_STATE
    nh = 2 * HEADS_PER_GROUP
    n_chunks = seq // CHUNK
    gspec = lambda w: pl.BlockSpec((1, w), lambda b, g: (0, g))
    return pl.pallas_call(
        _ssd_kernel,
        grid=(batch, N_GROUPS),
        in_specs=[
            pl.BlockSpec((None, seq, gw), lambda b, g: (b, 0, Z_OFF // gw + g)),
            pl.BlockSpec((None, seq, gw), lambda b, g: (b, 0, XS_OFF // gw + g)),
            pl.BlockSpec((None, seq, ns), lambda b, g: (b, 0, B_OFF // ns + g)),
            pl.BlockSpec((None, seq, ns), lambda b, g: (b, 0, C_OFF // ns + g)),
            pl.BlockSpec((None, n_chunks, nh, CHUNK), lambda b, g: (b, 0, g, 0)),
            pl.BlockSpec((CONV_SSD, gw), lambda b, g: (0, g)),
            pl.BlockSpec((CONV_SSD, ns), lambda b, g: (0, g)),
            pl.BlockSpec((CONV_SSD, ns), lambda b, g: (0, g)),
            gspec(gw), gspec(ns), gspec(ns),
            pl.BlockSpec((n_chunks * nh, 1), lambda b, g: (g, 0)),
            pl.BlockSpec((n_chunks * nh, 1), lambda b, g: (g, 0)),
            gspec(gw), gspec(gw),
        ],
        out_specs=pl.BlockSpec((None, seq, gw), lambda b, g: (b, 0, g)),
        out_shape=jax.ShapeDtypeStruct((batch, seq, D_INNER), BF16),
        scratch_shapes=[
            pltpu.VMEM(((gw + 2 * ns) // LANES, seq + 2 * SUBLANES, LANES), F32),
            pltpu.VMEM(((gw + 2 * ns) // LANES, seq, LANES), F32),
            pltpu.VMEM((2, seq, gw), F32),
            pltpu.VMEM((n_chunks, CHUNK, CHUNK), F32),
            pltpu.VMEM((n_chunks, nh, CHUNK), F32),
            pltpu.VMEM((n_chunks, nh, CHUNK), F32),
            pltpu.VMEM((2, ns, gw), F32),
        ],
        compiler_params=pltpu.CompilerParams(
            dimension_semantics=("parallel", "parallel"), vmem_limit_bytes=VMEM_LIMIT),
        name="ssd",
    )(proj3, proj3, proj3, proj3, dt_t, cw_x, cw_b, cw_c, cb_x, cb_b, cb_c, dtb, alog, dsk, nw)


def _layer_norm(v, g, b):
    mu = jnp.mean(v, axis=1, keepdims=True)
    d = v - mu
    var = jnp.mean(d * d, axis=1, keepdims=True)
    return d * lax.rsqrt(var + LN_EPS) * g + b


def _merge_kernel(x_ref, at_ref, sd_ref, ga_ref, gs_ref, wpa_ref, wps_ref, wo_ref, bg_ref, g_ref, b_ref, o_ref):
    bg = bg_ref[...]
    g_a = jax.nn.sigmoid(ga_ref[...].astype(F32) + bg[:, :D_MODEL])
    g_s = jax.nn.sigmoid(gs_ref[...].astype(F32) + bg[:, D_MODEL:])
    merged = g_a * _dot(at_ref[...], wpa_ref[...]) + g_s * _dot(sd_ref[...], wps_ref[...])
    y = ALPHA * x_ref[...] + _dot(merged.astype(BF16), wo_ref[...])
    o_ref[...] = _layer_norm(y, g_ref[...], b_ref[...])


def _merge(x2, attn2, ssd2, proj2, wpa, wps, wo, bg, ln_g, ln_b, *, tm=512):
    t = x2.shape[0]
    full = lambda a: pl.BlockSpec(a.shape, lambda i: (0, 0))
    return pl.pallas_call(
        _merge_kernel,
        grid=(t // tm,),
        in_specs=[
            pl.BlockSpec((tm, D_MODEL), lambda i: (i, 0)),
            pl.BlockSpec((tm, ATTN_W), lambda i: (i, 0)),
            pl.BlockSpec((tm, D_INNER), lambda i: (i, 0)),
            pl.BlockSpec((tm, D_MODEL), lambda i: (i, GATE_OFF // D_MODEL)),
            pl.BlockSpec((tm, D_MODEL), lambda i: (i, GATE_OFF // D_MODEL + 1)),
            full(wpa), full(wps), full(wo), full(bg), full(ln_g), full(ln_b),
        ],
        out_specs=pl.BlockSpec((tm, D_MODEL), lambda i: (i, 0)),
        out_shape=jax.ShapeDtypeStruct((t, D_MODEL), F32),
        compiler_params=pltpu.CompilerParams(
            dimension_semantics=("parallel",), vmem_limit_bytes=VMEM_LIMIT),
        name="merge_ln1",
    )(x2, attn2, ssd2, proj2, proj2, wpa, wps, wo, bg, ln_g, ln_b)


def _ffn_kernel(x_ref, prev_ref, next_ref, wup_ref, cw_ref, cb_ref, wdn_ref, g_ref, b_ref, o_ref, xh_ref,
                ug0_ref, uu0_ref, ug1_ref, uu1_ref, act_ref, *, tiles_per_seq, tf):
    i = pl.program_id(0)
    tm = x_ref.shape[0]
    halo = SUBLANES
    x = x_ref[...]
    first = (i % tiles_per_seq) == 0
    last = (i % tiles_per_seq) == tiles_per_seq - 1
    zero = jnp.zeros((halo, D_MODEL), F32)
    xh_ref[0:halo, :] = jnp.where(first, zero, prev_ref[...]).astype(BF16)
    xh_ref[halo:halo + tm, :] = x.astype(BF16)
    xh_ref[halo + tm:, :] = jnp.where(last, zero, next_ref[...]).astype(BF16)
    xh = xh_ref[...]

    n_slab = tf // LANES
    stride = CONV_STRIDE
    n_str = tm // stride

    def taps(u_ref, s, v, col):
        lanes = slice(col + s * LANES, col + (s + 1) * LANES)
        w = cw_ref[:, lanes]
        out = cb_ref[:, lanes] + u_ref[s, pl.ds(halo - 1 + v, n_str, stride=stride), :] * w[0:1]
        out = out + u_ref[s, pl.ds(halo + v, n_str, stride=stride), :] * w[1:2]
        return out + u_ref[s, pl.ds(halo + 1 + v, n_str, stride=stride), :] * w[2:3]

    bufs = ((ug0_ref, uu0_ref), (ug1_ref, uu1_ref))

    def up(j):
        ug_ref, uu_ref = bufs[j % 2]
        for u_ref, col in ((ug_ref, j * tf), (uu_ref, D_FF + j * tf)):
            r = _dot(xh, wup_ref[:, col:col + tf])
            for s in range(n_slab):
                u_ref[s] = r[:, s * LANES:(s + 1) * LANES]

    n_f = D_FF // tf
    acc = jnp.zeros((tm, D_MODEL), F32)
    up(0)
    for j in range(n_f):
        if j + 1 < n_f:
            up(j + 1)
        ug_ref, uu_ref = bufs[j % 2]
        for s in range(n_slab):
            for v in range(stride):
                piece = _silu(taps(ug_ref, s, v, j * tf)) * taps(uu_ref, s, v, D_FF + j * tf)
                act_ref[s, pl.ds(v, n_str, stride=stride), :] = piece
        act = jnp.concatenate([act_ref[s] for s in range(n_slab)], axis=1).astype(BF16)
        acc = acc + _dot(act, wdn_ref[j * tf:(j + 1) * tf, :])
    o_ref[...] = _layer_norm(ALPHA * x + acc, g_ref[...], b_ref[...])


def _ffn(x1, wup, cw, cb, wdn, ln_g, ln_b, *, seq, tm=512, tf=256):
    t = x1.shape[0]
    hb = tm // SUBLANES
    n_hb = t // SUBLANES
    full = lambda a: pl.BlockSpec(a.shape, lambda i: (0, 0))
    kern = functools.partial(_ffn_kernel, tiles_per_seq=seq // tm, tf=tf)
    return pl.pallas_call(
        kern,
        grid=(t // tm,),
        in_specs=[
            pl.BlockSpec((tm, D_MODEL), lambda i: (i, 0)),
            pl.BlockSpec((SUBLANES, D_MODEL), lambda i: (jnp.maximum(i * hb - 1, 0), 0)),
            pl.BlockSpec((SUBLANES, D_MODEL), lambda i: (jnp.minimum((i + 1) * hb, n_hb - 1), 0)),
            full(wup), full(cw), full(cb), full(wdn), full(ln_g), full(ln_b),
        ],
        out_specs=pl.BlockSpec((tm, D_MODEL), lambda i: (i, 0)),
        out_shape=jax.ShapeDtypeStruct((t, D_MODEL), F32),
        scratch_shapes=([pltpu.VMEM((tm + 2 * SUBLANES, D_MODEL), BF16)]
                        + [pltpu.VMEM((tf // LANES, tm + 2 * SUBLANES, LANES), F32)] * 4
                        + [pltpu.VMEM((tf // LANES, tm, LANES), F32)]),
        compiler_params=pltpu.CompilerParams(
            dimension_semantics=("parallel",), vmem_limit_bytes=VMEM_LIMIT),
        name="ffn_ln2",
    )(x1, x1, x1, wup, cw, cb, wdn, ln_g, ln_b)


def _rotary_tables(seq):
    half = ROT_DIM // 2
    pos = np.arange(seq, dtype=np.float32)
    inv_freq = np.power(np.float32(ROPE_THETA), -np.arange(0, ROT_DIM, 2, dtype=np.float32) / np.float32(ROT_DIM))
    ang = (pos[:, None] * inv_freq[None, :].astype(np.float32)).astype(np.float32)
    cos, sin = np.cos(ang).astype(np.float32), np.sin(ang).astype(np.float32)
    pad = np.zeros((seq, HEAD_DIM_A - ROT_DIM), np.float32)
    zer = np.zeros((seq, half), np.float32)
    c64 = np.concatenate([cos, cos, pad + 1.0], axis=1)
    sa64 = np.concatenate([-sin, zer, pad], axis=1)
    sb64 = np.concatenate([zer, sin, pad], axis=1)
    tile2 = lambda a: np.concatenate([a, a], axis=1)
    return jnp.asarray(np.stack([tile2(c64), tile2(sa64), tile2(sb64)]))


def _group_rows(p, n_chunks):
    nh = 2 * HEADS_PER_GROUP
    pg = p.reshape(2, N_GROUPS, HEADS_PER_GROUP).transpose(1, 0, 2).reshape(N_GROUPS, 1, nh)
    return jnp.broadcast_to(pg, (N_GROUPS, n_chunks, nh)).reshape(N_GROUPS * n_chunks * nh, 1)


def kernel(x, w_in, b_gate, lambda_q1, lambda_k1, lambda_q2, lambda_k2, attn_subln_w, conv_ssd_w, conv_ssd_b,
           dt_bias, a_log, d_skip, ssd_norm_w, w_proj_attn, w_proj_ssd, w_out, ln1_g, ln1_b, w_up, conv_ffn_w,
           conv_ffn_b, w_down, ln2_g, ln2_b):
    batch, seq, _ = x.shape
    t = batch * seq
    l = 0
    xbc_w = D_INNER + 2 * N_GROUPS * D_STATE
    dt_off = Z_OFF + D_INNER + xbc_w
    w = w_in[l]
    w_a = w.astype(BF16)
    w_b = w[:, dt_off + 2 * N_HEADS_S:].astype(BF16)
    w_dt = w[:, dt_off:dt_off + 2 * N_HEADS_S]
    w_dt_t = (w_dt.reshape(D_MODEL, 2, N_GROUPS, HEADS_PER_GROUP).transpose(2, 1, 3, 0)
              .reshape(2 * N_HEADS_S, D_MODEL).astype(BF16))
    rot_tab = _rotary_tables(seq)

    x2 = x.reshape(t, D_MODEL)
    proj2, dt_t = _in_proj(x2, w_a, w_b, w_dt_t, batch=batch, seq=seq)
    proj3 = proj2.reshape(batch, seq, PROJ_W)

    lam_p = jnp.stack([lambda_q1[l], lambda_k1[l], lambda_q2[l], lambda_k2[l]]).astype(F32)
    attn = _attention(proj3, rot_tab, lam_p, attn_subln_w[l].reshape(1, 2 * HEAD_DIM_A))

    cw = conv_ssd_w[l]
    cb = conv_ssd_b[l].reshape(1, xbc_w)
    nb = N_GROUPS * D_STATE
    ssd = _ssd(proj3, dt_t,
               cw[:, :D_INNER], cw[:, D_INNER:D_INNER + nb], cw[:, D_INNER + nb:],
               cb[:, :D_INNER], cb[:, D_INNER:D_INNER + nb], cb[:, D_INNER + nb:],
               _group_rows(dt_bias[l], seq // CHUNK), _group_rows(a_log[l], seq // CHUNK),
               jnp.repeat(d_skip[l], HEAD_DIM_S).reshape(1, D_INNER), ssd_norm_w[l].reshape(1, D_INNER))

    x1 = _merge(x2, attn.reshape(t, ATTN_W), ssd.reshape(t, D_INNER), proj2,
                w_proj_attn[l].astype(BF16), w_proj_ssd[l].astype(BF16), w_out[l].astype(BF16),
                b_gate[l].reshape(1, 2 * D_MODEL), ln1_g[l].reshape(1, D_MODEL), ln1_b[l].reshape(1, D_MODEL))

    out = _ffn(x1, w_up[l].astype(BF16), conv_ffn_w[l], conv_ffn_b[l].reshape(1, 2 * D_FF),
               w_down[l].astype(BF16), ln2_g[l].reshape(1, D_MODEL), ln2_b[l].reshape(1, D_MODEL), seq=seq)
    return out.reshape(batch, seq, D_MODEL)
```

```python
import functools
import math

import jax
import jax.numpy as jnp
import numpy as np
from jax import lax
from jax.experimental import pallas as pl
from jax.experimental.pallas import tpu as pltpu

F32 = jnp.float32
BF16 = jnp.bfloat16

D_MODEL = 1024
N_HEADS_A = 8
HEAD_DIM_A = 64
ROT_DIM = HEAD_DIM_A // 4
ROPE_THETA = 500000.0
ATTN_W = N_HEADS_A * 2 * HEAD_DIM_A
D_INNER = 2 * D_MODEL
HEAD_DIM_S = 64
N_HEADS_S = D_INNER // HEAD_DIM_S
N_GROUPS = 8
HEADS_PER_GROUP = N_HEADS_S // N_GROUPS
GROUP_W = HEADS_PER_GROUP * HEAD_DIM_S
D_STATE = 128
CONV_SSD = 5
CHUNK = 128
D_FF = 2816
DEPTH = 1
ALPHA = (2 * DEPTH) ** 0.25
LN_EPS = 1e-5
RMS_EPS = 1e-5
LAM_INIT = 0.8 - 0.6 * math.exp(-0.3 * 0)
LOG2_E = math.log2(math.e)

LANES = 128
SUBLANES = 8
CONV_STRIDE = 4
VMEM_LIMIT = 56 * 1024 * 1024

Q_OFF, K_OFF, V_OFF = 0, ATTN_W, 2 * ATTN_W
Z_OFF = 3 * ATTN_W
XS_OFF = Z_OFF + D_INNER
B_OFF = XS_OFF + D_INNER
C_OFF = B_OFF + N_GROUPS * D_STATE
GATE_OFF = C_OFF + N_GROUPS * D_STATE
PROJ_W = GATE_OFF + 2 * D_MODEL


def _dot(a, b):
    return jnp.dot(a, b, preferred_element_type=F32)


def _dot_nt(a, b):
    return lax.dot_general(a, b, (((1,), (1,)), ((), ())), preferred_element_type=F32)


def _dot_exact(a, b):
    return jnp.dot(a, b, preferred_element_type=F32, precision=lax.Precision.HIGHEST)


def _in_proj_kernel(x_ref, wa_ref, wb_ref, wdt_ref, o_ref, dt_ref, xb_ref, *, n_q, n_a):
    j = pl.program_id(1)

    @pl.when(j == 0)
    def _():
        xb = x_ref[...].astype(BF16)
        xb_ref[...] = xb
        dt_t = _dot_nt(wdt_ref[...], xb)
        for c in range(dt_ref.shape[0]):
            dt_ref[c] = dt_t[:, c * CHUNK:(c + 1) * CHUNK]

    @pl.when(j < n_q)
    def _():
        o_ref[...] = (_dot(xb_ref[...], wa_ref[...]) * (HEAD_DIM_A ** -0.5 * LOG2_E)).astype(o_ref.dtype)

    @pl.when((j >= n_q) & (j < n_a))
    def _():
        o_ref[...] = _dot(xb_ref[...], wa_ref[...]).astype(o_ref.dtype)

    @pl.when(j >= n_a)
    def _():
        o_ref[...] = _dot(xb_ref[...], wb_ref[...]).astype(o_ref.dtype)


def _in_proj(x2, w_a, w_b, w_dt_t, *, batch, seq, tm=2048, tn=1024):
    t = x2.shape[0]
    n_seq_tiles = seq // tm
    n_dt = w_dt_t.shape[0]
    n_a = GATE_OFF // tn
    kern = functools.partial(_in_proj_kernel, n_q=ATTN_W // tn, n_a=n_a)
    return pl.pallas_call(
        kern,
        grid=(t // tm, PROJ_W // tn),
        in_specs=[
            pl.BlockSpec((tm, D_MODEL), lambda i, j: (i, 0)),
            pl.BlockSpec((D_MODEL, tn), lambda i, j: (0, jnp.minimum(j, n_a - 1))),
            pl.BlockSpec((D_MODEL, tn), lambda i, j: (0, jnp.maximum(j - n_a, 0))),
            pl.BlockSpec((n_dt, D_MODEL), lambda i, j: (0, 0)),
        ],
        out_specs=[
            pl.BlockSpec((tm, tn), lambda i, j: (i, j)),
            pl.BlockSpec((None, tm // CHUNK, n_dt, CHUNK),
                         lambda i, j: (i // n_seq_tiles, i % n_seq_tiles, 0, 0)),
        ],
        out_shape=[
            jax.ShapeDtypeStruct((t, PROJ_W), BF16),
            jax.ShapeDtypeStruct((batch, seq // CHUNK, n_dt, CHUNK), F32),
        ],
        scratch_shapes=[pltpu.VMEM((tm, D_MODEL), BF16)],
        compiler_params=pltpu.CompilerParams(
            dimension_semantics=("parallel", "arbitrary"), vmem_limit_bytes=VMEM_LIMIT),
        name="in_proj",
    )(x2, w_a, w_b, w_dt_t)


def _rotary(a_ref, tab_ref, rows):
    a = a_ref[rows, :].astype(F32)
    up = pltpu.roll(a, LANES - ROT_DIM // 2, axis=1)
    dn = pltpu.roll(a, ROT_DIM // 2, axis=1)
    return (a * tab_ref[0, rows, :] + up * tab_ref[1, rows, :] + dn * tab_ref[2, rows, :]).astype(BF16)


def _attn_kernel(q_ref, k_ref, v_ref, tq_ref, tk_ref, lam_ref, sw_ref, o_ref, kr_ref, v1_ref, s1_ref, s2_ref, *, tk):
    tq = q_ref.shape[0]
    seq = k_ref.shape[0]
    n_kc = seq // tk

    @pl.when(pl.program_id(2) == 0)
    def _():
        v1_ref[:, :LANES] = v_ref[...]
        v1_ref[:, LANES:] = jnp.ones((seq, LANES), BF16)
        for c in range(n_kc):
            rows = slice(c * tk, (c + 1) * tk)
            kr_ref[rows, :] = _rotary(k_ref, tk_ref, rows)

    q = _rotary(q_ref, tq_ref, slice(None))
    lane = lax.broadcasted_iota(jnp.int32, q.shape, 1)
    zero = jnp.zeros_like(q)
    q1 = jnp.where(lane < HEAD_DIM_A, q, zero)
    q2 = jnp.where(lane >= HEAD_DIM_A, q, zero)

    m1 = jnp.full((tq, LANES), -jnp.inf, F32)
    m2 = m1
    for c in range(n_kc):
        kc = kr_ref[c * tk:(c + 1) * tk, :]
        sc1 = _dot_nt(q1, kc)
        sc2 = _dot_nt(q2, kc)
        s1_ref[c] = sc1
        s2_ref[c] = sc2
        for u in range(tk // LANES):
            m1 = jnp.maximum(m1, sc1[:, u * LANES:(u + 1) * LANES])
            m2 = jnp.maximum(m2, sc2[:, u * LANES:(u + 1) * LANES])
    m1 = jnp.max(m1, axis=1, keepdims=True)
    m2 = jnp.max(m2, axis=1, keepdims=True)

    a1 = jnp.zeros((tq, 2 * LANES), F32)
    a2 = a1
    for c in range(n_kc):
        vc = v1_ref[c * tk:(c + 1) * tk, :]
        p1 = jnp.exp2(s1_ref[c] - m1).astype(BF16)
        p2 = jnp.exp2(s2_ref[c] - m2).astype(BF16)
        a1 = a1 + _dot(p1, vc)
        a2 = a2 + _dot(p2, vc)

    lp = lam_ref[...]
    lam = (jnp.exp(jnp.sum(lp[0:1] * lp[1:2], axis=1, keepdims=True))
           - jnp.exp(jnp.sum(lp[2:3] * lp[3:4], axis=1, keepdims=True)) + LAM_INIT)
    o =a1[:, :LANES] / a1[:, LANES:] - lam * (a2[:, :LANES] / a2[:, LANES:])
    ms = jnp.mean(o * o, axis=1, keepdims=True)
    o = o * lax.rsqrt(ms + RMS_EPS) * sw_ref[...] * (1.0 - LAM_INIT)
    o_ref[...] = o.astype(o_ref.dtype)


def _attention(proj3, rot_tab, lam_p, subln_w, *, tq=1024, tk=512):
    batch, seq, _ = proj3.shape
    hw = 2 * HEAD_DIM_A
    kern = functools.partial(_attn_kernel, tk=tk)
    return pl.pallas_call(
        kern,
        grid=(batch, N_HEADS_A, seq // tq),
        in_specs=[
            pl.BlockSpec((None, tq, hw), lambda b, h, i: (b, i, Q_OFF // hw + h)),
            pl.BlockSpec((None, seq, hw), lambda b, h, i: (b, 0, K_OFF // hw + h)),
            pl.BlockSpec((None, seq, hw), lambda b, h, i: (b, 0, V_OFF // hw + h)),
            pl.BlockSpec((3, tq, hw), lambda b, h, i: (0, i, 0)),
            pl.BlockSpec((3, seq, hw), lambda b, h, i: (0, 0, 0), pipeline_mode=pl.Buffered(1)),
            pl.BlockSpec((4, HEAD_DIM_A), lambda b, h, i: (0, 0)),
            pl.BlockSpec((1, hw), lambda b, h, i: (0, 0)),
        ],
        out_specs=pl.BlockSpec((None, tq, hw), lambda b, h, i: (b, i, h)),
        out_shape=jax.ShapeDtypeStruct((batch, seq, ATTN_W), BF16),
        scratch_shapes=[
            pltpu.VMEM((seq, hw), BF16),
            pltpu.VMEM((seq, 2 * LANES), BF16),
            pltpu.VMEM((seq // tk, tq, tk), F32),
            pltpu.VMEM((seq // tk, tq, tk), F32),
        ],
        compiler_params=pltpu.CompilerParams(
            dimension_semantics=("parallel", "parallel", "arbitrary"), vmem_limit_bytes=VMEM_LIMIT),
        name="diff_attn",
    )(proj3, proj3, proj3, rot_tab, rot_tab, lam_p, subln_w)


def _silu(v):
    return v * jax.nn.sigmoid(v)


def _ssd_kernel(z_ref, xs_ref, b_ref, c_ref, dt_ref, cwx_ref, cwb_ref, cwc_ref, cbx_ref, cbb_ref, cbc_ref,
                dtb_ref, alog_ref, dsk_ref, nw_ref, o_ref,
                pad_ref, cv_ref, y_ref, cb_ref, src_ref, cs_ref, st_ref):
    seq = xs_ref.shape[0]
    n_chunks = seq // CHUNK
    halo = SUBLANES
    half = CONV_SSD // 2
    gw = GROUP_W
    w_all = gw + 2 * D_STATE

    n_slab = w_all // LANES
    srcs = (xs_ref, xs_ref, b_ref, c_ref)
    lane0 = (0, LANES, 0, 0)
    zero_halo = jnp.zeros((halo, LANES), F32)
    for s in range(n_slab):
        pad_ref[s, 0:halo, :] = zero_halo
        pad_ref[s, halo + seq:, :] = zero_halo
        pad_ref[s, halo:halo + seq, :] = srcs[s][:, lane0[s]:lane0[s] + LANES].astype(F32)
    cw = jnp.concatenate([cwx_ref[...], cwb_ref[...], cwc_ref[...]], axis=1)
    cbias = jnp.concatenate([cbx_ref[...], cbb_ref[...], cbc_ref[...]], axis=1)
    stride = CONV_STRIDE
    blk = SUBLANES * stride

    def conv_chunk(c, carry):
        t0 = pl.multiple_of(c * CHUNK, CHUNK)
        for s in range(n_slab):
            w_s = cw[:, s * LANES:(s + 1) * LANES]
            b_s = jnp.broadcast_to(cbias[:, s * LANES:(s + 1) * LANES], (SUBLANES, LANES))
            for r in range(0, CHUNK, blk):
                for v in range(stride):
                    base = t0 + (r + v)
                    acc = b_s
                    for k in range(CONV_SSD):
                        acc = acc + pad_ref[s, pl.ds(base + (halo - half + k), SUBLANES, stride=stride), :] * w_s[k:k + 1, :]
                    cv_ref[s, pl.ds(base, SUBLANES, stride=stride), :] = _silu(acc)
        rows = pl.ds(t0, CHUNK)
        cb_ref[c] = _dot_nt(cv_ref[3, rows, :].astype(BF16), cv_ref[2, rows, :].astype(BF16))
        return carry

    lax.fori_loop(0, n_chunks, conv_chunk, 0, unroll=8)

    nh = 2 * HEADS_PER_GROUP
    r_i = lax.broadcasted_iota(jnp.int32, (CHUNK, CHUNK), 0)
    c_i = lax.broadcasted_iota(jnp.int32, (CHUNK, CHUNK), 1)
    upper = (r_i <= c_i).astype(F32)
    lower = (r_i >= c_i).astype(F32)
    raw = dt_ref[...].reshape(n_chunks * nh, CHUNK) + dtb_ref[...]
    dt_all = jnp.maximum(raw, 0.0) + jnp.log1p(jnp.exp(-jnp.abs(raw)))
    da = dt_all * (-jnp.exp(alog_ref[...]))
    row_all = lax.broadcasted_iota(jnp.int32, (n_chunks * nh, CHUNK), 0)
    cs_all = jnp.where(row_all % nh < HEADS_PER_GROUP, _dot_exact(da, upper), _dot_exact(da, lower))
    cs2_all = cs_all * LOG2_E
    cs_ref[...] = cs2_all.reshape(n_chunks, nh, CHUNK)
    src_ref[...] = (cs2_all - jnp.log2(dt_all)).reshape(n_chunks, nh, CHUNK)

    lane_head = lax.broadcasted_iota(jnp.int32, (1, gw), 1) // HEAD_DIM_S
    first_head = lax.broadcasted_iota(jnp.int32, (1, LANES), 1) < HEAD_DIM_S

    def chunk_step(c, reverse):
        r0 = HEADS_PER_GROUP if reverse else 0
        t0 = pl.multiple_of(c * CHUNK, CHUNK)
        rows = pl.ds(t0, CHUNK)
        x = jnp.concatenate([cv_ref[0, rows, :], cv_ref[1, rows, :]], axis=1)
        bm = cv_ref[2, rows, :]
        cm = cv_ref[3, rows, :].astype(BF16)
        cs2 = cs_ref[c]
        src = src_ref[c]
        cbm = cb_ref[c]
        if reverse:
            tot = cs2[:, 0:1]
            keep = r_i <= c_i
        else:
            tot = cs2[:, CHUNK - 1:CHUNK]
            keep = r_i >= c_i
        cs2_t = jnp.transpose(cs2)
        col_b = [jnp.broadcast_to(cs2_t[:, r0 + e:r0 + e + 1], (CHUNK, LANES)) for e in range(HEADS_PER_GROUP)]
        g_out = jnp.exp2(jnp.concatenate(
            [jnp.where(first_head, col_b[2 * v], col_b[2 * v + 1]) for v in range(gw // LANES)], axis=1))
        sel_row = (lax.broadcasted_iota(jnp.int32, (nh, gw), 0) == lane_head + r0).astype(F32)
        cdec = jnp.exp2(jnp.sum(tot * sel_row, axis=0, keepdims=True))
        wrow = jnp.exp2(tot - src)
        bt = jnp.transpose(bm)
        xb = x.astype(BF16)
        zero = jnp.zeros_like(xb)
        ms, bs, xs = [], [], []
        for e in range(HEADS_PER_GROUP):
            r = r0 + e
            seg = col_b[e] - src[r:r + 1, :]
            decay = jnp.exp2(jnp.where(keep, seg, -jnp.inf))
            ms.append((cbm * decay).astype(BF16))
            bs.append((bt * wrow[r:r + 1, :]).astype(BF16))
            xs.append(jnp.where(lane_head == e, xb, zero))
        xcat = jnp.concatenate(xs, axis=0)
        d = 1 if reverse else 0
        st = st_ref[d]
        y = _dot(jnp.concatenate(ms, axis=1), xcat) + _dot(cm, st.astype(BF16)) * g_out
        st_ref[d] = st * cdec + _dot(jnp.concatenate(bs, axis=1), xcat)
        y_ref[d, pl.ds(t0, CHUNK), :] = y

    st_ref[...] = jnp.zeros_like(st_ref)

    def scan(i, carry):
        chunk_step(i, False)
        chunk_step(n_chunks - 1 - i, True)
        return carry

    lax.fori_loop(0, n_chunks, scan, 0, unroll=8)

    def finish(c, carry):
        t0 = pl.multiple_of(c * CHUNK, CHUNK)
        rows = pl.ds(t0, CHUNK)
        xc = jnp.concatenate([cv_ref[0, rows, :], cv_ref[1, rows, :]], axis=1)
        y = y_ref[0, rows, :] + y_ref[1, rows, :] + xc * dsk_ref[...]
        y = y * _silu(z_ref[rows, :].astype(F32))
        ms = jnp.mean(y * y, axis=1, keepdims=True)
        y = y * lax.rsqrt(ms + RMS_EPS) * nw_ref[...]
        o_ref[rows, :] = y.astype(o_ref.dtype)
        return carry

    lax.fori_loop(0, n_chunks, finish, 0, unroll=8)


def _ssd(proj3, dt_t, cw_x, cw_b, cw_c, cb_x, cb_b, cb_c, dtb, alog, dsk, nw):
    batch, seq, _ = proj3.shape
    gw, ns = GROUP_W, D_STATE
    nh = 2 * HEADS_PER_GROUP
    n_chunks = seq // CHUNK
    gspec = lambda w: pl.BlockSpec((1, w), lambda b, g: (0, g))
    return pl.pallas_call(
        _ssd_kernel,
        grid=(batch, N_GROUPS),
        in_specs=[
            pl.BlockSpec((None, seq, gw), lambda b, g: (b, 0, Z_OFF // gw + g)),
            pl.BlockSpec((None, seq, gw), lambda b, g: (b, 0, XS_OFF // gw + g)),
            pl.BlockSpec((None, seq, ns), lambda b, g: (b, 0, B_OFF // ns + g)),
            pl.BlockSpec((None, seq, ns), lambda b, g: (b, 0, C_OFF // ns + g)),
            pl.BlockSpec((None, n_chunks, nh, CHUNK), lambda b, g: (b, 0, g, 0)),
            pl.BlockSpec((CONV_SSD, gw), lambda b, g: (0, g)),
            pl.BlockSpec((CONV_SSD, ns), lambda b, g: (0, g)),
            pl.BlockSpec((CONV_SSD, ns), lambda b, g: (0, g)),
            gspec(gw), gspec(ns), gspec(ns),
            pl.BlockSpec((n_chunks * nh, 1), lambda b, g: (g, 0)),
            pl.BlockSpec((n_chunks * nh, 1), lambda b, g: (g, 0)),
            gspec(gw), gspec(gw),
        ],
        out_specs=pl.BlockSpec((None, seq, gw), lambda b, g: (b, 0, g)),
        out_shape=jax.ShapeDtypeStruct((batch, seq, D_INNER), BF16),
        scratch_shapes=[
            pltpu.VMEM(((gw + 2 * ns) // LANES, seq + 2 * SUBLANES, LANES), F32),
            pltpu.VMEM(((gw + 2 * ns) // LANES, seq, LANES), F32),
            pltpu.VMEM((2, seq, gw), F32),
            pltpu.VMEM((n_chunks, CHUNK, CHUNK), F32),
            pltpu.VMEM((n_chunks, nh, CHUNK), F32),
            pltpu.VMEM((n_chunks, nh, CHUNK), F32),
            pltpu.VMEM((2, ns, gw), F32),
        ],
        compiler_params=pltpu.CompilerParams(
            dimension_semantics=("parallel", "parallel"), vmem_limit_bytes=VMEM_LIMIT),
        name="ssd",
    )(proj3, proj3, proj3, proj3, dt_t, cw_x, cw_b, cw_c, cb_x, cb_b, cb_c, dtb, alog, dsk, nw)


def _layer_norm(v, g, b):
    mu = jnp.mean(v, axis=1, keepdims=True)
    d = v - mu
    var = jnp.mean(d * d, axis=1, keepdims=True)
    return d * lax.rsqrt(var + LN_EPS) * g + b


def _merge_kernel(x_ref, at_ref, sd_ref, ga_ref, gs_ref, wpa_ref, wps_ref, wo_ref, bg_ref, g_ref, b_ref, o_ref):
    bg = bg_ref[...]
    g_a = jax.nn.sigmoid(ga_ref[...].astype(F32) + bg[:, :D_MODEL])
    g_s = jax.nn.sigmoid(gs_ref[...].astype(F32) + bg[:, D_MODEL:])
    merged = g_a * _dot(at_ref[...], wpa_ref[...]) + g_s * _dot(sd_ref[...], wps_ref[...])
    y = ALPHA * x_ref[...] + _dot(merged.astype(BF16), wo_ref[...])
    o_ref[...] = _layer_norm(y, g_ref[...], b_ref[...])


def _merge(x2, attn2, ssd2, proj2, wpa, wps, wo, bg, ln_g, ln_b, *, tm=512):
    t = x2.shape[0]
    full = lambda a: pl.BlockSpec(a.shape, lambda i: (0, 0))
    return pl.pallas_call(
        _merge_kernel,
        grid=(t // tm,),
        in_specs=[
            pl.BlockSpec((tm, D_MODEL), lambda i: (i, 0)),
            pl.BlockSpec((tm, ATTN_W), lambda i: (i, 0)),
            pl.BlockSpec((tm, D_INNER), lambda i: (i, 0)),
            pl.BlockSpec((tm, D_MODEL), lambda i: (i, GATE_OFF // D_MODEL)),
            pl.BlockSpec((tm, D_MODEL), lambda i: (i, GATE_OFF // D_MODEL + 1)),
            full(wpa), full(wps), full(wo), full(bg), full(ln_g), full(ln_b),
        ],
        out_specs=pl.BlockSpec((tm, D_MODEL), lambda i: (i, 0)),
        out_shape=jax.ShapeDtypeStruct((t, D_MODEL), F32),
        compiler_params=pltpu.CompilerParams(
            dimension_semantics=("parallel",), vmem_limit_bytes=VMEM_LIMIT),
        name="merge_ln1",
    )(x2, attn2, ssd2, proj2, proj2, wpa, wps, wo, bg, ln_g, ln_b)


def _ffn_kernel(x_ref, prev_ref, next_ref, wup_ref, cw_ref, cb_ref, wdn_ref, g_ref, b_ref, o_ref, xh_ref,
                ug0_ref, uu0_ref, ug1_ref, uu1_ref, *, tiles_per_seq, tf):
    i = pl.program_id(0)
    tm = x_ref.shape[0]
    halo = SUBLANES
    x = x_ref[...]
    first = (i % tiles_per_seq) == 0
    last = (i % tiles_per_seq) == tiles_per_seq - 1
    zero = jnp.zeros((halo, D_MODEL), F32)
    xh_ref[0:halo, :] = jnp.where(first, zero, prev_ref[...]).astype(BF16)
    xh_ref[halo:halo + tm, :] = x.astype(BF16)
    xh_ref[halo + tm:, :] = jnp.where(last, zero, next_ref[...]).astype(BF16)
    xh = xh_ref[...]

    def conv(u_ref, col):
        w = cw_ref[:, col:col + tf]
        out = cb_ref[:, col:col + tf] + u_ref[halo - 1:halo - 1 + tm, :] * w[0:1]
        out = out + u_ref[halo:halo + tm, :] * w[1:2]
        return out + u_ref[halo + 1:halo + 1 + tm, :] * w[2:3]

    bufs = ((ug0_ref, uu0_ref), (ug1_ref, uu1_ref))

    def up(j):
        ug_ref, uu_ref = bufs[j % 2]
        ug_ref[...] = _dot(xh, wup_ref[:, j * tf:(j + 1) * tf])
        uu_ref[...] = _dot(xh, wup_ref[:, D_FF + j * tf:D_FF + (j + 1) * tf])

    n_f = D_FF // tf
    acc = jnp.zeros((tm, D_MODEL), F32)
    up(0)
    for j in range(n_f):
        if j + 1 < n_f:
            up(j + 1)
        ug_ref, uu_ref = bufs[j % 2]
        act = (_silu(conv(ug_ref, j * tf)) * conv(uu_ref, D_FF + j * tf)).astype(BF16)
        acc = acc + _dot(act, wdn_ref[j * tf:(j + 1) * tf, :])
    o_ref[...] = _layer_norm(ALPHA * x + acc, g_ref[...], b_ref[...])


def _ffn(x1, wup, cw, cb, wdn, ln_g, ln_b, *, seq, tm=512, tf=256):
    t = x1.shape[0]
    hb = tm // SUBLANES
    n_hb = t // SUBLANES
    full = lambda a: pl.BlockSpec(a.shape, lambda i: (0, 0))
    kern = functools.partial(_ffn_kernel, tiles_per_seq=seq // tm, tf=tf)
    return pl.pallas_call(
        kern,
        grid=(t // tm,),
        in_specs=[
            pl.BlockSpec((tm, D_MODEL), lambda i: (i, 0)),
            pl.BlockSpec((SUBLANES, D_MODEL), lambda i: (jnp.maximum(i * hb - 1, 0), 0)),
            pl.BlockSpec((SUBLANES, D_MODEL), lambda i: (jnp.minimum((i + 1) * hb, n_hb - 1), 0)),
            full(wup), full(cw), full(cb), full(wdn), full(ln_g), full(ln_b),
        ],
        out_specs=pl.BlockSpec((tm, D_MODEL), lambda i: (i, 0)),
        out_shape=jax.ShapeDtypeStruct((t, D_MODEL), F32),
        scratch_shapes=([pltpu.VMEM((tm + 2 * SUBLANES, D_MODEL), BF16)]
                        + [pltpu.VMEM((tm + 2 * SUBLANES, tf), F32)] * 4),
        compiler_params=pltpu.CompilerParams(
            dimension_semantics=("parallel",), vmem_limit_bytes=VMEM_LIMIT),
        name="ffn_ln2",
    )(x1, x1, x1, wup, cw, cb, wdn, ln_g, ln_b)


def _rotary_tables(seq):
    half = ROT_DIM // 2
    pos = np.arange(seq, dtype=np.float32)
    inv_freq = np.power(np.float32(ROPE_THETA), -np.arange(0, ROT_DIM, 2, dtype=np.float32) / np.float32(ROT_DIM))
    ang = (pos[:, None] * inv_freq[None, :].astype(np.float32)).astype(np.float32)
    cos, sin = np.cos(ang).astype(np.float32), np.sin(ang).astype(np.float32)
    pad = np.zeros((seq, HEAD_DIM_A - ROT_DIM), np.float32)
    zer = np.zeros((seq, half), np.float32)
    c64 = np.concatenate([cos, cos, pad + 1.0], axis=1)
    sa64 = np.concatenate([-sin, zer, pad], axis=1)
    sb64 = np.concatenate([zer, sin, pad], axis=1)
    tile2 = lambda a: np.concatenate([a, a], axis=1)
    return jnp.asarray(np.stack([tile2(c64), tile2(sa64), tile2(sb64)]))


def _group_rows(p, n_chunks):
    nh = 2 * HEADS_PER_GROUP
    pg = p.reshape(2, N_GROUPS, HEADS_PER_GROUP).transpose(1, 0, 2).reshape(N_GROUPS, 1, nh)
    return jnp.broadcast_to(pg, (N_GROUPS, n_chunks, nh)).reshape(N_GROUPS * n_chunks * nh, 1)


def kernel(x, w_in, b_gate, lambda_q1, lambda_k1, lambda_q2, lambda_k2, attn_subln_w, conv_ssd_w, conv_ssd_b,
           dt_bias, a_log, d_skip, ssd_norm_w, w_proj_attn, w_proj_ssd, w_out, ln1_g, ln1_b, w_up, conv_ffn_w,
           conv_ffn_b, w_down, ln2_g, ln2_b):
    batch, seq, _ = x.shape
    t = batch * seq
    l = 0
    xbc_w = D_INNER + 2 * N_GROUPS * D_STATE
    dt_off = Z_OFF + D_INNER + xbc_w
    w = w_in[l]
    w_a = w.astype(BF16)
    w_b = w[:, dt_off + 2 * N_HEADS_S:].astype(BF16)
    w_dt = w[:, dt_off:dt_off + 2 * N_HEADS_S]
    w_dt_t = (w_dt.reshape(D_MODEL, 2, N_GROUPS, HEADS_PER_GROUP).transpose(2, 1, 3, 0)
              .reshape(2 * N_HEADS_S, D_MODEL).astype(BF16))
    rot_tab = _rotary_tables(seq)

    x2 = x.reshape(t, D_MODEL)
    proj2, dt_t = _in_proj(x2, w_a, w_b, w_dt_t, batch=batch, seq=seq)
    proj3 = proj2.reshape(batch, seq, PROJ_W)

    lam_p = jnp.stack([lambda_q1[l], lambda_k1[l], lambda_q2[l], lambda_k2[l]]).astype(F32)
    attn = _attention(proj3, rot_tab, lam_p, attn_subln_w[l].reshape(1, 2 * HEAD_DIM_A))

    cw = conv_ssd_w[l]
    cb = conv_ssd_b[l].reshape(1, xbc_w)
    nb = N_GROUPS * D_STATE
    ssd = _ssd(proj3, dt_t,
               cw[:, :D_INNER], cw[:, D_INNER:D_INNER + nb], cw[:, D_INNER + nb:],
               cb[:, :D_INNER], cb[:, D_INNER:D_INNER + nb], cb[:, D_INNER + nb:],
               _group_rows(dt_bias[l], seq // CHUNK), _group_rows(a_log[l], seq // CHUNK),
               jnp.repeat(d_skip[l], HEAD_DIM_S).reshape(1, D_INNER), ssd_norm_w[l].reshape(1, D_INNER))

    x1 = _merge(x2, attn.reshape(t, ATTN_W), ssd.reshape(t, D_INNER), proj2,
                w_proj_attn[l].astype(BF16), w_proj_ssd[l].astype(BF16), w_out[l].astype(BF16),
                b_gate[l].reshape(1, 2 * D_MODEL), ln1_g[l].reshape(1, D_MODEL), ln1_b[l].reshape(1, D_MODEL))

    out = _ffn(x1, w_up[l].astype(BF16), conv_ffn_w[l], conv_ffn_b[l].reshape(1, 2 * D_FF),
               w_down[l].astype(BF16), ln2_g[l].reshape(1, D_MODEL), ln2_b[l].reshape(1, D_MODEL), seq=seq)
    return out.reshape(batch, seq, D_MODEL)
```

```python
import functools
import math

import jax
import jax.numpy as jnp
import numpy as np
from jax import lax
from jax.experimental import pallas as pl
from jax.experimental.pallas import tpu as pltpu

F32 = jnp.float32
BF16 = jnp.bfloat16

D_MODEL = 1024
N_HEADS_A = 8
HEAD_DIM_A = 64
ROT_DIM = HEAD_DIM_A // 4
ROPE_THETA = 500000.0
ATTN_W = N_HEADS_A * 2 * HEAD_DIM_A
D_INNER = 2 * D_MODEL
HEAD_DIM_S = 64
N_HEADS_S = D_INNER // HEAD_DIM_S
N_GROUPS = 8
HEADS_PER_GROUP = N_HEADS_S // N_GROUPS
GROUP_W = HEADS_PER_GROUP * HEAD_DIM_S
D_STATE = 128
CONV_SSD = 5
CHUNK = 128
D_FF = 2816
DEPTH = 1
ALPHA = (2 * DEPTH) ** 0.25
LN_EPS = 1e-5
RMS_EPS = 1e-5
LAM_INIT = 0.8 - 0.6 * math.exp(-0.3 * 0)
LOG2_E = math.log2(math.e)

LANES = 128
SUBLANES = 8
CONV_STRIDE = 4
VMEM_LIMIT = 56 * 1024 * 1024

Q_OFF, K_OFF, V_OFF = 0, ATTN_W, 2 * ATTN_W
Z_OFF = 3 * ATTN_W
XS_OFF = Z_OFF + D_INNER
B_OFF = XS_OFF + D_INNER
C_OFF = B_OFF + N_GROUPS * D_STATE
GATE_OFF = C_OFF + N_GROUPS * D_STATE
PROJ_W = GATE_OFF + 2 * D_MODEL


def _dot(a, b):
    return jnp.dot(a, b, preferred_element_type=F32)


def _dot_nt(a, b):
    return lax.dot_general(a, b, (((1,), (1,)), ((), ())), preferred_element_type=F32)


def _dot_exact(a, b):
    return jnp.dot(a, b, preferred_element_type=F32, precision=lax.Precision.HIGHEST)


def _in_proj_kernel(x_ref, wa_ref, wb_ref, wdt_ref, o_ref, dt_ref, xb_ref, *, n_q, n_a):
    j = pl.program_id(1)

    @pl.when(j == 0)
    def _():
        xb = x_ref[...].astype(BF16)
        xb_ref[...] = xb
        dt_t = _dot_nt(wdt_ref[...], xb)
        for c in range(dt_ref.shape[0]):
            dt_ref[c] = dt_t[:, c * CHUNK:(c + 1) * CHUNK]

    @pl.when(j < n_q)
    def _():
        o_ref[...] = (_dot(xb_ref[...], wa_ref[...]) * (HEAD_DIM_A ** -0.5 * LOG2_E)).astype(o_ref.dtype)

    @pl.when((j >= n_q) & (j < n_a))
    def _():
        o_ref[...] = _dot(xb_ref[...], wa_ref[...]).astype(o_ref.dtype)

    @pl.when(j >= n_a)
    def _():
        o_ref[...] = _dot(xb_ref[...], wb_ref[...]).astype(o_ref.dtype)


def _in_proj(x2, w_a, w_b, w_dt_t, *, batch, seq, tm=2048, tn=1024):
    t = x2.shape[0]
    n_seq_tiles = seq // tm
    n_dt = w_dt_t.shape[0]
    n_a = GATE_OFF // tn
    kern = functools.partial(_in_proj_kernel, n_q=ATTN_W // tn, n_a=n_a)
    return pl.pallas_call(
        kern,
        grid=(t // tm, PROJ_W // tn),
        in_specs=[
            pl.BlockSpec((tm, D_MODEL), lambda i, j: (i, 0)),
            pl.BlockSpec((D_MODEL, tn), lambda i, j: (0, jnp.minimum(j, n_a - 1))),
            pl.BlockSpec((D_MODEL, tn), lambda i, j: (0, jnp.maximum(j - n_a, 0))),
            pl.BlockSpec((n_dt, D_MODEL), lambda i, j: (0, 0)),
        ],
        out_specs=[
            pl.BlockSpec((tm, tn), lambda i, j: (i, j)),
            pl.BlockSpec((None, tm // CHUNK, n_dt, CHUNK),
                         lambda i, j: (i // n_seq_tiles, i % n_seq_tiles, 0, 0)),
        ],
        out_shape=[
            jax.ShapeDtypeStruct((t, PROJ_W), BF16),
            jax.ShapeDtypeStruct((batch, seq // CHUNK, n_dt, CHUNK), F32),
        ],
        scratch_shapes=[pltpu.VMEM((tm, D_MODEL), BF16)],
        compiler_params=pltpu.CompilerParams(
            dimension_semantics=("parallel", "arbitrary"), vmem_limit_bytes=VMEM_LIMIT),
        name="in_proj",
    )(x2, w_a, w_b, w_dt_t)


def _rotary(a_ref, tab_ref, rows):
    a = a_ref[rows, :].astype(F32)
    up = pltpu.roll(a, LANES - ROT_DIM // 2, axis=1)
    dn = pltpu.roll(a, ROT_DIM // 2, axis=1)
    return (a * tab_ref[0, rows, :] + up * tab_ref[1, rows, :] + dn * tab_ref[2, rows, :]).astype(BF16)


def _attn_kernel(q_ref, k_ref, v_ref, tq_ref, tk_ref, lam_ref, sw_ref, o_ref, kr_ref, v1_ref, s1_ref, s2_ref, *, tk):
    tq = q_ref.shape[0]
    seq = k_ref.shape[0]
    n_kc = seq // tk

    @pl.when(pl.program_id(2) == 0)
    def _():
        v1_ref[:, :LANES] = v_ref[...]
        v1_ref[:, LANES:] = jnp.ones((seq, LANES), BF16)
        for c in range(n_kc):
            rows = slice(c * tk, (c + 1) * tk)
            kr_ref[rows, :] = _rotary(k_ref, tk_ref, rows)

    q = _rotary(q_ref, tq_ref, slice(None))
    lane = lax.broadcasted_iota(jnp.int32, q.shape, 1)
    zero = jnp.zeros_like(q)
    q1 = jnp.where(lane < HEAD_DIM_A, q, zero)
    q2 = jnp.where(lane >= HEAD_DIM_A, q, zero)

    m1 = jnp.full((tq, LANES), -jnp.inf, F32)
    m2 = m1
    for c in range(n_kc):
        kc = kr_ref[c * tk:(c + 1) * tk, :]
        sc1 = _dot_nt(q1, kc)
        sc2 = _dot_nt(q2, kc)
        s1_ref[c] = sc1
        s2_ref[c] = sc2
        for u in range(tk // LANES):
            m1 = jnp.maximum(m1, sc1[:, u * LANES:(u + 1) * LANES])
            m2 = jnp.maximum(m2, sc2[:, u * LANES:(u + 1) * LANES])
    m1 = jnp.max(m1, axis=1, keepdims=True)
    m2 = jnp.max(m2, axis=1, keepdims=True)

    a1 = jnp.zeros((tq, 2 * LANES), F32)
    a2 = a1
    for c in range(n_kc):
        vc = v1_ref[c * tk:(c + 1) * tk, :]
        p1 = jnp.exp2(s1_ref[c] - m1).astype(BF16)
        p2 = jnp.exp2(s2_ref[c] - m2).astype(BF16)
        a1 = a1 + _dot(p1, vc)
        a2 = a2 + _dot(p2, vc)

    lp = lam_ref[...]
    lam = (jnp.exp(jnp.sum(lp[0:1] * lp[1:2], axis=1, keepdims=True))
           - jnp.exp(jnp.sum(lp[2:3] * lp[3:4], axis=1, keepdims=True)) + LAM_INIT)
    o =a1[:, :LANES] / a1[:, LANES:] - lam * (a2[:, :LANES] / a2[:, LANES:])
    ms = jnp.mean(o * o, axis=1, keepdims=True)
    o = o * lax.rsqrt(ms + RMS_EPS) * sw_ref[...] * (1.0 - LAM_INIT)
    o_ref[...] = o.astype(o_ref.dtype)


def _attention(proj3, rot_tab, lam_p, subln_w, *, tq=1024, tk=512):
    batch, seq, _ = proj3.shape
    hw = 2 * HEAD_DIM_A
    kern = functools.partial(_attn_kernel, tk=tk)
    return pl.pallas_call(
        kern,
        grid=(batch, N_HEADS_A, seq // tq),
        in_specs=[
            pl.BlockSpec((None, tq, hw), lambda b, h, i: (b, i, Q_OFF // hw + h)),
            pl.BlockSpec((None, seq, hw), lambda b, h, i: (b, 0, K_OFF // hw + h)),
            pl.BlockSpec((None, seq, hw), lambda b, h, i: (b, 0, V_OFF // hw + h)),
            pl.BlockSpec((3, tq, hw), lambda b, h, i: (0, i, 0)),
            pl.BlockSpec((3, seq, hw), lambda b, h, i: (0, 0, 0), pipeline_mode=pl.Buffered(1)),
            pl.BlockSpec((4, HEAD_DIM_A), lambda b, h, i: (0, 0)),
            pl.BlockSpec((1, hw), lambda b, h, i: (0, 0)),
        ],
        out_specs=pl.BlockSpec((None, tq, hw), lambda b, h, i: (b, i, h)),
        out_shape=jax.ShapeDtypeStruct((batch, seq, ATTN_W), BF16),
        scratch_shapes=[
            pltpu.VMEM((seq, hw), BF16),
            pltpu.VMEM((seq, 2 * LANES), BF16),
            pltpu.VMEM((seq // tk, tq, tk), F32),
            pltpu.VMEM((seq // tk, tq, tk), F32),
        ],
        compiler_params=pltpu.CompilerParams(
            dimension_semantics=("parallel", "parallel", "arbitrary"), vmem_limit_bytes=VMEM_LIMIT),
        name="diff_attn",
    )(proj3, proj3, proj3, rot_tab, rot_tab, lam_p, subln_w)


def _silu(v):
    return v * jax.nn.sigmoid(v)


def _ssd_kernel(z_ref, xs_ref, b_ref, c_ref, dt_ref, cwx_ref, cwb_ref, cwc_ref, cbx_ref, cbb_ref, cbc_ref,
                dtb_ref, alog_ref, dsk_ref, nw_ref, o_ref,
                pad_ref, cv_ref, y_ref, cb_ref, src_ref, cs_ref, st_ref):
    seq = xs_ref.shape[0]
    n_chunks = seq // CHUNK
    halo = SUBLANES
    half = CONV_SSD // 2
    gw = GROUP_W
    w_all = gw + 2 * D_STATE

    n_slab = w_all // LANES
    srcs = (xs_ref, xs_ref, b_ref, c_ref)
    lane0 = (0, LANES, 0, 0)
    zero_halo = jnp.zeros((halo, LANES), F32)
    for s in range(n_slab):
        pad_ref[s, 0:halo, :] = zero_halo
        pad_ref[s, halo + seq:, :] = zero_halo
        pad_ref[s, halo:halo + seq, :] = srcs[s][:, lane0[s]:lane0[s] + LANES].astype(F32)
    cw = jnp.concatenate([cwx_ref[...], cwb_ref[...], cwc_ref[...]], axis=1)
    cbias = jnp.concatenate([cbx_ref[...], cbb_ref[...], cbc_ref[...]], axis=1)
    stride = CONV_STRIDE
    blk = SUBLANES * stride

    def conv_chunk(c, carry):
        t0 = pl.multiple_of(c * CHUNK, CHUNK)
        for s in range(n_slab):
            w_s = cw[:, s * LANES:(s + 1) * LANES]
            b_s = jnp.broadcast_to(cbias[:, s * LANES:(s + 1) * LANES], (SUBLANES, LANES))
            for r in range(0, CHUNK, blk):
                for v in range(stride):
                    base = t0 + (r + v)
                    acc = b_s
                    for k in range(CONV_SSD):
                        acc = acc + pad_ref[s, pl.ds(base + (halo - half + k), SUBLANES, stride=stride), :] * w_s[k:k + 1, :]
                    cv_ref[s, pl.ds(base, SUBLANES, stride=stride), :] = _silu(acc)
        rows = pl.ds(t0, CHUNK)
        cb_ref[c] = _dot_nt(cv_ref[3, rows, :].astype(BF16), cv_ref[2, rows, :].astype(BF16))
        return carry

    lax.fori_loop(0, n_chunks, conv_chunk, 0, unroll=8)

    nh = 2 * HEADS_PER_GROUP
    r_i = lax.broadcasted_iota(jnp.int32, (CHUNK, CHUNK), 0)
    c_i = lax.broadcasted_iota(jnp.int32, (CHUNK, CHUNK), 1)
    upper = (r_i <= c_i).astype(F32)
    lower = (r_i >= c_i).astype(F32)
    raw = dt_ref[...].reshape(n_chunks * nh, CHUNK) + dtb_ref[...]
    dt_all = jnp.maximum(raw, 0.0) + jnp.log1p(jnp.exp(-jnp.abs(raw)))
    da = dt_all * (-jnp.exp(alog_ref[...]))
    row_all = lax.broadcasted_iota(jnp.int32, (n_chunks * nh, CHUNK), 0)
    cs_all = jnp.where(row_all % nh < HEADS_PER_GROUP, _dot_exact(da, upper), _dot_exact(da, lower))
    cs2_all = cs_all * LOG2_E
    cs_ref[...] = cs2_all.reshape(n_chunks, nh, CHUNK)
    src_ref[...] = (cs2_all - jnp.log2(dt_all)).reshape(n_chunks, nh, CHUNK)

    lane_head = lax.broadcasted_iota(jnp.int32, (1, gw), 1) // HEAD_DIM_S
    first_head = lax.broadcasted_iota(jnp.int32, (1, LANES), 1) < HEAD_DIM_S

    def chunk_step(c, reverse):
        r0 = HEADS_PER_GROUP if reverse else 0
        t0 = pl.multiple_of(c * CHUNK, CHUNK)
        rows = pl.ds(t0, CHUNK)
        x = jnp.concatenate([cv_ref[0, rows, :], cv_ref[1, rows, :]], axis=1)
        bm = cv_ref[2, rows, :]
        cm = cv_ref[3, rows, :].astype(BF16)
        cs2 = cs_ref[c]
        src = src_ref[c]
        cbm = cb_ref[c]
        if reverse:
            tot = cs2[:, 0:1]
            keep = r_i <= c_i
        else:
            tot = cs2[:, CHUNK - 1:CHUNK]
            keep = r_i >= c_i
        cs2_t = jnp.transpose(cs2)
        col_b = [jnp.broadcast_to(cs2_t[:, r0 + e:r0 + e + 1], (CHUNK, LANES)) for e in range(HEADS_PER_GROUP)]
        g_out = jnp.exp2(jnp.concatenate(
            [jnp.where(first_head, col_b[2 * v], col_b[2 * v + 1]) for v in range(gw // LANES)], axis=1))
        sel_row = (lax.broadcasted_iota(jnp.int32, (nh, gw), 0) == lane_head + r0).astype(F32)
        cdec = jnp.exp2(jnp.sum(tot * sel_row, axis=0, keepdims=True))
        wrow = jnp.exp2(tot - src)
        bt = jnp.transpose(bm)
        xb = x.astype(BF16)
        zero = jnp.zeros_like(xb)
        ms, bs, xs = [], [], []
        for e in range(HEADS_PER_GROUP):
            r = r0 + e
            seg = col_b[e] - src[r:r + 1, :]
            decay = jnp.exp2(jnp.where(keep, seg, -jnp.inf))
            ms.append((cbm * decay).astype(BF16))
            bs.append((bt * wrow[r:r + 1, :]).astype(BF16))
            xs.append(jnp.where(lane_head == e, xb, zero))
        xcat = jnp.concatenate(xs, axis=0)
        d = 1 if reverse else 0
        st = st_ref[d]
        y = _dot(jnp.concatenate(ms, axis=1), xcat) + _dot(cm, st.astype(BF16)) * g_out
        st_ref[d] = st * cdec + _dot(jnp.concatenate(bs, axis=1), xcat)
        y_ref[d, pl.ds(t0, CHUNK), :] = y

    st_ref[...] = jnp.zeros_like(st_ref)

    def scan(i, carry):
        chunk_step(i, False)
        chunk_step(n_chunks - 1 - i, True)
        return carry

    lax.fori_loop(0, n_chunks, scan, 0, unroll=16)

    def finish(c, carry):
        t0 = pl.multiple_of(c * CHUNK, CHUNK)
        rows = pl.ds(t0, CHUNK)
        xc = jnp.concatenate([cv_ref[0, rows, :], cv_ref[1, rows, :]], axis=1)
        y = y_ref[0, rows, :] + y_ref[1, rows, :] + xc * dsk_ref[...]
        y = y * _silu(z_ref[rows, :].astype(F32))
        ms = jnp.mean(y * y, axis=1, keepdims=True)
        y = y * lax.rsqrt(ms + RMS_EPS) * nw_ref[...]
        o_ref[rows, :] = y.astype(o_ref.dtype)
        return carry

    lax.fori_loop(0, n_chunks, finish, 0, unroll=8)


def _ssd(proj3, dt_t, cw_x, cw_b, cw_c, cb_x, cb_b, cb_c, dtb, alog, dsk, nw):
    batch, seq, _ = proj3.shape
    gw, ns = GROUP_W, D_STATE
    nh = 2 * HEADS_PER_GROUP
    n_chunks = seq // CHUNK
    gspec = lambda w: pl.BlockSpec((1, w), lambda b, g: (0, g))
    return pl.pallas_call(
        _ssd_kernel,
        grid=(batch, N_GROUPS),
        in_specs=[
            pl.BlockSpec((None, seq, gw), lambda b, g: (b, 0, Z_OFF // gw + g)),
            pl.BlockSpec((None, seq, gw), lambda b, g: (b, 0, XS_OFF // gw + g)),
            pl.BlockSpec((None, seq, ns), lambda b, g: (b, 0, B_OFF // ns + g)),
            pl.BlockSpec((None, seq, ns), lambda b, g: (b, 0, C_OFF // ns + g)),
            pl.BlockSpec((None, n_chunks, nh, CHUNK), lambda b, g: (b, 0, g, 0)),
            pl.BlockSpec((CONV_SSD, gw), lambda b, g: (0, g)),
            pl.BlockSpec((CONV_SSD, ns), lambda b, g: (0, g)),
            pl.BlockSpec((CONV_SSD, ns), lambda b, g: (0, g)),
            gspec(gw), gspec(ns), gspec(ns),
            pl.BlockSpec((n_chunks * nh, 1), lambda b, g: (g, 0)),
            pl.BlockSpec((n_chunks * nh, 1), lambda b, g: (g, 0)),
            gspec(gw), gspec(gw),
        ],
        out_specs=pl.BlockSpec((None, seq, gw), lambda b, g: (b, 0, g)),
        out_shape=jax.ShapeDtypeStruct((batch, seq, D_INNER), BF16),
        scratch_shapes=[
            pltpu.VMEM(((gw + 2 * ns) // LANES, seq + 2 * SUBLANES, LANES), F32),
            pltpu.VMEM(((gw + 2 * ns) // LANES, seq, LANES), F32),
            pltpu.VMEM((2, seq, gw), F32),
            pltpu.VMEM((n_chunks, CHUNK, CHUNK), F32),
            pltpu.VMEM((n_chunks, nh, CHUNK), F32),
            pltpu.VMEM((n_chunks, nh, CHUNK), F32),
            pltpu.VMEM((2, ns, gw), F32),
        ],
        compiler_params=pltpu.CompilerParams(
            dimension_semantics=("parallel", "parallel"), vmem_limit_bytes=VMEM_LIMIT),
        name="ssd",
    )(proj3, proj3, proj3, proj3, dt_t, cw_x, cw_b, cw_c, cb_x, cb_b, cb_c, dtb, alog, dsk, nw)


def _layer_norm(v, g, b):
    mu = jnp.mean(v, axis=1, keepdims=True)
    d = v - mu
    var = jnp.mean(d * d, axis=1, keepdims=True)
    return d * lax.rsqrt(var + LN_EPS) * g + b


def _merge_kernel(x_ref, at_ref, sd_ref, ga_ref, gs_ref, wpa_ref, wps_ref, wo_ref, bg_ref, g_ref, b_ref, o_ref):
    bg = bg_ref[...]
    g_a = jax.nn.sigmoid(ga_ref[...].astype(F32) + bg[:, :D_MODEL])
    g_s = jax.nn.sigmoid(gs_ref[...].astype(F32) + bg[:, D_MODEL:])
    merged = g_a * _dot(at_ref[...], wpa_ref[...]) + g_s * _dot(sd_ref[...], wps_ref[...])
    y = ALPHA * x_ref[...] + _dot(merged.astype(BF16), wo_ref[...])
    o_ref[...] = _layer_norm(y, g_ref[...], b_ref[...])


def _merge(x2, attn2, ssd2, proj2, wpa, wps, wo, bg, ln_g, ln_b, *, tm=512):
    t = x2.shape[0]
    full = lambda a: pl.BlockSpec(a.shape, lambda i: (0, 0))
    return pl.pallas_call(
        _merge_kernel,
        grid=(t // tm,),
        in_specs=[
            pl.BlockSpec((tm, D_MODEL), lambda i: (i, 0)),
            pl.BlockSpec((tm, ATTN_W), lambda i: (i, 0)),
            pl.BlockSpec((tm, D_INNER), lambda i: (i, 0)),
            pl.BlockSpec((tm, D_MODEL), lambda i: (i, GATE_OFF // D_MODEL)),
            pl.BlockSpec((tm, D_MODEL), lambda i: (i, GATE_OFF // D_MODEL + 1)),
            full(wpa), full(wps), full(wo), full(bg), full(ln_g), full(ln_b),
        ],
        out_specs=pl.BlockSpec((tm, D_MODEL), lambda i: (i, 0)),
        out_shape=jax.ShapeDtypeStruct((t, D_MODEL), F32),
        compiler_params=pltpu.CompilerParams(
            dimension_semantics=("parallel",), vmem_limit_bytes=VMEM_LIMIT),
        name="merge_ln1",
    )(x2, attn2, ssd2, proj2, proj2, wpa, wps, wo, bg, ln_g, ln_b)


def _ffn_kernel(x_ref, prev_ref, next_ref, wup_ref, cw_ref, cb_ref, wdn_ref, g_ref, b_ref, o_ref, xh_ref,
                ug0_ref, uu0_ref, ug1_ref, uu1_ref, *, tiles_per_seq, tf):
    i = pl.program_id(0)
    tm = x_ref.shape[0]
    halo = SUBLANES
    x = x_ref[...]
    first = (i % tiles_per_seq) == 0
    last = (i % tiles_per_seq) == tiles_per_seq - 1
    zero = jnp.zeros((halo, D_MODEL), F32)
    xh_ref[0:halo, :] = jnp.where(first, zero, prev_ref[...]).astype(BF16)
    xh_ref[halo:halo + tm, :] = x.astype(BF16)
    xh_ref[halo + tm:, :] = jnp.where(last, zero, next_ref[...]).astype(BF16)
    xh = xh_ref[...]

    def conv(u_ref, col):
        w = cw_ref[:, col:col + tf]
        out = cb_ref[:, col:col + tf] + u_ref[halo - 1:halo - 1 + tm, :] * w[0:1]
        out = out + u_ref[halo:halo + tm, :] * w[1:2]
        return out + u_ref[halo + 1:halo + 1 + tm, :] * w[2:3]

    bufs = ((ug0_ref, uu0_ref), (ug1_ref, uu1_ref))

    def up(j):
        ug_ref, uu_ref = bufs[j % 2]
        ug_ref[...] = _dot(xh, wup_ref[:, j * tf:(j + 1) * tf])
        uu_ref[...] = _dot(xh, wup_ref[:, D_FF + j * tf:D_FF + (j + 1) * tf])

    n_f = D_FF // tf
    acc = jnp.zeros((tm, D_MODEL), F32)
    up(0)
    for j in range(n_f):
        if j + 1 < n_f:
            up(j + 1)
        ug_ref, uu_ref = bufs[j % 2]
        act = (_silu(conv(ug_ref, j * tf)) * conv(uu_ref, D_FF + j * tf)).astype(BF16)
        acc = acc + _dot(act, wdn_ref[j * tf:(j + 1) * tf, :])
    o_ref[...] = _layer_norm(ALPHA * x + acc, g_ref[...], b_ref[...])


def _ffn(x1, wup, cw, cb, wdn, ln_g, ln_b, *, seq, tm=512, tf=256):
    t = x1.shape[0]
    hb = tm // SUBLANES
    n_hb = t // SUBLANES
    full = lambda a: pl.BlockSpec(a.shape, lambda i: (0, 0))
    kern = functools.partial(_ffn_kernel, tiles_per_seq=seq // tm, tf=tf)
    return pl.pallas_call(
        kern,
        grid=(t // tm,),
        in_specs=[
            pl.BlockSpec((tm, D_MODEL), lambda i: (i, 0)),
            pl.BlockSpec((SUBLANES, D_MODEL), lambda i: (jnp.maximum(i * hb - 1, 0), 0)),
            pl.BlockSpec((SUBLANES, D_MODEL), lambda i: (jnp.minimum((i + 1) * hb, n_hb - 1), 0)),
            full(wup), full(cw), full(cb), full(wdn), full(ln_g), full(ln_b),
        ],
        out_specs=pl.BlockSpec((tm, D_MODEL), lambda i: (i, 0)),
        out_shape=jax.ShapeDtypeStruct((t, D_MODEL), F32),
        scratch_shapes=([pltpu.VMEM((tm + 2 * SUBLANES, D_MODEL), BF16)]
                        + [pltpu.VMEM((tm + 2 * SUBLANES, tf), F32)] * 4),
        compiler_params=pltpu.CompilerParams(
            dimension_semantics=("parallel",), vmem_limit_bytes=VMEM_LIMIT),
        name="ffn_ln2",
    )(x1, x1, x1, wup, cw, cb, wdn, ln_g, ln_b)


def _rotary_tables(seq):
    half = ROT_DIM // 2
    pos = np.arange(seq, dtype=np.float32)
    inv_freq = np.power(np.float32(ROPE_THETA), -np.arange(0, ROT_DIM, 2, dtype=np.float32) / np.float32(ROT_DIM))
    ang = (pos[:, None] * inv_freq[None, :].astype(np.float32)).astype(np.float32)
    cos, sin = np.cos(ang).astype(np.float32), np.sin(ang).astype(np.float32)
    pad = np.zeros((seq, HEAD_DIM_A - ROT_DIM), np.float32)
    zer = np.zeros((seq, half), np.float32)
    c64 = np.concatenate([cos, cos, pad + 1.0], axis=1)
    sa64 = np.concatenate([-sin, zer, pad], axis=1)
    sb64 = np.concatenate([zer, sin, pad], axis=1)
    tile2 = lambda a: np.concatenate([a, a], axis=1)
    return jnp.asarray(np.stack([tile2(c64), tile2(sa64), tile2(sb64)]))


def _group_rows(p, n_chunks):
    nh = 2 * HEADS_PER_GROUP
    pg = p.reshape(2, N_GROUPS, HEADS_PER_GROUP).transpose(1, 0, 2).reshape(N_GROUPS, 1, nh)
    return jnp.broadcast_to(pg, (N_GROUPS, n_chunks, nh)).reshape(N_GROUPS * n_chunks * nh, 1)


def kernel(x, w_in, b_gate, lambda_q1, lambda_k1, lambda_q2, lambda_k2, attn_subln_w, conv_ssd_w, conv_ssd_b,
           dt_bias, a_log, d_skip, ssd_norm_w, w_proj_attn, w_proj_ssd, w_out, ln1_g, ln1_b, w_up, conv_ffn_w,
           conv_ffn_b, w_down, ln2_g, ln2_b):
    batch, seq, _ = x.shape
    t = batch * seq
    l = 0
    xbc_w = D_INNER + 2 * N_GROUPS * D_STATE
    dt_off = Z_OFF + D_INNER + xbc_w
    w = w_in[l]
    w_a = w.astype(BF16)
    w_b = w[:, dt_off + 2 * N_HEADS_S:].astype(BF16)
    w_dt = w[:, dt_off:dt_off + 2 * N_HEADS_S]
    w_dt_t = (w_dt.reshape(D_MODEL, 2, N_GROUPS, HEADS_PER_GROUP).transpose(2, 1, 3, 0)
              .reshape(2 * N_HEADS_S, D_MODEL).astype(BF16))
    rot_tab = _rotary_tables(seq)

    x2 = x.reshape(t, D_MODEL)
    proj2, dt_t = _in_proj(x2, w_a, w_b, w_dt_t, batch=batch, seq=seq)
    proj3 = proj2.reshape(batch, seq, PROJ_W)

    lam_p = jnp.stack([lambda_q1[l], lambda_k1[l], lambda_q2[l], lambda_k2[l]]).astype(F32)
    attn = _attention(proj3, rot_tab, lam_p, attn_subln_w[l].reshape(1, 2 * HEAD_DIM_A))

    cw = conv_ssd_w[l]
    cb = conv_ssd_b[l].reshape(1, xbc_w)
    nb = N_GROUPS * D_STATE
    ssd = _ssd(proj3, dt_t,
               cw[:, :D_INNER], cw[:, D_INNER:D_INNER + nb], cw[:, D_INNER + nb:],
               cb[:, :D_INNER], cb[:, D_INNER:D_INNER + nb], cb[:, D_INNER + nb:],
               _group_rows(dt_bias[l], seq // CHUNK), _group_rows(a_log[l], seq // CHUNK),
               jnp.repeat(d_skip[l], HEAD_DIM_S).reshape(1, D_INNER), ssd_norm_w[l].reshape(1, D_INNER))

    x1 = _merge(x2, attn.reshape(t, ATTN_W), ssd.reshape(t, D_INNER), proj2,
                w_proj_attn[l].astype(BF16), w_proj_ssd[l].astype(BF16), w_out[l].astype(BF16),
                b_gate[l].reshape(1, 2 * D_MODEL), ln1_g[l].reshape(1, D_MODEL), ln1_b[l].reshape(1, D_MODEL))

    out = _ffn(x1, w_up[l].astype(BF16), conv_ffn_w[l], conv_ffn_b[l].reshape(1, 2 * D_FF),
               w_down[l].astype(BF16), ln2_g[l].reshape(1, D_MODEL), ln2_b[l].reshape(1, D_MODEL), seq=seq)
    return out.reshape(batch, seq, D_MODEL)
```

```python
import functools
import math

import jax
import jax.numpy as jnp
import numpy as np
from jax import lax
from jax.experimental import pallas as pl
from jax.experimental.pallas import tpu as pltpu

F32 = jnp.float32
BF16 = jnp.bfloat16

D_MODEL = 1024
N_HEADS_A = 8
HEAD_DIM_A = 64
ROT_DIM = HEAD_DIM_A // 4
ROPE_THETA = 500000.0
ATTN_W = N_HEADS_A * 2 * HEAD_DIM_A
D_INNER = 2 * D_MODEL
HEAD_DIM_S = 64
N_HEADS_S = D_INNER // HEAD_DIM_S
N_GROUPS = 8
HEADS_PER_GROUP = N_HEADS_S // N_GROUPS
GROUP_W = HEADS_PER_GROUP * HEAD_DIM_S
D_STATE = 128
CONV_SSD = 5
CHUNK = 128
D_FF = 2816
DEPTH = 1
ALPHA = (2 * DEPTH) ** 0.25
LN_EPS = 1e-5
RMS_EPS = 1e-5
LAM_INIT = 0.8 - 0.6 * math.exp(-0.3 * 0)
LOG2_E = math.log2(math.e)

LANES = 128
SUBLANES = 8
CONV_STRIDE = 4
VMEM_LIMIT = 56 * 1024 * 1024

Q_OFF, K_OFF, V_OFF = 0, ATTN_W, 2 * ATTN_W
Z_OFF = 3 * ATTN_W
XS_OFF = Z_OFF + D_INNER
B_OFF = XS_OFF + D_INNER
C_OFF = B_OFF + N_GROUPS * D_STATE
GATE_OFF = C_OFF + N_GROUPS * D_STATE
PROJ_W = GATE_OFF + 2 * D_MODEL


def _dot(a, b):
    return jnp.dot(a, b, preferred_element_type=F32)


def _dot_nt(a, b):
    return lax.dot_general(a, b, (((1,), (1,)), ((), ())), preferred_element_type=F32)


def _dot_exact(a, b):
    return jnp.dot(a, b, preferred_element_type=F32, precision=lax.Precision.HIGHEST)


def _in_proj_kernel(x_ref, wa_ref, wb_ref, wdt_ref, o_ref, dt_ref, xb_ref, *, n_q, n_a):
    j = pl.program_id(1)

    @pl.when(j == 0)
    def _():
        xb = x_ref[...].astype(BF16)
        xb_ref[...] = xb
        dt_t = _dot_nt(wdt_ref[...], xb)
        for c in range(dt_ref.shape[0]):
            dt_ref[c] = dt_t[:, c * CHUNK:(c + 1) * CHUNK]

    @pl.when(j < n_q)
    def _():
        o_ref[...] = (_dot(xb_ref[...], wa_ref[...]) * (HEAD_DIM_A ** -0.5 * LOG2_E)).astype(o_ref.dtype)

    @pl.when((j >= n_q) & (j < n_a))
    def _():
        o_ref[...] = _dot(xb_ref[...], wa_ref[...]).astype(o_ref.dtype)

    @pl.when(j >= n_a)
    def _():
        o_ref[...] = _dot(xb_ref[...], wb_ref[...]).astype(o_ref.dtype)


def _in_proj(x2, w_a, w_b, w_dt_t, *, batch, seq, tm=2048, tn=1024):
    t = x2.shape[0]
    n_seq_tiles = seq // tm
    n_dt = w_dt_t.shape[0]
    n_a = GATE_OFF // tn
    kern = functools.partial(_in_proj_kernel, n_q=ATTN_W // tn, n_a=n_a)
    return pl.pallas_call(
        kern,
        grid=(t // tm, PROJ_W // tn),
        in_specs=[
            pl.BlockSpec((tm, D_MODEL), lambda i, j: (i, 0)),
            pl.BlockSpec((D_MODEL, tn), lambda i, j: (0, jnp.minimum(j, n_a - 1))),
            pl.BlockSpec((D_MODEL, tn), lambda i, j: (0, jnp.maximum(j - n_a, 0))),
            pl.BlockSpec((n_dt, D_MODEL), lambda i, j: (0, 0)),
        ],
        out_specs=[
            pl.BlockSpec((tm, tn), lambda i, j: (i, j)),
            pl.BlockSpec((None, tm // CHUNK, n_dt, CHUNK),
                         lambda i, j: (i // n_seq_tiles, i % n_seq_tiles, 0, 0)),
        ],
        out_shape=[
            jax.ShapeDtypeStruct((t, PROJ_W), BF16),
            jax.ShapeDtypeStruct((batch, seq // CHUNK, n_dt, CHUNK), F32),
        ],
        scratch_shapes=[pltpu.VMEM((tm, D_MODEL), BF16)],
        compiler_params=pltpu.CompilerParams(
            dimension_semantics=("parallel", "arbitrary"), vmem_limit_bytes=VMEM_LIMIT),
        name="in_proj",
    )(x2, w_a, w_b, w_dt_t)


def _rotary(a_ref, tab_ref, rows):
    a = a_ref[rows, :].astype(F32)
    up = pltpu.roll(a, LANES - ROT_DIM // 2, axis=1)
    dn = pltpu.roll(a, ROT_DIM // 2, axis=1)
    return (a * tab_ref[0, rows, :] + up * tab_ref[1, rows, :] + dn * tab_ref[2, rows, :]).astype(BF16)


def _attn_kernel(q_ref, k_ref, v_ref, tq_ref, tk_ref, lam_ref, sw_ref, o_ref, kr_ref, v1_ref, s1_ref, s2_ref, *, tk):
    tq = q_ref.shape[0]
    seq = k_ref.shape[0]
    n_kc = seq // tk

    @pl.when(pl.program_id(2) == 0)
    def _():
        v1_ref[:, :LANES] = v_ref[...]
        v1_ref[:, LANES:] = jnp.ones((seq, LANES), BF16)
        for c in range(n_kc):
            rows = slice(c * tk, (c + 1) * tk)
            kr_ref[rows, :] = _rotary(k_ref, tk_ref, rows)

    q = _rotary(q_ref, tq_ref, slice(None))
    lane = lax.broadcasted_iota(jnp.int32, q.shape, 1)
    zero = jnp.zeros_like(q)
    q1 = jnp.where(lane < HEAD_DIM_A, q, zero)
    q2 = jnp.where(lane >= HEAD_DIM_A, q, zero)

    m1 = jnp.full((tq, LANES), -jnp.inf, F32)
    m2 = m1
    for c in range(n_kc):
        kc = kr_ref[c * tk:(c + 1) * tk, :]
        sc1 = _dot_nt(q1, kc)
        sc2 = _dot_nt(q2, kc)
        s1_ref[c] = sc1
        s2_ref[c] = sc2
        for u in range(tk // LANES):
            m1 = jnp.maximum(m1, sc1[:, u * LANES:(u + 1) * LANES])
            m2 = jnp.maximum(m2, sc2[:, u * LANES:(u + 1) * LANES])
    m1 = jnp.max(m1, axis=1, keepdims=True)
    m2 = jnp.max(m2, axis=1, keepdims=True)

    a1 = jnp.zeros((tq, 2 * LANES), F32)
    a2 = a1
    for c in range(n_kc):
        vc = v1_ref[c * tk:(c + 1) * tk, :]
        p1 = jnp.exp2(s1_ref[c] - m1).astype(BF16)
        p2 = jnp.exp2(s2_ref[c] - m2).astype(BF16)
        a1 = a1 + _dot(p1, vc)
        a2 = a2 + _dot(p2, vc)

    lp = lam_ref[...]
    lam = (jnp.exp(jnp.sum(lp[0:1] * lp[1:2], axis=1, keepdims=True))
           - jnp.exp(jnp.sum(lp[2:3] * lp[3:4], axis=1, keepdims=True)) + LAM_INIT)
    o =a1[:, :LANES] / a1[:, LANES:] - lam * (a2[:, :LANES] / a2[:, LANES:])
    ms = jnp.mean(o * o, axis=1, keepdims=True)
    o = o * lax.rsqrt(ms + RMS_EPS) * sw_ref[...] * (1.0 - LAM_INIT)
    o_ref[...] = o.astype(o_ref.dtype)


def _attention(proj3, rot_tab, lam_p, subln_w, *, tq=1024, tk=512):
    batch, seq, _ = proj3.shape
    hw = 2 * HEAD_DIM_A
    kern = functools.partial(_attn_kernel, tk=tk)
    return pl.pallas_call(
        kern,
        grid=(batch, N_HEADS_A, seq // tq),
        in_specs=[
            pl.BlockSpec((None, tq, hw), lambda b, h, i: (b, i, Q_OFF // hw + h)),
            pl.BlockSpec((None, seq, hw), lambda b, h, i: (b, 0, K_OFF // hw + h)),
            pl.BlockSpec((None, seq, hw), lambda b, h, i: (b, 0, V_OFF // hw + h)),
            pl.BlockSpec((3, tq, hw), lambda b, h, i: (0, i, 0)),
            pl.BlockSpec((3, seq, hw), lambda b, h, i: (0, 0, 0), pipeline_mode=pl.Buffered(1)),
            pl.BlockSpec((4, HEAD_DIM_A), lambda b, h, i: (0, 0)),
            pl.BlockSpec((1, hw), lambda b, h, i: (0, 0)),
        ],
        out_specs=pl.BlockSpec((None, tq, hw), lambda b, h, i: (b, i, h)),
        out_shape=jax.ShapeDtypeStruct((batch, seq, ATTN_W), BF16),
        scratch_shapes=[
            pltpu.VMEM((seq, hw), BF16),
            pltpu.VMEM((seq, 2 * LANES), BF16),
            pltpu.VMEM((seq // tk, tq, tk), F32),
            pltpu.VMEM((seq // tk, tq, tk), F32),
        ],
        compiler_params=pltpu.CompilerParams(
            dimension_semantics=("parallel", "parallel", "arbitrary"), vmem_limit_bytes=VMEM_LIMIT),
        name="diff_attn",
    )(proj3, proj3, proj3, rot_tab, rot_tab, lam_p, subln_w)


def _silu(v):
    return v * jax.nn.sigmoid(v)


def _ssd_kernel(z_ref, xs_ref, b_ref, c_ref, dt_ref, cwx_ref, cwb_ref, cwc_ref, cbx_ref, cbb_ref, cbc_ref,
                dtb_ref, alog_ref, dsk_ref, nw_ref, o_ref,
                pad_ref, cv_ref, y_ref, cb_ref, src_ref, cs_ref, st_ref):
    seq = xs_ref.shape[0]
    n_chunks = seq // CHUNK
    halo = SUBLANES
    half = CONV_SSD // 2
    gw = GROUP_W
    w_all = gw + 2 * D_STATE

    n_slab = w_all // LANES
    srcs = (xs_ref, xs_ref, b_ref, c_ref)
    lane0 = (0, LANES, 0, 0)
    zero_halo = jnp.zeros((halo, LANES), F32)
    for s in range(n_slab):
        pad_ref[s, 0:halo, :] = zero_halo
        pad_ref[s, halo + seq:, :] = zero_halo
        pad_ref[s, halo:halo + seq, :] = srcs[s][:, lane0[s]:lane0[s] + LANES].astype(F32)
    cw = jnp.concatenate([cwx_ref[...], cwb_ref[...], cwc_ref[...]], axis=1)
    cbias = jnp.concatenate([cbx_ref[...], cbb_ref[...], cbc_ref[...]], axis=1)
    stride = CONV_STRIDE
    blk = SUBLANES * stride

    def conv_chunk(c, carry):
        t0 = pl.multiple_of(c * CHUNK, CHUNK)
        for s in range(n_slab):
            w_s = cw[:, s * LANES:(s + 1) * LANES]
            b_s = jnp.broadcast_to(cbias[:, s * LANES:(s + 1) * LANES], (SUBLANES, LANES))
            for r in range(0, CHUNK, blk):
                for v in range(stride):
                    base = t0 + (r + v)
                    acc = b_s
                    for k in range(CONV_SSD):
                        acc = acc + pad_ref[s, pl.ds(base + (halo - half + k), SUBLANES, stride=stride), :] * w_s[k:k + 1, :]
                    cv_ref[s, pl.ds(base, SUBLANES, stride=stride), :] = _silu(acc)
        rows = pl.ds(t0, CHUNK)
        cb_ref[c] = _dot_nt(cv_ref[3, rows, :].astype(BF16), cv_ref[2, rows, :].astype(BF16))
        return carry

    lax.fori_loop(0, n_chunks, conv_chunk, 0, unroll=8)

    nh = 2 * HEADS_PER_GROUP
    r_i = lax.broadcasted_iota(jnp.int32, (CHUNK, CHUNK), 0)
    c_i = lax.broadcasted_iota(jnp.int32, (CHUNK, CHUNK), 1)
    upper = (r_i <= c_i).astype(F32)
    lower = (r_i >= c_i).astype(F32)
    raw = dt_ref[...].reshape(n_chunks * nh, CHUNK) + dtb_ref[...]
    dt_all = jnp.maximum(raw, 0.0) + jnp.log1p(jnp.exp(-jnp.abs(raw)))
    da = dt_all * (-jnp.exp(alog_ref[...]))
    row_all = lax.broadcasted_iota(jnp.int32, (n_chunks * nh, CHUNK), 0)
    cs_all = jnp.where(row_all % nh < HEADS_PER_GROUP, _dot_exact(da, upper), _dot_exact(da, lower))
    cs2_all = cs_all * LOG2_E
    cs_ref[...] = cs2_all.reshape(n_chunks, nh, CHUNK)
    src_ref[...] = (cs2_all - jnp.log2(dt_all)).reshape(n_chunks, nh, CHUNK)

    lane_head = lax.broadcasted_iota(jnp.int32, (1, gw), 1) // HEAD_DIM_S
    first_head = lax.broadcasted_iota(jnp.int32, (1, LANES), 1) < HEAD_DIM_S

    def chunk_step(c, reverse):
        r0 = HEADS_PER_GROUP if reverse else 0
        t0 = pl.multiple_of(c * CHUNK, CHUNK)
        rows = pl.ds(t0, CHUNK)
        x = jnp.concatenate([cv_ref[0, rows, :], cv_ref[1, rows, :]], axis=1)
        bm = cv_ref[2, rows, :]
        cm = cv_ref[3, rows, :].astype(BF16)
        cs2 = cs_ref[c]
        src = src_ref[c]
        cbm = cb_ref[c]
        if reverse:
            tot = cs2[:, 0:1]
            keep = r_i <= c_i
        else:
            tot = cs2[:, CHUNK - 1:CHUNK]
            keep = r_i >= c_i
        cs2_t = jnp.transpose(cs2)
        col_b = [jnp.broadcast_to(cs2_t[:, r0 + e:r0 + e + 1], (CHUNK, LANES)) for e in range(HEADS_PER_GROUP)]
        g_out = jnp.exp2(jnp.concatenate(
            [jnp.where(first_head, col_b[2 * v], col_b[2 * v + 1]) for v in range(gw // LANES)], axis=1))
        sel_row = (lax.broadcasted_iota(jnp.int32, (nh, gw), 0) == lane_head + r0).astype(F32)
        cdec = jnp.exp2(jnp.sum(tot * sel_row, axis=0, keepdims=True))
        wrow = jnp.exp2(tot - src)
        bt = jnp.transpose(bm)
        xb = x.astype(BF16)
        zero = jnp.zeros_like(xb)
        ms, bs, xs = [], [], []
        for e in range(HEADS_PER_GROUP):
            r = r0 + e
            seg = col_b[e] - src[r:r + 1, :]
            decay = jnp.exp2(jnp.where(keep, seg, -jnp.inf))
            ms.append((cbm * decay).astype(BF16))
            bs.append((bt * wrow[r:r + 1, :]).astype(BF16))
            xs.append(jnp.where(lane_head == e, xb, zero))
        xcat = jnp.concatenate(xs, axis=0)
        d = 1 if reverse else 0
        st = st_ref[d]
        y = _dot(jnp.concatenate(ms, axis=1), xcat) + _dot(cm, st.astype(BF16)) * g_out
        st_ref[d] = st * cdec + _dot(jnp.concatenate(bs, axis=1), xcat)
        y_ref[d, pl.ds(t0, CHUNK), :] = y

    st_ref[...] = jnp.zeros_like(st_ref)

    def scan(i, carry):
        chunk_step(i, False)
        chunk_step(n_chunks - 1 - i, True)
        return carry

    lax.fori_loop(0, n_chunks, scan, 0, unroll=32)

    def finish(c, carry):
        t0 = pl.multiple_of(c * CHUNK, CHUNK)
        rows = pl.ds(t0, CHUNK)
        xc = jnp.concatenate([cv_ref[0, rows, :], cv_ref[1, rows, :]], axis=1)
        y = y_ref[0, rows, :] + y_ref[1, rows, :] + xc * dsk_ref[...]
        y = y * _silu(z_ref[rows, :].astype(F32))
        ms = jnp.mean(y * y, axis=1, keepdims=True)
        y = y * lax.rsqrt(ms + RMS_EPS) * nw_ref[...]
        o_ref[rows, :] = y.astype(o_ref.dtype)
        return carry

    lax.fori_loop(0, n_chunks, finish, 0, unroll=8)


def _ssd(proj3, dt_t, cw_x, cw_b, cw_c, cb_x, cb_b, cb_c, dtb, alog, dsk, nw):
    batch, seq, _ = proj3.shape
    gw, ns = GROUP_W, D_STATE
    nh = 2 * HEADS_PER_GROUP
    n_chunks = seq // CHUNK
    gspec = lambda w: pl.BlockSpec((1, w), lambda b, g: (0, g))
    return pl.pallas_call(
        _ssd_kernel,
        grid=(batch, N_GROUPS),
        in_specs=[
            pl.BlockSpec((None, seq, gw), lambda b, g: (b, 0, Z_OFF // gw + g)),
            pl.BlockSpec((None, seq, gw), lambda b, g: (b, 0, XS_OFF // gw + g)),
            pl.BlockSpec((None, seq, ns), lambda b, g: (b, 0, B_OFF // ns + g)),
            pl.BlockSpec((None, seq, ns), lambda b, g: (b, 0, C_OFF // ns + g)),
            pl.BlockSpec((None, n_chunks, nh, CHUNK), lambda b, g: (b, 0, g, 0)),
            pl.BlockSpec((CONV_SSD, gw), lambda b, g: (0, g)),
            pl.BlockSpec((CONV_SSD, ns), lambda b, g: (0, g)),
            pl.BlockSpec((CONV_SSD, ns), lambda b, g: (0, g)),
            gspec(gw), gspec(ns), gspec(ns),
            pl.BlockSpec((n_chunks * nh, 1), lambda b, g: (g, 0)),
            pl.BlockSpec((n_chunks * nh, 1), lambda b, g: (g, 0)),
            gspec(gw), gspec(gw),
        ],
        out_specs=pl.BlockSpec((None, seq, gw), lambda b, g: (b, 0, g)),
        out_shape=jax.ShapeDtypeStruct((batch, seq, D_INNER), BF16),
        scratch_shapes=[
            pltpu.VMEM(((gw + 2 * ns) // LANES, seq + 2 * SUBLANES, LANES), F32),
            pltpu.VMEM(((gw + 2 * ns) // LANES, seq, LANES), F32),
            pltpu.VMEM((2, seq, gw), F32),
            pltpu.VMEM((n_chunks, CHUNK, CHUNK), F32),
            pltpu.VMEM((n_chunks, nh, CHUNK), F32),
            pltpu.VMEM((n_chunks, nh, CHUNK), F32),
            pltpu.VMEM((2, ns, gw), F32),
        ],
        compiler_params=pltpu.CompilerParams(
            dimension_semantics=("parallel", "parallel"), vmem_limit_bytes=VMEM_LIMIT),
        name="ssd",
    )(proj3, proj3, proj3, proj3, dt_t, cw_x, cw_b, cw_c, cb_x, cb_b, cb_c, dtb, alog, dsk, nw)


def _layer_norm(v, g, b):
    mu = jnp.mean(v, axis=1, keepdims=True)
    d = v - mu
    var = jnp.mean(d * d, axis=1, keepdims=True)
    return d * lax.rsqrt(var + LN_EPS) * g + b


def _merge_kernel(x_ref, at_ref, sd_ref, ga_ref, gs_ref, wpa_ref, wps_ref, wo_ref, bg_ref, g_ref, b_ref, o_ref):
    bg = bg_ref[...]
    g_a = jax.nn.sigmoid(ga_ref[...].astype(F32) + bg[:, :D_MODEL])
    g_s = jax.nn.sigmoid(gs_ref[...].astype(F32) + bg[:, D_MODEL:])
    merged = g_a * _dot(at_ref[...], wpa_ref[...]) + g_s * _dot(sd_ref[...], wps_ref[...])
    y = ALPHA * x_ref[...] + _dot(merged.astype(BF16), wo_ref[...])
    o_ref[...] = _layer_norm(y, g_ref[...], b_ref[...])


def _merge(x2, attn2, ssd2, proj2, wpa, wps, wo, bg, ln_g, ln_b, *, tm=512):
    t = x2.shape[0]
    full = lambda a: pl.BlockSpec(a.shape, lambda i: (0, 0))
    return pl.pallas_call(
        _merge_kernel,
        grid=(t // tm,),
        in_specs=[
            pl.BlockSpec((tm, D_MODEL), lambda i: (i, 0)),
            pl.BlockSpec((tm, ATTN_W), lambda i: (i, 0)),
            pl.BlockSpec((tm, D_INNER), lambda i: (i, 0)),
            pl.BlockSpec((tm, D_MODEL), lambda i: (i, GATE_OFF // D_MODEL)),
            pl.BlockSpec((tm, D_MODEL), lambda i: (i, GATE_OFF // D_MODEL + 1)),
            full(wpa), full(wps), full(wo), full(bg), full(ln_g), full(ln_b),
        ],
        out_specs=pl.BlockSpec((tm, D_MODEL), lambda i: (i, 0)),
        out_shape=jax.ShapeDtypeStruct((t, D_MODEL), F32),
        compiler_params=pltpu.CompilerParams(
            dimension_semantics=("parallel",), vmem_limit_bytes=VMEM_LIMIT),
        name="merge_ln1",
    )(x2, attn2, ssd2, proj2, proj2, wpa, wps, wo, bg, ln_g, ln_b)


def _ffn_kernel(x_ref, prev_ref, next_ref, wup_ref, cw_ref, cb_ref, wdn_ref, g_ref, b_ref, o_ref, xh_ref,
                ug0_ref, uu0_ref, ug1_ref, uu1_ref, *, tiles_per_seq, tf):
    i = pl.program_id(0)
    tm = x_ref.shape[0]
    halo = SUBLANES
    x = x_ref[...]
    first = (i % tiles_per_seq) == 0
    last = (i % tiles_per_seq) == tiles_per_seq - 1
    zero = jnp.zeros((halo, D_MODEL), F32)
    xh_ref[0:halo, :] = jnp.where(first, zero, prev_ref[...]).astype(BF16)
    xh_ref[halo:halo + tm, :] = x.astype(BF16)
    xh_ref[halo + tm:, :] = jnp.where(last, zero, next_ref[...]).astype(BF16)
    xh = xh_ref[...]

    def conv(u_ref, col):
        w = cw_ref[:, col:col + tf]
        out = cb_ref[:, col:col + tf] + u_ref[halo - 1:halo - 1 + tm, :] * w[0:1]
        out = out + u_ref[halo:halo + tm, :] * w[1:2]
        return out + u_ref[halo + 1:halo + 1 + tm, :] * w[2:3]

    bufs = ((ug0_ref, uu0_ref), (ug1_ref, uu1_ref))

    def up(j):
        ug_ref, uu_ref = bufs[j % 2]
        ug_ref[...] = _dot(xh, wup_ref[:, j * tf:(j + 1) * tf])
        uu_ref[...] = _dot(xh, wup_ref[:, D_FF + j * tf:D_FF + (j + 1) * tf])

    n_f = D_FF // tf
    acc = jnp.zeros((tm, D_MODEL), F32)
    up(0)
    for j in range(n_f):
        if j + 1 < n_f:
            up(j + 1)
        ug_ref, uu_ref = bufs[j % 2]
        act = (_silu(conv(ug_ref, j * tf)) * conv(uu_ref, D_FF + j * tf)).astype(BF16)
        acc = acc + _dot(act, wdn_ref[j * tf:(j + 1) * tf, :])
    o_ref[...] = _layer_norm(ALPHA * x + acc, g_ref[...], b_ref[...])


def _ffn(x1, wup, cw, cb, wdn, ln_g, ln_b, *, seq, tm=512, tf=256):
    t = x1.shape[0]
    hb = tm // SUBLANES
    n_hb = t // SUBLANES
    full = lambda a: pl.BlockSpec(a.shape, lambda i: (0, 0))
    kern = functools.partial(_ffn_kernel, tiles_per_seq=seq // tm, tf=tf)
    return pl.pallas_call(
        kern,
        grid=(t // tm,),
        in_specs=[
            pl.BlockSpec((tm, D_MODEL), lambda i: (i, 0)),
            pl.BlockSpec((SUBLANES, D_MODEL), lambda i: (jnp.maximum(i * hb - 1, 0), 0)),
            pl.BlockSpec((SUBLANES, D_MODEL), lambda i: (jnp.minimum((i + 1) * hb, n_hb - 1), 0)),
            full(wup), full(cw), full(cb), full(wdn), full(ln_g), full(ln_b),
        ],
        out_specs=pl.BlockSpec((tm, D_MODEL), lambda i: (i, 0)),
        out_shape=jax.ShapeDtypeStruct((t, D_MODEL), F32),
        scratch_shapes=([pltpu.VMEM((tm + 2 * SUBLANES, D_MODEL), BF16)]
                        + [pltpu.VMEM((tm + 2 * SUBLANES, tf), F32)] * 4),
        compiler_params=pltpu.CompilerParams(
            dimension_semantics=("parallel",), vmem_limit_bytes=VMEM_LIMIT),
        name="ffn_ln2",
    )(x1, x1, x1, wup, cw, cb, wdn, ln_g, ln_b)


def _rotary_tables(seq):
    half = ROT_DIM // 2
    pos = np.arange(seq, dtype=np.float32)
    inv_freq = np.power(np.float32(ROPE_THETA), -np.arange(0, ROT_DIM, 2, dtype=np.float32) / np.float32(ROT_DIM))
    ang = (pos[:, None] * inv_freq[None, :].astype(np.float32)).astype(np.float32)
    cos, sin = np.cos(ang).astype(np.float32), np.sin(ang).astype(np.float32)
    pad = np.zeros((seq, HEAD_DIM_A - ROT_DIM), np.float32)
    zer = np.zeros((seq, half), np.float32)
    c64 = np.concatenate([cos, cos, pad + 1.0], axis=1)
    sa64 = np.concatenate([-sin, zer, pad], axis=1)
    sb64 = np.concatenate([zer, sin, pad], axis=1)
    tile2 = lambda a: np.concatenate([a, a], axis=1)
    return jnp.asarray(np.stack([tile2(c64), tile2(sa64), tile2(sb64)]))


def _group_rows(p, n_chunks):
    nh = 2 * HEADS_PER_GROUP
    pg = p.reshape(2, N_GROUPS, HEADS_PER_GROUP).transpose(1, 0, 2).reshape(N_GROUPS, 1, nh)
    return jnp.broadcast_to(pg, (N_GROUPS, n_chunks, nh)).reshape(N_GROUPS * n_chunks * nh, 1)


def kernel(x, w_in, b_gate, lambda_q1, lambda_k1, lambda_q2, lambda_k2, attn_subln_w, conv_ssd_w, conv_ssd_b,
           dt_bias, a_log, d_skip, ssd_norm_w, w_proj_attn, w_proj_ssd, w_out, ln1_g, ln1_b, w_up, conv_ffn_w,
           conv_ffn_b, w_down, ln2_g, ln2_b):
    batch, seq, _ = x.shape
    t = batch * seq
    l = 0
    xbc_w = D_INNER + 2 * N_GROUPS * D_STATE
    dt_off = Z_OFF + D_INNER + xbc_w
    w = w_in[l]
    w_a = w.astype(BF16)
    w_b = w[:, dt_off + 2 * N_HEADS_S:].astype(BF16)
    w_dt = w[:, dt_off:dt_off + 2 * N_HEADS_S]
    w_dt_t = (w_dt.reshape(D_MODEL, 2, N_GROUPS, HEADS_PER_GROUP).transpose(2, 1, 3, 0)
              .reshape(2 * N_HEADS_S, D_MODEL).astype(BF16))
    rot_tab = _rotary_tables(seq)

    x2 = x.reshape(t, D_MODEL)
    proj2, dt_t = _in_proj(x2, w_a, w_b, w_dt_t, batch=batch, seq=seq)
    proj3 = proj2.reshape(batch, seq, PROJ_W)

    lam_p = jnp.stack([lambda_q1[l], lambda_k1[l], lambda_q2[l], lambda_k2[l]]).astype(F32)
    attn = _attention(proj3, rot_tab, lam_p, attn_subln_w[l].reshape(1, 2 * HEAD_DIM_A))

    cw = conv_ssd_w[l]
    cb = conv_ssd_b[l].reshape(1, xbc_w)
    nb = N_GROUPS * D_STATE
    ssd = _ssd(proj3, dt_t,
               cw[:, :D_INNER], cw[:, D_INNER:D_INNER + nb], cw[:, D_INNER + nb:],
               cb[:, :D_INNER], cb[:, D_INNER:D_INNER + nb], cb[:, D_INNER + nb:],
               _group_rows(dt_bias[l], seq // CHUNK), _group_rows(a_log[l], seq // CHUNK),
               jnp.repeat(d_skip[l], HEAD_DIM_S).reshape(1, D_INNER), ssd_norm_w[l].reshape(1, D_INNER))

    x1 = _merge(x2, attn.reshape(t, ATTN_W), ssd.reshape(t, D_INNER), proj2,
                w_proj_attn[l].astype(BF16), w_proj_ssd[l].astype(BF16), w_out[l].astype(BF16),
                b_gate[l].reshape(1, 2 * D_MODEL), ln1_g[l].reshape(1, D_MODEL), ln1_b[l].reshape(1, D_MODEL))

    out = _ffn(x1, w_up[l].astype(BF16), conv_ffn_w[l], conv_ffn_b[l].reshape(1, 2 * D_FF),
               w_down[l].astype(BF16), ln2_g[l].reshape(1, D_MODEL), ln2_b[l].reshape(1, D_MODEL), seq=seq)
    return out.reshape(batch, seq, D_MODEL)
```

```python
import functools
import math

import jax
import jax.numpy as jnp
import numpy as np
from jax import lax
from jax.experimental import pallas as pl
from jax.experimental.pallas import tpu as pltpu

F32 = jnp.float32
BF16 = jnp.bfloat16

D_MODEL = 1024
N_HEADS_A = 8
HEAD_DIM_A = 64
ROT_DIM = HEAD_DIM_A // 4
ROPE_THETA = 500000.0
ATTN_W = N_HEADS_A * 2 * HEAD_DIM_A
D_INNER = 2 * D_MODEL
HEAD_DIM_S = 64
N_HEADS_S = D_INNER // HEAD_DIM_S
N_GROUPS = 8
HEADS_PER_GROUP = N_HEADS_S // N_GROUPS
GROUP_W = HEADS_PER_GROUP * HEAD_DIM_S
D_STATE = 128
CONV_SSD = 5
CHUNK = 128
D_FF = 2816
DEPTH = 1
ALPHA = (2 * DEPTH) ** 0.25
LN_EPS = 1e-5
RMS_EPS = 1e-5
LAM_INIT = 0.8 - 0.6 * math.exp(-0.3 * 0)
LOG2_E = math.log2(math.e)

LANES = 128
SUBLANES = 8
CONV_STRIDE = 4
VMEM_LIMIT = 56 * 1024 * 1024

Q_OFF, K_OFF, V_OFF = 0, ATTN_W, 2 * ATTN_W
Z_OFF = 3 * ATTN_W
XS_OFF = Z_OFF + D_INNER
B_OFF = XS_OFF + D_INNER
C_OFF = B_OFF + N_GROUPS * D_STATE
GATE_OFF = C_OFF + N_GROUPS * D_STATE
PROJ_W = GATE_OFF + 2 * D_MODEL


def _dot(a, b):
    return jnp.dot(a, b, preferred_element_type=F32)


def _dot_nt(a, b):
    return lax.dot_general(a, b, (((1,), (1,)), ((), ())), preferred_element_type=F32)


def _dot_exact(a, b):
    return jnp.dot(a, b, preferred_element_type=F32, precision=lax.Precision.HIGHEST)


def _in_proj_kernel(x_ref, wa_ref, wb_ref, wdt_ref, o_ref, dt_ref, xb_ref, *, n_q, n_a):
    j = pl.program_id(1)

    @pl.when(j == 0)
    def _():
        xb = x_ref[...].astype(BF16)
        xb_ref[...] = xb
        dt_t = _dot_nt(wdt_ref[...], xb)
        for c in range(dt_ref.shape[0]):
            dt_ref[c] = dt_t[:, c * CHUNK:(c + 1) * CHUNK]

    @pl.when(j < n_q)
    def _():
        o_ref[...] = (_dot(xb_ref[...], wa_ref[...]) * (HEAD_DIM_A ** -0.5 * LOG2_E)).astype(o_ref.dtype)

    @pl.when((j >= n_q) & (j < n_a))
    def _():
        o_ref[...] = _dot(xb_ref[...], wa_ref[...]).astype(o_ref.dtype)

    @pl.when(j >= n_a)
    def _():
        o_ref[...] = _dot(xb_ref[...], wb_ref[...]).astype(o_ref.dtype)


def _in_proj(x2, w_a, w_b, w_dt_t, *, batch, seq, tm=2048, tn=1024):
    t = x2.shape[0]
    n_seq_tiles = seq // tm
    n_dt = w_dt_t.shape[0]
    n_a = GATE_OFF // tn
    kern = functools.partial(_in_proj_kernel, n_q=ATTN_W // tn, n_a=n_a)
    return pl.pallas_call(
        kern,
        grid=(t // tm, PROJ_W // tn),
        in_specs=[
            pl.BlockSpec((tm, D_MODEL), lambda i, j: (i, 0)),
            pl.BlockSpec((D_MODEL, tn), lambda i, j: (0, jnp.minimum(j, n_a - 1))),
            pl.BlockSpec((D_MODEL, tn), lambda i, j: (0, jnp.maximum(j - n_a, 0))),
            pl.BlockSpec((n_dt, D_MODEL), lambda i, j: (0, 0)),
        ],
        out_specs=[
            pl.BlockSpec((tm, tn), lambda i, j: (i, j)),
            pl.BlockSpec((None, tm // CHUNK, n_dt, CHUNK),
                         lambda i, j: (i // n_seq_tiles, i % n_seq_tiles, 0, 0)),
        ],
        out_shape=[
            jax.ShapeDtypeStruct((t, PROJ_W), BF16),
            jax.ShapeDtypeStruct((batch, seq // CHUNK, n_dt, CHUNK), F32),
        ],
        scratch_shapes=[pltpu.VMEM((tm, D_MODEL), BF16)],
        compiler_params=pltpu.CompilerParams(
            dimension_semantics=("parallel", "arbitrary"), vmem_limit_bytes=VMEM_LIMIT),
        name="in_proj",
    )(x2, w_a, w_b, w_dt_t)


def _rotary(a_ref, tab_ref, rows):
    a = a_ref[rows, :].astype(F32)
    up = pltpu.roll(a, LANES - ROT_DIM // 2, axis=1)
    dn = pltpu.roll(a, ROT_DIM // 2, axis=1)
    return (a * tab_ref[0, rows, :] + up * tab_ref[1, rows, :] + dn * tab_ref[2, rows, :]).astype(BF16)


def _attn_kernel(q_ref, k_ref, v_ref, tq_ref, tk_ref, lam_ref, sw_ref, o_ref, kr_ref, v1_ref, s1_ref, s2_ref, *, tk):
    tq = q_ref.shape[0]
    seq = k_ref.shape[0]
    n_kc = seq // tk

    @pl.when(pl.program_id(2) == 0)
    def _():
        v1_ref[:, :LANES] = v_ref[...]
        v1_ref[:, LANES:] = jnp.ones((seq, LANES), BF16)
        for c in range(n_kc):
            rows = slice(c * tk, (c + 1) * tk)
            kr_ref[rows, :] = _rotary(k_ref, tk_ref, rows)

    q = _rotary(q_ref, tq_ref, slice(None))
    lane = lax.broadcasted_iota(jnp.int32, q.shape, 1)
    zero = jnp.zeros_like(q)
    q1 = jnp.where(lane < HEAD_DIM_A, q, zero)
    q2 = jnp.where(lane >= HEAD_DIM_A, q, zero)

    m1 = jnp.full((tq, LANES), -jnp.inf, F32)
    m2 = m1
    for c in range(n_kc):
        kc = kr_ref[c * tk:(c + 1) * tk, :]
        sc1 = _dot_nt(q1, kc)
        sc2 = _dot_nt(q2, kc)
        s1_ref[c] = sc1
        s2_ref[c] = sc2
        for u in range(tk // LANES):
            m1 = jnp.maximum(m1, sc1[:, u * LANES:(u + 1) * LANES])
            m2 = jnp.maximum(m2, sc2[:, u * LANES:(u + 1) * LANES])
    m1 = jnp.max(m1, axis=1, keepdims=True)
    m2 = jnp.max(m2, axis=1, keepdims=True)

    a1 = jnp.zeros((tq, 2 * LANES), F32)
    a2 = a1
    for c in range(n_kc):
        vc = v1_ref[c * tk:(c + 1) * tk, :]
        p1 = jnp.exp2((s1_ref[c] - m1).astype(BF16))
        p2 = jnp.exp2((s2_ref[c] - m2).astype(BF16))
        a1 = a1 + _dot(p1, vc)
        a2 = a2 + _dot(p2, vc)

    lp = lam_ref[...]
    lam = (jnp.exp(jnp.sum(lp[0:1] * lp[1:2], axis=1, keepdims=True))
           - jnp.exp(jnp.sum(lp[2:3] * lp[3:4], axis=1, keepdims=True)) + LAM_INIT)
    o =a1[:, :LANES] / a1[:, LANES:] - lam * (a2[:, :LANES] / a2[:, LANES:])
    ms = jnp.mean(o * o, axis=1, keepdims=True)
    o = o * lax.rsqrt(ms + RMS_EPS) * sw_ref[...] * (1.0 - LAM_INIT)
    o_ref[...] = o.astype(o_ref.dtype)


def _attention(proj3, rot_tab, lam_p, subln_w, *, tq=1024, tk=512):
    batch, seq, _ = proj3.shape
    hw = 2 * HEAD_DIM_A
    kern = functools.partial(_attn_kernel, tk=tk)
    return pl.pallas_call(
        kern,
        grid=(batch, N_HEADS_A, seq // tq),
        in_specs=[
            pl.BlockSpec((None, tq, hw), lambda b, h, i: (b, i, Q_OFF // hw + h)),
            pl.BlockSpec((None, seq, hw), lambda b, h, i: (b, 0, K_OFF // hw + h)),
            pl.BlockSpec((None, seq, hw), lambda b, h, i: (b, 0, V_OFF // hw + h)),
            pl.BlockSpec((3, tq, hw), lambda b, h, i: (0, i, 0)),
            pl.BlockSpec((3, seq, hw), lambda b, h, i: (0, 0, 0), pipeline_mode=pl.Buffered(1)),
            pl.BlockSpec((4, HEAD_DIM_A), lambda b, h, i: (0, 0)),
            pl.BlockSpec((1, hw), lambda b, h, i: (0, 0)),
        ],
        out_specs=pl.BlockSpec((None, tq, hw), lambda b, h, i: (b, i, h)),
        out_shape=jax.ShapeDtypeStruct((batch, seq, ATTN_W), BF16),
        scratch_shapes=[
            pltpu.VMEM((seq, hw), BF16),
            pltpu.VMEM((seq, 2 * LANES), BF16),
            pltpu.VMEM((seq // tk, tq, tk), F32),
            pltpu.VMEM((seq // tk, tq, tk), F32),
        ],
        compiler_params=pltpu.CompilerParams(
            dimension_semantics=("parallel", "parallel", "arbitrary"), vmem_limit_bytes=VMEM_LIMIT),
        name="diff_attn",
    )(proj3, proj3, proj3, rot_tab, rot_tab, lam_p, subln_w)


def _silu(v):
    return v * jax.nn.sigmoid(v)


def _ssd_kernel(z_ref, xs_ref, b_ref, c_ref, dt_ref, cwx_ref, cwb_ref, cwc_ref, cbx_ref, cbb_ref, cbc_ref,
                dtb_ref, alog_ref, dsk_ref, nw_ref, o_ref,
                pad_ref, cv_ref, y_ref, cb_ref, src_ref, cs_ref, st_ref):
    seq = xs_ref.shape[0]
    n_chunks = seq // CHUNK
    halo = SUBLANES
    half = CONV_SSD // 2
    gw = GROUP_W
    w_all = gw + 2 * D_STATE

    n_slab = w_all // LANES
    srcs = (xs_ref, xs_ref, b_ref, c_ref)
    lane0 = (0, LANES, 0, 0)
    zero_halo = jnp.zeros((halo, LANES), F32)
    for s in range(n_slab):
        pad_ref[s, 0:halo, :] = zero_halo
        pad_ref[s, halo + seq:, :] = zero_halo
        pad_ref[s, halo:halo + seq, :] = srcs[s][:, lane0[s]:lane0[s] + LANES].astype(F32)
    cw = jnp.concatenate([cwx_ref[...], cwb_ref[...], cwc_ref[...]], axis=1)
    cbias = jnp.concatenate([cbx_ref[...], cbb_ref[...], cbc_ref[...]], axis=1)
    stride = CONV_STRIDE
    blk = SUBLANES * stride

    def conv_chunk(c, carry):
        t0 = pl.multiple_of(c * CHUNK, CHUNK)
        for s in range(n_slab):
            w_s = cw[:, s * LANES:(s + 1) * LANES]
            b_s = jnp.broadcast_to(cbias[:, s * LANES:(s + 1) * LANES], (SUBLANES, LANES))
            for r in range(0, CHUNK, blk):
                for v in range(stride):
                    base = t0 + (r + v)
                    acc = b_s
                    for k in range(CONV_SSD):
                        acc = acc + pad_ref[s, pl.ds(base + (halo - half + k), SUBLANES, stride=stride), :] * w_s[k:k + 1, :]
                    cv_ref[s, pl.ds(base, SUBLANES, stride=stride), :] = _silu(acc)
        rows = pl.ds(t0, CHUNK)
        cb_ref[c] = _dot_nt(cv_ref[3, rows, :].astype(BF16), cv_ref[2, rows, :].astype(BF16))
        return carry

    lax.fori_loop(0, n_chunks, conv_chunk, 0, unroll=8)

    nh = 2 * HEADS_PER_GROUP
    r_i = lax.broadcasted_iota(jnp.int32, (CHUNK, CHUNK), 0)
    c_i = lax.broadcasted_iota(jnp.int32, (CHUNK, CHUNK), 1)
    upper = (r_i <= c_i).astype(F32)
    lower = (r_i >= c_i).astype(F32)
    raw = dt_ref[...].reshape(n_chunks * nh, CHUNK) + dtb_ref[...]
    dt_all = jnp.maximum(raw, 0.0) + jnp.log1p(jnp.exp(-jnp.abs(raw)))
    da = dt_all * (-jnp.exp(alog_ref[...]))
    row_all = lax.broadcasted_iota(jnp.int32, (n_chunks * nh, CHUNK), 0)
    cs_all = jnp.where(row_all % nh < HEADS_PER_GROUP, _dot_exact(da, upper), _dot_exact(da, lower))
    cs2_all = cs_all * LOG2_E
    cs_ref[...] = cs2_all.reshape(n_chunks, nh, CHUNK)
    src_ref[...] = (cs2_all - jnp.log2(dt_all)).reshape(n_chunks, nh, CHUNK)

    lane_head = lax.broadcasted_iota(jnp.int32, (1, gw), 1) // HEAD_DIM_S
    first_head = lax.broadcasted_iota(jnp.int32, (1, LANES), 1) < HEAD_DIM_S

    def chunk_step(c, reverse):
        r0 = HEADS_PER_GROUP if reverse else 0
        t0 = pl.multiple_of(c * CHUNK, CHUNK)
        rows = pl.ds(t0, CHUNK)
        x = jnp.concatenate([cv_ref[0, rows, :], cv_ref[1, rows, :]], axis=1)
        bm = cv_ref[2, rows, :]
        cm = cv_ref[3, rows, :].astype(BF16)
        cs2 = cs_ref[c]
        src = src_ref[c]
        cbm = cb_ref[c]
        if reverse:
            tot = cs2[:, 0:1]
            keep = r_i <= c_i
        else:
            tot = cs2[:, CHUNK - 1:CHUNK]
            keep = r_i >= c_i
        cs2_t = jnp.transpose(cs2)
        col_b = [jnp.broadcast_to(cs2_t[:, r0 + e:r0 + e + 1], (CHUNK, LANES)) for e in range(HEADS_PER_GROUP)]
        g_out = jnp.exp2(jnp.concatenate(
            [jnp.where(first_head, col_b[2 * v], col_b[2 * v + 1]) for v in range(gw // LANES)], axis=1))
        sel_row = (lax.broadcasted_iota(jnp.int32, (nh, gw), 0) == lane_head + r0).astype(F32)
        cdec = jnp.exp2(jnp.sum(tot * sel_row, axis=0, keepdims=True))
        wrow = jnp.exp2(tot - src)
        bt = jnp.transpose(bm)
        xb = x.astype(BF16)
        zero = jnp.zeros_like(xb)
        ms, bs, xs = [], [], []
        for e in range(HEADS_PER_GROUP):
            r = r0 + e
            seg = col_b[e] - src[r:r + 1, :]
            decay = jnp.exp2(jnp.where(keep, seg, -jnp.inf))
            ms.append((cbm * decay).astype(BF16))
            bs.append((bt * wrow[r:r + 1, :]).astype(BF16))
            xs.append(jnp.where(lane_head == e, xb, zero))
        xcat = jnp.concatenate(xs, axis=0)
        d = 1 if reverse else 0
        st = st_ref[d]
        y = _dot(jnp.concatenate(ms, axis=1), xcat) + _dot(cm, st.astype(BF16)) * g_out
        st_ref[d] = st * cdec + _dot(jnp.concatenate(bs, axis=1), xcat)
        y_ref[d, pl.ds(t0, CHUNK), :] = y

    st_ref[...] = jnp.zeros_like(st_ref)

    def scan(i, carry):
        chunk_step(i, False)
        chunk_step(n_chunks - 1 - i, True)
        return carry

    lax.fori_loop(0, n_chunks, scan, 0, unroll=32)

    def finish(c, carry):
        t0 = pl.multiple_of(c * CHUNK, CHUNK)
        rows = pl.ds(t0, CHUNK)
        xc = jnp.concatenate([cv_ref[0, rows, :], cv_ref[1, rows, :]], axis=1)
        y = y_ref[0, rows, :] + y_ref[1, rows, :] + xc * dsk_ref[...]
        y = y * _silu(z_ref[rows, :].astype(F32))
        ms = jnp.mean(y * y, axis=1, keepdims=True)
        y = y * lax.rsqrt(ms + RMS_EPS) * nw_ref[...]
        o_ref[rows, :] = y.astype(o_ref.dtype)
        return carry

    lax.fori_loop(0, n_chunks, finish, 0, unroll=8)


def _ssd(proj3, dt_t, cw_x, cw_b, cw_c, cb_x, cb_b, cb_c, dtb, alog, dsk, nw):
    batch, seq, _ = proj3.shape
    gw, ns = GROUP_W, D_STATE
    nh = 2 * HEADS_PER_GROUP
    n_chunks = seq // CHUNK
    gspec = lambda w: pl.BlockSpec((1, w), lambda b, g: (0, g))
    return pl.pallas_call(
        _ssd_kernel,
        grid=(batch, N_GROUPS),
        in_specs=[
            pl.BlockSpec((None, seq, gw), lambda b, g: (b, 0, Z_OFF // gw + g)),
            pl.BlockSpec((None, seq, gw), lambda b, g: (b, 0, XS_OFF // gw + g)),
            pl.BlockSpec((None, seq, ns), lambda b, g: (b, 0, B_OFF // ns + g)),
            pl.BlockSpec((None, seq, ns), lambda b, g: (b, 0, C_OFF // ns + g)),
            pl.BlockSpec((None, n_chunks, nh, CHUNK), lambda b, g: (b, 0, g, 0)),
            pl.BlockSpec((CONV_SSD, gw), lambda b, g: (0, g)),
            pl.BlockSpec((CONV_SSD, ns), lambda b, g: (0, g)),
            pl.BlockSpec((CONV_SSD, ns), lambda b, g: (0, g)),
            gspec(gw), gspec(ns), gspec(ns),
            pl.BlockSpec((n_chunks * nh, 1), lambda b, g: (g, 0)),
            pl.BlockSpec((n_chunks * nh, 1), lambda b, g: (g, 0)),
            gspec(gw), gspec(gw),
        ],
        out_specs=pl.BlockSpec((None, seq, gw), lambda b, g: (b, 0, g)),
        out_shape=jax.ShapeDtypeStruct((batch, seq, D_INNER), BF16),
        scratch_shapes=[
            pltpu.VMEM(((gw + 2 * ns) // LANES, seq + 2 * SUBLANES, LANES), F32),
            pltpu.VMEM(((gw + 2 * ns) // LANES, seq, LANES), F32),
            pltpu.VMEM((2, seq, gw), F32),
            pltpu.VMEM((n_chunks, CHUNK, CHUNK), F32),
            pltpu.VMEM((n_chunks, nh, CHUNK), F32),
            pltpu.VMEM((n_chunks, nh, CHUNK), F32),
            pltpu.VMEM((2, ns, gw), F32),
        ],
        compiler_params=pltpu.CompilerParams(
            dimension_semantics=("parallel", "parallel"), vmem_limit_bytes=VMEM_LIMIT),
        name="ssd",
    )(proj3, proj3, proj3, proj3, dt_t, cw_x, cw_b, cw_c, cb_x, cb_b, cb_c, dtb, alog, dsk, nw)


def _layer_norm(v, g, b):
    mu = jnp.mean(v, axis=1, keepdims=True)
    d = v - mu
    var = jnp.mean(d * d, axis=1, keepdims=True)
    return d * lax.rsqrt(var + LN_EPS) * g + b


def _merge_kernel(x_ref, at_ref, sd_ref, ga_ref, gs_ref, wpa_ref, wps_ref, wo_ref, bg_ref, g_ref, b_ref, o_ref):
    bg = bg_ref[...]
    g_a = jax.nn.sigmoid(ga_ref[...].astype(F32) + bg[:, :D_MODEL])
    g_s = jax.nn.sigmoid(gs_ref[...].astype(F32) + bg[:, D_MODEL:])
    merged = g_a * _dot(at_ref[...], wpa_ref[...]) + g_s * _dot(sd_ref[...], wps_ref[...])
    y = ALPHA * x_ref[...] + _dot(merged.astype(BF16), wo_ref[...])
    o_ref[...] = _layer_norm(y, g_ref[...], b_ref[...])


def _merge(x2, attn2, ssd2, proj2, wpa, wps, wo, bg, ln_g, ln_b, *, tm=512):
    t = x2.shape[0]
    full = lambda a: pl.BlockSpec(a.shape, lambda i: (0, 0))
    return pl.pallas_call(
        _merge_kernel,
        grid=(t // tm,),
        in_specs=[
            pl.BlockSpec((tm, D_MODEL), lambda i: (i, 0)),
            pl.BlockSpec((tm, ATTN_W), lambda i: (i, 0)),
            pl.BlockSpec((tm, D_INNER), lambda i: (i, 0)),
            pl.BlockSpec((tm, D_MODEL), lambda i: (i, GATE_OFF // D_MODEL)),
            pl.BlockSpec((tm, D_MODEL), lambda i: (i, GATE_OFF // D_MODEL + 1)),
            full(wpa), full(wps), full(wo), full(bg), full(ln_g), full(ln_b),
        ],
        out_specs=pl.BlockSpec((tm, D_MODEL), lambda i: (i, 0)),
        out_shape=jax.ShapeDtypeStruct((t, D_MODEL), F32),
        compiler_params=pltpu.CompilerParams(
            dimension_semantics=("parallel",), vmem_limit_bytes=VMEM_LIMIT),
        name="merge_ln1",
    )(x2, attn2, ssd2, proj2, proj2, wpa, wps, wo, bg, ln_g, ln_b)


def _ffn_kernel(x_ref, prev_ref, next_ref, wup_ref, cw_ref, cb_ref, wdn_ref, g_ref, b_ref, o_ref, xh_ref,
                ug0_ref, uu0_ref, ug1_ref, uu1_ref, *, tiles_per_seq, tf):
    i = pl.program_id(0)
    tm = x_ref.shape[0]
    halo = SUBLANES
    x = x_ref[...]
    first = (i % tiles_per_seq) == 0
    last = (i % tiles_per_seq) == tiles_per_seq - 1
    zero = jnp.zeros((halo, D_MODEL), F32)
    xh_ref[0:halo, :] = jnp.where(first, zero, prev_ref[...]).astype(BF16)
    xh_ref[halo:halo + tm, :] = x.astype(BF16)
    xh_ref[halo + tm:, :] = jnp.where(last, zero, next_ref[...]).astype(BF16)
    xh = xh_ref[...]

    def conv(u_ref, col):
        w = cw_ref[:, col:col + tf]
        out = cb_ref[:, col:col + tf] + u_ref[halo - 1:halo - 1 + tm, :] * w[0:1]
        out = out + u_ref[halo:halo + tm, :] * w[1:2]
        return out + u_ref[halo + 1:halo + 1 + tm, :] * w[2:3]

    bufs = ((ug0_ref, uu0_ref), (ug1_ref, uu1_ref))

    def up(j):
        ug_ref, uu_ref = bufs[j % 2]
        ug_ref[...] = _dot(xh, wup_ref[:, j * tf:(j + 1) * tf])
        uu_ref[...] = _dot(xh, wup_ref[:, D_FF + j * tf:D_FF + (j + 1) * tf])

    n_f = D_FF // tf
    acc = jnp.zeros((tm, D_MODEL), F32)
    up(0)
    for j in range(n_f):
        if j + 1 < n_f:
            up(j + 1)
        ug_ref, uu_ref = bufs[j % 2]
        act = (_silu(conv(ug_ref, j * tf)) * conv(uu_ref, D_FF + j * tf)).astype(BF16)
        acc = acc + _dot(act, wdn_ref[j * tf:(j + 1) * tf, :])
    o_ref[...] = _layer_norm(ALPHA * x + acc, g_ref[...], b_ref[...])


def _ffn(x1, wup, cw, cb, wdn, ln_g, ln_b, *, seq, tm=512, tf=256):
    t = x1.shape[0]
    hb = tm // SUBLANES
    n_hb = t // SUBLANES
    full = lambda a: pl.BlockSpec(a.shape, lambda i: (0, 0))
    kern = functools.partial(_ffn_kernel, tiles_per_seq=seq // tm, tf=tf)
    return pl.pallas_call(
        kern,
        grid=(t // tm,),
        in_specs=[
            pl.BlockSpec((tm, D_MODEL), lambda i: (i, 0)),
            pl.BlockSpec((SUBLANES, D_MODEL), lambda i: (jnp.maximum(i * hb - 1, 0), 0)),
            pl.BlockSpec((SUBLANES, D_MODEL), lambda i: (jnp.minimum((i + 1) * hb, n_hb - 1), 0)),
            full(wup), full(cw), full(cb), full(wdn), full(ln_g), full(ln_b),
        ],
        out_specs=pl.BlockSpec((tm, D_MODEL), lambda i: (i, 0)),
        out_shape=jax.ShapeDtypeStruct((t, D_MODEL), F32),
        scratch_shapes=([pltpu.VMEM((tm + 2 * SUBLANES, D_MODEL), BF16)]
                        + [pltpu.VMEM((tm + 2 * SUBLANES, tf), F32)] * 4),
        compiler_params=pltpu.CompilerParams(
            dimension_semantics=("parallel",), vmem_limit_bytes=VMEM_LIMIT),
        name="ffn_ln2",
    )(x1, x1, x1, wup, cw, cb, wdn, ln_g, ln_b)


def _rotary_tables(seq):
    half = ROT_DIM // 2
    pos = np.arange(seq, dtype=np.float32)
    inv_freq = np.power(np.float32(ROPE_THETA), -np.arange(0, ROT_DIM, 2, dtype=np.float32) / np.float32(ROT_DIM))
    ang = (pos[:, None] * inv_freq[None, :].astype(np.float32)).astype(np.float32)
    cos, sin = np.cos(ang).astype(np.float32), np.sin(ang).astype(np.float32)
    pad = np.zeros((seq, HEAD_DIM_A - ROT_DIM), np.float32)
    zer = np.zeros((seq, half), np.float32)
    c64 = np.concatenate([cos, cos, pad + 1.0], axis=1)
    sa64 = np.concatenate([-sin, zer, pad], axis=1)
    sb64 = np.concatenate([zer, sin, pad], axis=1)
    tile2 = lambda a: np.concatenate([a, a], axis=1)
    return jnp.asarray(np.stack([tile2(c64), tile2(sa64), tile2(sb64)]))


def _group_rows(p, n_chunks):
    nh = 2 * HEADS_PER_GROUP
    pg = p.reshape(2, N_GROUPS, HEADS_PER_GROUP).transpose(1, 0, 2).reshape(N_GROUPS, 1, nh)
    return jnp.broadcast_to(pg, (N_GROUPS, n_chunks, nh)).reshape(N_GROUPS * n_chunks * nh, 1)


def kernel(x, w_in, b_gate, lambda_q1, lambda_k1, lambda_q2, lambda_k2, attn_subln_w, conv_ssd_w, conv_ssd_b,
           dt_bias, a_log, d_skip, ssd_norm_w, w_proj_attn, w_proj_ssd, w_out, ln1_g, ln1_b, w_up, conv_ffn_w,
           conv_ffn_b, w_down, ln2_g, ln2_b):
    batch, seq, _ = x.shape
    t = batch * seq
    l = 0
    xbc_w = D_INNER + 2 * N_GROUPS * D_STATE
    dt_off = Z_OFF + D_INNER + xbc_w
    w = w_in[l]
    w_a = w.astype(BF16)
    w_b = w[:, dt_off + 2 * N_HEADS_S:].astype(BF16)
    w_dt = w[:, dt_off:dt_off + 2 * N_HEADS_S]
    w_dt_t = (w_dt.reshape(D_MODEL, 2, N_GROUPS, HEADS_PER_GROUP).transpose(2, 1, 3, 0)
              .reshape(2 * N_HEADS_S, D_MODEL).astype(BF16))
    rot_tab = _rotary_tables(seq)

    x2 = x.reshape(t, D_MODEL)
    proj2, dt_t = _in_proj(x2, w_a, w_b, w_dt_t, batch=batch, seq=seq)
    proj3 = proj2.reshape(batch, seq, PROJ_W)

    lam_p = jnp.stack([lambda_q1[l], lambda_k1[l], lambda_q2[l], lambda_k2[l]]).astype(F32)
    attn = _attention(proj3, rot_tab, lam_p, attn_subln_w[l].reshape(1, 2 * HEAD_DIM_A))

    cw = conv_ssd_w[l]
    cb = conv_ssd_b[l].reshape(1, xbc_w)
    nb = N_GROUPS * D_STATE
    ssd = _ssd(proj3, dt_t,
               cw[:, :D_INNER], cw[:, D_INNER:D_INNER + nb], cw[:, D_INNER + nb:],
               cb[:, :D_INNER], cb[:, D_INNER:D_INNER + nb], cb[:, D_INNER + nb:],
               _group_rows(dt_bias[l], seq // CHUNK), _group_rows(a_log[l], seq // CHUNK),
               jnp.repeat(d_skip[l], HEAD_DIM_S).reshape(1, D_INNER), ssd_norm_w[l].reshape(1, D_INNER))

    x1 = _merge(x2, attn.reshape(t, ATTN_W), ssd.reshape(t, D_INNER), proj2,
                w_proj_attn[l].astype(BF16), w_proj_ssd[l].astype(BF16), w_out[l].astype(BF16),
                b_gate[l].reshape(1, 2 * D_MODEL), ln1_g[l].reshape(1, D_MODEL), ln1_b[l].reshape(1, D_MODEL))

    out = _ffn(x1, w_up[l].astype(BF16), conv_ffn_w[l], conv_ffn_b[l].reshape(1, 2 * D_FF),
               w_down[l].astype(BF16), ln2_g[l].reshape(1, D_MODEL), ln2_b[l].reshape(1, D_MODEL), seq=seq)
    return out.reshape(batch, seq, D_MODEL)
```
